```python
import math
import jax, jax.numpy as jnp
from jax import lax
import numpy as np

D_MODEL = 1024
BATCH = 4
SEQ = 4096
DEPTH = 2

N_MEM = 256
D_MIX = D_MODEL
NSA_HEADS = 8
NSA_KV_HEADS = 2
NSA_GROUP = NSA_HEADS // NSA_KV_HEADS
HEAD_DIM = 64
D_NSA = NSA_HEADS * HEAD_DIM
D_KV = NSA_KV_HEADS * HEAD_DIM
ROT_DIM = HEAD_DIM // 4
ROPE_THETA = 500000.0
CMP_BLOCK = 32
CMP_STRIDE = 16
CMP_HIDDEN = 256
SEL_BLOCK = 64
SEL_TOPN = 16
WINDOW = 512
Q_BLOCK = 128
D_CONV = D_MIX // 4
CONV_WIDTH = 31
D_POOL = D_MIX // 4
POOL_WINDOWS = (2, 4, 8, 16)
POOL_GROUP = D_POOL // len(POOL_WINDOWS)
IN_SPLITS = (D_NSA,) + (D_KV,) * 6 + (3 * NSA_HEADS, D_CONV, D_CONV, D_POOL)
IN_OFFSETS = tuple(int(v) for v in np.cumsum(IN_SPLITS)[:-1])
D_IN = sum(IN_SPLITS)
XA_HEADS = 4
XA_HEAD_DIM = D_MODEL // XA_HEADS
PEER_HEADS = 8
PEER_KEYS = 128
PEER_EXPERTS = PEER_KEYS ** 2
PEER_QDIM = 256
PEER_HALF = PEER_QDIM // 2
PEER_TOPK = 16
TOKEN_CHUNK = 128
EPS = 1e-6
NEG = -1e30
FORCE = 1e4

kernel_name = 'hymba_style_nsa_conv_pool_peer_block'


def _rmsnorm(x, g):
    xf = x.astype(jnp.float32)
    y = xf * lax.rsqrt(jnp.mean(xf * xf, axis=-1, keepdims=True) + EPS)
    return (y * g.astype(jnp.float32)).astype(x.dtype)


def _layernorm(x, g, b):
    xf = x.astype(jnp.float32)
    mu = jnp.mean(xf, axis=-1, keepdims=True)
    xc = xf - mu
    y = xc * lax.rsqrt(jnp.mean(xc * xc, axis=-1, keepdims=True) + EPS)
    return (y * g.astype(jnp.float32) + b.astype(jnp.float32)).astype(x.dtype)


def _rope(x, pos):
    half = ROT_DIM // 2
    freqs = ROPE_THETA ** (-jnp.arange(half, dtype=jnp.float32) * 2.0 / ROT_DIM)
    ang = pos.astype(jnp.float32)[:, :, None, None] * freqs
    cos, sin = jnp.cos(ang), jnp.sin(ang)
    xf = x.astype(jnp.float32)
    x1, x2, rest = xf[..., :half], xf[..., half:ROT_DIM], xf[..., ROT_DIM:]
    out = jnp.concatenate([x1 * cos - x2 * sin, x1 * sin + x2 * cos, rest], axis=-1)
    return out.astype(x.dtype)


def _nsa(q, kc, vc, ks, vs, kw, vw, gates, pos, cmp_pe, cmp_w1, cmp_w2):
    B, S = q.shape[0], q.shape[1]
    scale = HEAD_DIM ** -0.5
    q, kc, ks, kw = _rope(q, pos), _rope(kc, pos), _rope(ks, pos), _rope(kw, pos)
    qg = q.reshape(B, S, NSA_KV_HEADS, NSA_GROUP, HEAD_DIM)
    tpos_all = jnp.arange(S)

    n_cmp = (S - CMP_BLOCK) // CMP_STRIDE + 1
    cidx = jnp.arange(n_cmp)[:, None] * CMP_STRIDE + jnp.arange(CMP_BLOCK)[None, :]

    def compress(t, pe, w1, w2):
        blk = t[:, cidx] + pe[None, None, :, None, :]
        blk = jnp.moveaxis(blk, 3, 2).reshape(B, n_cmp, NSA_KV_HEADS, CMP_BLOCK * HEAD_DIM)
        return jax.nn.gelu(blk @ w1) @ w2

    k_cmp = compress(kc, cmp_pe[0], cmp_w1[0], cmp_w2[0])
    v_cmp = compress(vc, cmp_pe[1], cmp_w1[1], cmp_w2[1])
    s_cmp = jnp.einsum('bskgd,bnkd->bkgsn', qg, k_cmp).astype(jnp.float32) * scale
    cmp_end = jnp.arange(n_cmp) * CMP_STRIDE + CMP_BLOCK - 1
    valid_cmp = cmp_end[None, :] <= tpos_all[:, None]
    p_cmp = jnp.where(valid_cmp, jax.nn.softmax(jnp.where(valid_cmp, s_cmp, NEG), axis=-1), 0.0)
    o_cmp = jnp.einsum('bkgsn,bnkd->bskgd', p_cmp.astype(v_cmp.dtype), v_cmp)

    n_sel = S // SEL_BLOCK
    n_top = min(SEL_TOPN, n_sel)
    ci = jnp.arange(n_cmp) * CMP_STRIDE
    sj = jnp.arange(n_sel) * SEL_BLOCK
    overlap = ((ci[:, None] < sj[None, :] + SEL_BLOCK) & (ci[:, None] + CMP_BLOCK > sj[None, :])).astype(jnp.float32)
    imp = jnp.einsum('bkgsn,nj->bksj', p_cmp, overlap)
    blk_id = jnp.arange(n_sel)[None, :]
    cur = (tpos_all // SEL_BLOCK)[:, None]
    forced = (blk_id == 0) | (blk_id == cur) | (blk_id == cur - 1)
    causal_blk = blk_id * SEL_BLOCK <= tpos_all[:, None]
    imp = jnp.where(causal_blk, jnp.where(forced, FORCE, imp), NEG)
    _, sel_idx = lax.top_k(imp, n_top)

    ks_blocks = jnp.transpose(ks.reshape(B, n_sel, SEL_BLOCK, NSA_KV_HEADS, HEAD_DIM), (0, 3, 1, 2, 4))
    vs_blocks = jnp.transpose(vs.reshape(B, n_sel, SEL_BLOCK, NSA_KV_HEADS, HEAD_DIM), (0, 3, 1, 2, 4))
    gather = jax.vmap(jax.vmap(lambda kb, ix: kb[ix]))
    kw_pad = jnp.pad(kw, ((0, 0), (WINDOW, 0), (0, 0), (0, 0)))
    vw_pad = jnp.pad(vw, ((0, 0), (WINDOW, 0), (0, 0), (0, 0)))

    def q_block(qb):
        t0 = qb * Q_BLOCK
        tpos = t0 + jnp.arange(Q_BLOCK)
        qblk = lax.dynamic_slice_in_dim(qg, t0, Q_BLOCK, axis=1)
        sidx = lax.dynamic_slice_in_dim(sel_idx, t0, Q_BLOCK, axis=2)
        kg = gather(ks_blocks, sidx)
        vg = gather(vs_blocks, sidx).reshape(B, NSA_KV_HEADS, Q_BLOCK, n_top * SEL_BLOCK, HEAD_DIM)
        kpos = sidx[..., None] * SEL_BLOCK + jnp.arange(SEL_BLOCK)
        msel = (kpos <= tpos[None, None, :, None, None]).reshape(B, NSA_KV_HEADS, 1, Q_BLOCK, n_top * SEL_BLOCK)
        s = jnp.einsum('btkgd,bktnld->bkgtnl', qblk, kg).astype(jnp.float32) * scale
        s = s.reshape(B, NSA_KV_HEADS, NSA_GROUP, Q_BLOCK, n_top * SEL_BLOCK)
        p = jax.nn.softmax(jnp.where(msel, s, NEG), axis=-1).astype(vg.dtype)
        o_sel = jnp.einsum('bkgtm,bktmd->btkgd', p, vg)
        kwb = lax.dynamic_slice_in_dim(kw_pad, t0, WINDOW + Q_BLOCK, axis=1)
        vwb = lax.dynamic_slice_in_dim(vw_pad, t0, WINDOW + Q_BLOCK, axis=1)
        kp = t0 - WINDOW + jnp.arange(WINDOW + Q_BLOCK)
        mw = (kp[None, :] >= 0) & (kp[None, :] <= tpos[:, None]) & (kp[None, :] > tpos[:, None] - WINDOW)
        sw = jnp.einsum('btkgd,bmkd->bkgtm', qblk, kwb).astype(jnp.float32) * scale
        pw = jax.nn.softmax(jnp.where(mw, sw, NEG), axis=-1).astype(vwb.dtype)
        o_win = jnp.einsum('bkgtm,bmkd->btkgd', pw, vwb)
        return o_sel, o_win

    o_sel, o_win = lax.map(q_block, jnp.arange(S // Q_BLOCK))
    o_sel = jnp.swapaxes(o_sel, 0, 1).reshape(B, S, NSA_KV_HEADS, NSA_GROUP, HEAD_DIM)
    o_win = jnp.swapaxes(o_win, 0, 1).reshape(B, S, NSA_KV_HEADS, NSA_GROUP, HEAD_DIM)
    g = jax.nn.sigmoid(gates.reshape(B, S, NSA_KV_HEADS, NSA_GROUP, 3))
    o = g[..., 0:1] * o_cmp + g[..., 1:2] * o_sel + g[..., 2:3] * o_win
    return o.reshape(B, S, D_NSA)


def _conformer_conv(a, b, conv_w, conv_b, ln_g, ln_b, pw_w, pw_b):
    h = a * jax.nn.sigmoid(b)
    h = jnp.pad(h, ((0, 0), (CONV_WIDTH - 1, 0), (0, 0)))
    h = lax.conv_general_dilated(h, conv_w[:, None, :].astype(h.dtype), window_strides=(1,), padding='VALID',
                                 dimension_numbers=('NWC', 'WIO', 'NWC'), feature_group_count=D_CONV) + conv_b
    h = jax.nn.silu(_layernorm(h, ln_g, ln_b))
    return h @ pw_w + pw_b


def _pool_mixer(h, pool_w, pool_scale):
    B, S = h.shape[0], h.shape[1]
    hf = h.astype(jnp.float32)
    c = jnp.concatenate([jnp.zeros((B, 1, D_POOL), jnp.float32), jnp.cumsum(hf, axis=1)], axis=1)
    t = jnp.arange(S)
    outs = []
    for gi, w in enumerate(POOL_WINDOWS):
        lo, hi = gi * POOL_GROUP, (gi + 1) * POOL_GROUP
        start = jnp.maximum(t + 1 - w, 0)
        cnt = (t + 1 - start).astype(jnp.float32)
        mean = (c[:, t + 1, lo:hi] - c[:, start, lo:hi]) / cnt[None, :, None]
        outs.append(mean - hf[:, :, lo:hi])
    m = jnp.stack(outs, axis=2).astype(h.dtype)
    y = jnp.einsum('bsgc,gcd->bsgd', m, pool_w).reshape(B, S, D_POOL)
    return y * pool_scale


def _cross_attention(h, memn, wq, wkv, wo):
    B, S = h.shape[0], h.shape[1]
    q = (h @ wq).reshape(B, S, XA_HEADS, XA_HEAD_DIM)
    kv = (memn @ wkv).reshape(B, memn.shape[1], 2, XA_HEADS, XA_HEAD_DIM)
    k, v = kv[:, :, 0], kv[:, :, 1]
    s = jnp.einsum('bshd,bmhd->bhsm', q, k).astype(jnp.float32) * (XA_HEAD_DIM ** -0.5)
    p = jax.nn.softmax(s, axis=-1).astype(v.dtype)
    o = jnp.einsum('bhsm,bmhd->bshd', p, v).reshape(B, S, D_MODEL)
    return o @ wo


def _peer(h, wq, sub_keys, u, v):
    B, S, D = h.shape
    T = B * S
    hf = h.reshape(T, D)
    q = (hf @ wq).reshape(T, PEER_HEADS, 2, PEER_HALF)
    s_half = jnp.einsum('thpc,pnc->thpn', q, sub_keys).astype(jnp.float32)
    v_half, i_half = lax.top_k(s_half, PEER_TOPK)
    cand = (v_half[:, :, 0, :, None] + v_half[:, :, 1, None, :]).reshape(T, PEER_HEADS, PEER_TOPK * PEER_TOPK)
    cand_idx = (i_half[:, :, 0, :, None] * PEER_KEYS + i_half[:, :, 1, None, :]).reshape(T, PEER_HEADS, PEER_TOPK * PEER_TOPK)
    top_s, top_pos = lax.top_k(cand, PEER_TOPK)
    expert_idx = jnp.take_along_axis(cand_idx, top_pos, axis=-1)
    gate = jax.nn.softmax(top_s, axis=-1).astype(h.dtype)
    n_chunks = T // TOKEN_CHUNK

    def chunk(args):
        hc, ic, gc = args
        act = jax.nn.gelu(jnp.einsum('cd,chkd->chk', hc, u[ic]))
        return jnp.einsum('chk,chkd->cd', gc * act, v[ic])

    out = lax.map(chunk, (hf.reshape(n_chunks, TOKEN_CHUNK, D),
                          expert_idx.reshape(n_chunks, TOKEN_CHUNK, PEER_HEADS, PEER_TOPK),
                          gate.reshape(n_chunks, TOKEN_CHUNK, PEER_HEADS, PEER_TOPK)))
    return out.reshape(B, S, D)


def setup_inputs(seed: int = 0) -> dict:
    key = jax.random.key(seed)
    ks = jax.random.split(key, 32)
    f32 = jnp.float32

    def nrm(k, shape, scale):
        return jax.random.normal(k, shape, f32) * scale

    def gain(k, shape):
        return 1.0 + 0.01 * jax.random.normal(k, shape, f32)

    L = DEPTH
    return {
        'x': nrm(ks[0], (BATCH, SEQ, D_MODEL), 1.0),
        'mem': nrm(ks[1], (BATCH, N_MEM, D_MODEL), 1.0),
        'positions': jax.random.randint(ks[2], (BATCH, 1), 0, 1024, jnp.int32) + jnp.arange(SEQ, dtype=jnp.int32)[None, :],
        'norm_mix_g': gain(ks[3], (L, D_MODEL)),
        'w_in': nrm(ks[4], (L, D_MODEL, D_IN), D_MODEL ** -0.5),
        'cmp_pe': nrm(ks[5], (L, 2, CMP_BLOCK, HEAD_DIM), 0.1),
        'cmp_w1': nrm(ks[6], (L, 2, CMP_BLOCK * HEAD_DIM, CMP_HIDDEN), (CMP_BLOCK * HEAD_DIM) ** -0.5),
        'cmp_w2': nrm(ks[7], (L, 2, CMP_HIDDEN, HEAD_DIM), CMP_HIDDEN ** -0.5),
        'conv_w': nrm(ks[8], (L, CONV_WIDTH, D_CONV), CONV_WIDTH ** -0.5),
        'conv_b': nrm(ks[9], (L, D_CONV), 0.01),
        'conv_ln_g': gain(ks[10], (L, D_CONV)),
        'conv_ln_b': nrm(ks[11], (L, D_CONV), 0.01),
        'conv_pw_w': nrm(ks[12], (L, D_CONV, D_CONV), D_CONV ** -0.5),
        'conv_pw_b': nrm(ks[13], (L, D_CONV), 0.01),
        'pool_w': nrm(ks[14], (L, len(POOL_WINDOWS), POOL_GROUP, POOL_GROUP), POOL_GROUP ** -0.5),
        'pool_scale': gain(ks[15], (L, D_POOL)),
        'w_out': nrm(ks[16], (L, D_MIX, D_MODEL), D_MIX ** -0.5),
        'norm_xa_g': gain(ks[17], (L, D_MODEL)),
        'norm_mem_g': gain(ks[18], (L, D_MODEL)),
        'xa_wq': nrm(ks[19], (L, D_MODEL, D_MODEL), D_MODEL ** -0.5),
        'xa_wkv': nrm(ks[20], (L, D_MODEL, 2 * D_MODEL), D_MODEL ** -0.5),
        'xa_wo': nrm(ks[21], (L, D_MODEL, D_MODEL), D_MODEL ** -0.5),
        'norm_ffn_g': gain(ks[22], (L, D_MODEL)),
        'peer_wq': nrm(ks[23], (L, D_MODEL, PEER_HEADS * PEER_QDIM), D_MODEL ** -0.5),
        'peer_subkeys': nrm(ks[24], (L, 2, PEER_KEYS, PEER_HALF), PEER_HALF ** -0.5),
        'peer_u': nrm(ks[25], (L, PEER_EXPERTS, D_MODEL), D_MODEL ** -0.5),
        'peer_v': nrm(ks[26], (L, PEER_EXPERTS, D_MODEL), PEER_HEADS ** -0.5),
        'final_g': gain(ks[27], (D_MODEL,)),
    }


def reference(x, mem, positions, norm_mix_g, w_in, cmp_pe, cmp_w1, cmp_w2, conv_w, conv_b, conv_ln_g, conv_ln_b,
              conv_pw_w, conv_pw_b, pool_w, pool_scale, w_out, norm_xa_g, norm_mem_g, xa_wq, xa_wkv, xa_wo,
              norm_ffn_g, peer_wq, peer_subkeys, peer_u, peer_v, final_g):
    B, S = x.shape[0], x.shape[1]
    for l in range(DEPTH):
        h = _rmsnorm(x, norm_mix_g[l])
        z = h @ w_in[l]
        (zq, zkc, zvc, zks, zvs, zkw, zvw, zg, za, zb, zp) = jnp.split(z, IN_OFFSETS, axis=-1)
        kv = lambda t: t.reshape(B, S, NSA_KV_HEADS, HEAD_DIM)
        o_nsa = _nsa(zq.reshape(B, S, NSA_HEADS, HEAD_DIM), kv(zkc), kv(zvc), kv(zks), kv(zvs), kv(zkw), kv(zvw),
                     zg, positions, cmp_pe[l], cmp_w1[l], cmp_w2[l])
        o_conv = _conformer_conv(za, zb, conv_w[l], conv_b[l], conv_ln_g[l], conv_ln_b[l], conv_pw_w[l], conv_pw_b[l])
        o_pool = _pool_mixer(zp, pool_w[l], pool_scale[l])
        x = x + jnp.concatenate([o_nsa, o_conv, o_pool], axis=-1) @ w_out[l]
        x = x + _cross_attention(_rmsnorm(x, norm_xa_g[l]), _rmsnorm(mem, norm_mem_g[l]), xa_wq[l], xa_wkv[l], xa_wo[l])
        x = x + _peer(_rmsnorm(x, norm_ffn_g[l]), peer_wq[l], peer_subkeys[l], peer_u[l], peer_v[l])
    return _rmsnorm(x, final_g)
```

```python
import functools
import math

import jax
import jax.numpy as jnp
import numpy as np
from jax import lax
from jax.experimental import pallas as pl
from jax.experimental.pallas import tpu as pltpu

F32 = jnp.float32
BF16 = jnp.bfloat16

NSA_HEADS = 8
NSA_KV_HEADS = 2
NSA_GROUP = NSA_HEADS // NSA_KV_HEADS
HEAD_DIM = 64
D_NSA = NSA_HEADS * HEAD_DIM
D_KV = NSA_KV_HEADS * HEAD_DIM
ROT_DIM = HEAD_DIM // 4
ROT_HALF = ROT_DIM // 2
ROPE_THETA = 500000.0
CMP_BLOCK = 32
CMP_STRIDE = 16
SEL_BLOCK = 64
SEL_TOPN = 16
WINDOW = 512
CONV_WIDTH = 31
POOL_WINDOWS = (2, 4, 8, 16)
XA_HEADS = 4
PEER_HEADS = 8
PEER_KEYS = 128
PEER_TOPK = 16
EPS = 1e-6
NEG = -1e30
FORCE = 1e4
LOWEST = -3.0e38

LANES = 128
VMEM_LIMIT = 48 * 1024 * 1024

NT_DIMS = (((1,), (1,)), ((), ()))


def _params(*sem):
    return pltpu.CompilerParams(dimension_semantics=sem, vmem_limit_bytes=VMEM_LIMIT)


def _full(shape):
    nd = len(shape)
    return pl.BlockSpec(shape, lambda *_: (0,) * nd)


def _rms(x, g):
    return x * lax.rsqrt(jnp.mean(x * x, axis=-1, keepdims=True) + EPS) * g


def _gelu(x):
    c = math.sqrt(2.0 / math.pi)
    return 0.5 * x * (1.0 + jnp.tanh(c * (x + 0.044715 * (x * x * x))))


def _dot(a, b):
    return jnp.dot(a, b, preferred_element_type=F32)


def _dot_nt(a, b):
    return lax.dot_general(a, b, NT_DIMS, preferred_element_type=F32)


N_ROPE_Q = D_NSA // LANES
N_KV_CHUNKS = 6
D_QKV = D_NSA + N_KV_CHUNKS * D_KV


def _inproj_kernel(x_ref, g_ref, w_ref, cos_ref, sa_ref, sb_ref, q_ref, kv_ref, misc_ref):
    y = _rms(x_ref[...], g_ref[...])
    z = _dot(y.astype(BF16), w_ref[...])
    cos, sa, sb = cos_ref[...], sa_ref[...], sb_ref[...]

    def rope(c):
        return c * cos + pltpu.roll(c, ROT_HALF, 1) * sa + pltpu.roll(c, LANES - ROT_HALF, 1) * sb

    scale = HEAD_DIM ** -0.5
    for j in range(N_ROPE_Q):
        q_ref[:, j * LANES:(j + 1) * LANES] = (rope(z[:, j * LANES:(j + 1) * LANES]) * scale).astype(BF16)
    for j in range(N_KV_CHUNKS):
        c = z[:, D_NSA + j * LANES:D_NSA + (j + 1) * LANES]
        if j % 2 == 0:
            c = rope(c)
        kv_ref[:, j * LANES:(j + 1) * LANES] = c.astype(BF16)
    misc_ref[...] = z[:, D_QKV:]


def _inproj(x2, g, w_pad, cos_t, sa_t, sb_t, tm):
    T, D = x2.shape
    n = w_pad.shape[1]
    n_misc = n - D_QKV
    row = lambda w: pl.BlockSpec((tm, w), lambda i: (i, 0))
    return pl.pallas_call(
        _inproj_kernel,
        grid=(T // tm,),
        in_specs=[row(D), _full((1, D)), _full((D, n)), row(LANES), row(LANES), row(LANES)],
        out_specs=[row(D_NSA), row(N_KV_CHUNKS * LANES), row(n_misc)],
        out_shape=[jax.ShapeDtypeStruct((T, D_NSA), BF16),
                   jax.ShapeDtypeStruct((T, N_KV_CHUNKS * LANES), BF16),
                   jax.ShapeDtypeStruct((T, n_misc), F32)],
        compiler_params=_params("parallel"),
    )(x2, g, w_pad, cos_t, sa_t, sb_t)


def _compress_kernel(x_ref, pe_ref, w1_ref, w2_ref, o_ref):
    x = x_ref[0, 0].astype(F32)
    pe = pe_ref[0]
    half = x.shape[1]
    a = _dot((x + pe[0:1]).astype(BF16), w1_ref[0, :half])
    b = _dot((x + pe[1:2]).astype(BF16), w1_ref[0, half:])
    n = x.shape[0]
    hid = _gelu(a + pltpu.roll(b, n - 1, 0))
    o_ref[0, 0] = _dot(hid.astype(BF16), w2_ref[0])


def _compress(xc, pe, w1, w2):
    two, nb, nch, width = xc.shape
    hid = w1.shape[-1]
    return pl.pallas_call(
        _compress_kernel,
        grid=(two, nb),
        in_specs=[pl.BlockSpec((1, 1, nch, width), lambda w, i: (w, i, 0, 0)),
                  pl.BlockSpec((1, 2, width), lambda w, i: (w, 0, 0)),
                  pl.BlockSpec((1, 2 * width, hid), lambda w, i: (w, 0, 0)),
                  pl.BlockSpec((1, hid, HEAD_DIM), lambda w, i: (w, 0, 0))],
        out_specs=pl.BlockSpec((1, 1, nch, HEAD_DIM), lambda w, i: (w, i, 0, 0)),
        out_shape=jax.ShapeDtypeStruct((two, nb, nch, HEAD_DIM), F32),
        compiler_params=_params("parallel", "parallel"),
    )(xc, pe, w1, w2)


def _nsa_kernel(q_ref, kcmp_ref, vcmp_ref, ks_ref, vs_ref, kw_ref, vw_ref, gate_ref, ov_ref, o_ref,
                m_sc, l_sc, acc_sc, *, tq, kc_sel, kc_win):
    t0 = pl.program_id(1) * tq
    rows = NSA_GROUP * tq
    lane = lax.broadcasted_iota(jnp.int32, (tq, LANES), 1)
    tpos = t0 + lax.broadcasted_iota(jnp.int32, (tq, 1), 0)
    gates = jax.nn.sigmoid(gate_ref[...])
    n_cmp = kcmp_ref.shape[1]

    def flash(k_ref, v_ref, qs, c_lo, c_hi, kc, mask_fn):
        m_sc[...] = jnp.full((rows, 1), NEG, F32)
        l_sc[...] = jnp.zeros((rows, 1), F32)
        acc_sc[...] = jnp.zeros((rows, LANES), F32)

        def body(c, carry):
            k0 = pl.multiple_of(c * kc, kc)
            kb = k_ref[0, pl.ds(k0, kc), :]
            vb = v_ref[0, pl.ds(k0, kc), :]
            s = _dot_nt(qs, kb).reshape(NSA_GROUP, tq, kc)
            kpos = k0 + lax.broadcasted_iota(jnp.int32, (tq, kc), 1)
            bias = jnp.where(mask_fn(k0, kpos), 0.0, NEG)
            s = (s + bias[None]).reshape(rows, kc)
            m_old = m_sc[...]
            m_new = jnp.maximum(m_old, jnp.max(s, axis=1, keepdims=True))
            alpha = jnp.exp(m_old - m_new)
            p = jnp.exp(s - m_new)
            l_sc[...] = alpha * l_sc[...] + jnp.sum(p, axis=1, keepdims=True)
            acc_sc[...] = alpha * acc_sc[...] + _dot(p.astype(BF16), vb)
            m_sc[...] = m_new
            return carry

        lax.fori_loop(c_lo, c_hi, body, 0)
        return acc_sc[...] / l_sc[...]

    for k in range(NSA_KV_HEADS):
        keep = (lane >= HEAD_DIM) if k == 1 else (lane < HEAD_DIM)
        parts = []
        for g in range(NSA_GROUP):
            hh = k * NSA_GROUP + g
            c = q_ref[:, (hh // 2) * LANES:(hh // 2 + 1) * LANES].astype(F32)
            if hh % 2 != k:
                c = pltpu.roll(c, HEAD_DIM, 1)
            parts.append(jnp.where(keep, c, 0.0))
        qs = jnp.concatenate(parts, axis=0).astype(BF16)

        s = _dot_nt(qs, kcmp_ref[0]).reshape(NSA_GROUP, tq, n_cmp)
        cmp_end = lax.broadcasted_iota(jnp.int32, (tq, n_cmp), 1) * CMP_STRIDE + (CMP_BLOCK - 1)
        valid = cmp_end <= tpos
        s = s + jnp.where(valid, 0.0, NEG)[None]
        e = jnp.exp(s - jnp.max(s, axis=-1, keepdims=True)) * valid.astype(F32)[None]
        l = jnp.sum(e, axis=-1, keepdims=True)
        p3 = e / jnp.where(l > 0.0, l, 1.0)
        o_cmp = _dot(p3.reshape(rows, n_cmp).astype(BF16), vcmp_ref[0])

        psum = p3[0] + p3[1] + p3[2] + p3[3]
        ov = ov_ref[...]
        p_hi = psum.astype(BF16)
        r1 = psum - p_hi.astype(F32)
        p_mid = r1.astype(BF16)
        p_lo = (r1 - p_mid.astype(F32)).astype(BF16)
        imp = _dot(p_hi, ov) + _dot(p_mid, ov) + _dot(p_lo, ov)
        cur = tpos // SEL_BLOCK
        forced = (lane == 0) | (lane == cur) | (lane == cur - 1)
        imp = jnp.where(lane * SEL_BLOCK <= tpos, jnp.where(forced, FORCE, imp), NEG)
        n_sel = ks_ref.shape[1] // SEL_BLOCK
        rank = jnp.zeros((tq, LANES), jnp.int32)
        for jp in range(n_sel):
            col = imp[:, jp:jp + 1]
            beats = (col > imp) | ((col == imp) & (lane > jp))
            rank = rank + beats.astype(jnp.int32)
        sel = (rank < SEL_TOPN).astype(BF16)

        def sel_mask(k0, kpos):
            jrow = lax.broadcasted_iota(jnp.int32, (LANES, kc_sel), 0)
            kcol = lax.broadcasted_iota(jnp.int32, (LANES, kc_sel), 1)
            expand = (jrow == k0 // SEL_BLOCK + kcol // SEL_BLOCK).astype(BF16)
            return (_dot(sel, expand) > 0.5) & (kpos <= tpos)

        o_sel = flash(ks_ref, vs_ref, qs, 0, (t0 + tq + kc_sel - 1) // kc_sel, kc_sel, sel_mask)

        def win_mask(k0, kpos):
            return (kpos <= tpos) & (kpos > tpos - WINDOW)

        o_win = flash(kw_ref, vw_ref, qs, jnp.maximum(t0 - WINDOW, 0) // kc_win, (t0 + tq) // kc_win, kc_win,
                      win_mask)

        outs = []
        for g in range(NSA_GROUP):
            r = slice(g * tq, (g + 1) * tq)
            gi = (k * NSA_GROUP + g) * 3
            og = (gates[:, gi:gi + 1] * o_cmp[r] + gates[:, gi + 1:gi + 2] * o_sel[r]
                  + gates[:, gi + 2:gi + 3] * o_win[r])
            if g % 2 != k:
                og = pltpu.roll(og, HEAD_DIM, 1)
            outs.append(og)
        for j in range(NSA_GROUP // 2):
            chunk = jnp.where(lane < HEAD_DIM, outs[2 * j], outs[2 * j + 1])
            cj = k * (NSA_GROUP // 2) + j
            o_ref[:, cj * LANES:(cj + 1) * LANES] = chunk.astype(BF16)


def _nsa(q, kcmp, vcmp, kv, gates_misc, overlap, B, S, tq):
    kc_sel, kc_win = 256, 128
    nq = S // tq
    n_cmp = kcmp.shape[1]
    rows = NSA_GROUP * tq
    kvspec = lambda j: pl.BlockSpec((1, S, LANES), lambda b, i: (b, 0, j))
    return pl.pallas_call(
        functools.partial(_nsa_kernel, tq=tq, kc_sel=kc_sel, kc_win=kc_win),
        grid=(B, nq),
        in_specs=[pl.BlockSpec((tq, D_NSA), lambda b, i: (b * nq + i, 0)),
                  pl.BlockSpec((1, n_cmp, LANES), lambda b, i: (b, 0, 0)),
                  pl.BlockSpec((1, n_cmp, LANES), lambda b, i: (b, 0, 0)),
                  kvspec(2), kvspec(3), kvspec(4), kvspec(5),
                  pl.BlockSpec((tq, LANES), lambda b, i: (b * nq + i, 0)),
                  _full(overlap.shape)],
        out_specs=pl.BlockSpec((tq, D_NSA), lambda b, i: (b * nq + i, 0)),
        out_shape=jax.ShapeDtypeStruct((B * S, D_NSA), BF16),
        scratch_shapes=[pltpu.VMEM((rows, 1), F32), pltpu.VMEM((rows, 1), F32), pltpu.VMEM((rows, LANES), F32)],
        compiler_params=_params("parallel", "parallel"),
    )(q, kcmp, vcmp, kv, kv, kv, kv, gates_misc, overlap)


HALO = 32


def _mixout_kernel(mc_ref, mp_ref, on_ref, x_ref, cw_ref, cb_ref, lg_ref, lb_ref, pw_ref, pb_ref, plw_ref,
                   pls_ref, wo_ref, o_ref, hbuf, pbuf, *, ts, dc):
    i = pl.program_id(1)
    first = i == 0
    off_a, off_b, off_p = LANES, LANES + dc, LANES + 2 * dc

    def glu(ref, r):
        return ref[r, off_a:off_a + dc] * jax.nn.sigmoid(ref[r, off_b:off_b + dc])

    tail = slice(ts - HALO, ts)
    hbuf[0:HALO, :] = jnp.where(first, 0.0, glu(mp_ref, tail))
    hbuf[HALO:, :] = glu(mc_ref, slice(None))
    pbuf[0:HALO, :] = jnp.where(first, 0.0, mp_ref[tail, off_p:off_p + dc])
    pcur = mc_ref[:, off_p:off_p + dc]
    pbuf[HALO:, :] = pcur

    acc = jnp.zeros((ts, dc), F32) + cb_ref[...]
    for w in range(CONV_WIDTH):
        acc = acc + hbuf[pl.ds(HALO - (CONV_WIDTH - 1) + w, ts), :] * cw_ref[w:w + 1, :]
    mu = jnp.mean(acc, axis=-1, keepdims=True)
    xc = acc - mu
    y = xc * lax.rsqrt(jnp.mean(xc * xc, axis=-1, keepdims=True) + EPS) * lg_ref[...] + lb_ref[...]
    y = y * jax.nn.sigmoid(y)
    o_conv = _dot(y.astype(BF16), pw_ref[...]) + pb_ref[...]

    tglob = i * ts + lax.broadcasted_iota(jnp.int32, (ts, dc), 0)
    lane = lax.broadcasted_iota(jnp.int32, (ts, dc), 1)
    pg = dc // len(POOL_WINDOWS)
    run = pcur
    d = 1
    mean = jnp.zeros((ts, dc), F32)
    for gi, w in enumerate(POOL_WINDOWS):
        while d < w:
            run = run + pbuf[pl.ds(HALO - d, ts), :]
            d += 1
        cnt = jnp.minimum(tglob + 1, w).astype(F32)
        mean = jnp.where(lane // pg == gi, run / cnt, mean)
    o_pool = _dot((mean - pcur).astype(BF16), plw_ref[...]) * pls_ref[...]

    dn = on_ref.shape[1]
    o_ref[...] = (x_ref[...] + _dot(on_ref[...], wo_ref[0:dn, :])
                  + _dot(o_conv.astype(BF16), wo_ref[dn:dn + dc, :])
                  + _dot(o_pool.astype(BF16), wo_ref[dn + dc:, :]))


def _mixout(misc, o_nsa, x2, cw, cb, lg, lb, pw, pb, plw, pls, wo, B, S, ts):
    T, D = x2.shape
    dc = cw.shape[1]
    ns = S // ts
    nm = misc.shape[1]
    cur = lambda w: pl.BlockSpec((ts, w), lambda b, i: (b * ns + i, 0))
    prev = pl.BlockSpec((ts, nm), lambda b, i: (b * ns + jnp.maximum(i - 1, 0), 0))
    return pl.pallas_call(
        functools.partial(_mixout_kernel, ts=ts, dc=dc),
        grid=(B, ns),
        in_specs=[cur(nm), prev, cur(o_nsa.shape[1]), cur(D), _full(cw.shape), _full(cb.shape), _full(lg.shape),
                  _full(lb.shape), _full(pw.shape), _full(pb.shape), _full(plw.shape), _full(pls.shape),
                  _full(wo.shape)],
        out_specs=cur(D),
        out_shape=jax.ShapeDtypeStruct((T, D), F32),
        scratch_shapes=[pltpu.VMEM((ts + HALO, dc), F32), pltpu.VMEM((ts + HALO, dc), F32)],
        compiler_params=_params("parallel", "parallel"),
    )(misc, misc, o_nsa, x2, cw, cb, lg, lb, pw, pb, plw, pls, wo)


def _norm_matmul_kernel(x_ref, g_ref, w_ref, o_ref):
    o_ref[...] = _dot(_rms(x_ref[...], g_ref[...]).astype(BF16), w_ref[...]).astype(o_ref.dtype)


def _norm_matmul(x2, g, w, tm, out_dtype):
    T, D = x2.shape
    n = w.shape[1]
    return pl.pallas_call(
        _norm_matmul_kernel,
        grid=(T // tm,),
        in_specs=[pl.BlockSpec((tm, D), lambda i: (i, 0)), _full((1, D)), _full((D, n))],
        out_specs=pl.BlockSpec((tm, n), lambda i: (i, 0)),
        out_shape=jax.ShapeDtypeStruct((T, n), out_dtype),
        compiler_params=_params("parallel"),
    )(x2, g, w)


def _xattn_kernel(x_ref, g_ref, wq_ref, k_ref, v_ref, wo_ref, o_ref):
    x = x_ref[...]
    D = x.shape[1]
    dh = D // XA_HEADS
    q = _dot(_rms(x, g_ref[...]).astype(BF16), wq_ref[...]) * (dh ** -0.5)
    outs = []
    for h in range(XA_HEADS):
        c = slice(h * dh, (h + 1) * dh)
        s = _dot_nt(q[:, c].astype(BF16), k_ref[0, :, c])
        e = jnp.exp(s - jnp.max(s, axis=-1, keepdims=True))
        p = e / jnp.sum(e, axis=-1, keepdims=True)
        outs.append(_dot(p.astype(BF16), v_ref[0, :, c]))
    o = jnp.concatenate(outs, axis=1)
    o_ref[...] = x + _dot(o.astype(BF16), wo_ref[...])


def _xattn(x2, g, wq, memkv, wo, B, S, tm):
    T, D = x2.shape
    ns = S // tm
    M = memkv.shape[1]
    return pl.pallas_call(
        _xattn_kernel,
        grid=(B, ns),
        in_specs=[pl.BlockSpec((tm, D), lambda b, i: (b * ns + i, 0)), _full((1, D)), _full((D, D)),
                  pl.BlockSpec((1, M, D), lambda b, i: (b, 0, 0)),
                  pl.BlockSpec((1, M, D), lambda b, i: (b, 0, 1)),
                  _full((D, D))],
        out_specs=pl.BlockSpec((tm, D), lambda b, i: (b * ns + i, 0)),
        out_shape=jax.ShapeDtypeStruct((T, D), F32),
        compiler_params=_params("parallel", "parallel"),
    )(x2, g, wq, memkv, memkv, wo)


PAIR_LIST = tuple((a, b) for a in range(PEER_TOPK) for b in range(PEER_TOPK) if (a + 1) * (b + 1) <= PEER_TOPK)
N_PAIR_ROWS = -(-len(PAIR_LIST) // 8) * 8


def _top_rows(v, n):
    R = v.shape[0]
    ridx = lax.broadcasted_iota(jnp.int32, v.shape, 0)
    outs = []
    for r in range(n):
        m = jnp.max(v, axis=0, keepdims=True)
        outs.append(m)
        if r + 1 < n:
            first = jnp.min(jnp.where(v == m, ridx, R), axis=0, keepdims=True)
            v = jnp.where(ridx == first, LOWEST, v)
    return outs


def _peer_route_kernel(x_ref, g_ref, wqt_ref, sk_ref, hn_ref, s_ref, gt_ref, tau_ref, cand_sc, cw_sc):
    hn = _rms(x_ref[...], g_ref[...]).astype(BF16)
    hn_ref[...] = hn
    qt = _dot_nt(wqt_ref[...], hn)
    nk = PEER_KEYS
    half_rows = PEER_HEADS * nk
    cand_sc[...] = jnp.full(cand_sc.shape, LOWEST, F32)
    cw_sc[...] = jnp.zeros(cw_sc.shape, F32)
    for h in range(PEER_HEADS):
        r0 = h * 2 * nk
        s1 = _dot(sk_ref[0], qt[r0:r0 + nk].astype(BF16))
        s2 = _dot(sk_ref[1], qt[r0 + nk:r0 + 2 * nk].astype(BF16))
        top1 = _top_rows(s1, PEER_TOPK)
        top2 = _top_rows(s2, PEER_TOPK)
        e1 = [jnp.exp(t - top1[0]) for t in top1]
        e2 = [jnp.exp(t - top2[0]) for t in top2]
        for r, (a, b) in enumerate(PAIR_LIST):
            cand_sc[r:r + 1, :] = top1[a] + top2[b]
            cw_sc[r:r + 1, :] = e1[a] * e2[b]
        cand = cand_sc[...]
        tau = _top_rows(cand, PEER_TOPK)[-1]
        z = jnp.sum(jnp.where(cand >= tau, cw_sc[...], 0.0), axis=0, keepdims=True)
        s_ref[h * nk:(h + 1) * nk, :] = s1
        s_ref[half_rows + h * nk:half_rows + (h + 1) * nk, :] = s2
        gt_ref[h * nk:(h + 1) * nk, :] = jnp.exp(s1 - top1[0]) * (1.0 / z)
        gt_ref[half_rows + h * nk:half_rows + (h + 1) * nk, :] = jnp.exp(s2 - top2[0])
        tau_ref[h:h + 1, :] = tau


def _peer_route(x2, g, wqt, sk, tm):
    T, D = x2.shape
    nq = wqt.shape[0]
    col = lambda r: pl.BlockSpec((r, tm), lambda i: (0, i))
    return pl.pallas_call(
        _peer_route_kernel,
        grid=(T // tm,),
        in_specs=[pl.BlockSpec((tm, D), lambda i: (i, 0)), _full((1, D)), _full(wqt.shape), _full(sk.shape)],
        out_specs=[pl.BlockSpec((tm, D), lambda i: (i, 0)), col(nq), col(nq), col(PEER_HEADS)],
        out_shape=[jax.ShapeDtypeStruct((T, D), BF16), jax.ShapeDtypeStruct((nq, T), F32),
                   jax.ShapeDtypeStruct((nq, T), F32), jax.ShapeDtypeStruct((PEER_HEADS, T), F32)],
        scratch_shapes=[pltpu.VMEM((N_PAIR_ROWS, tm), F32), pltpu.VMEM((N_PAIR_ROWS, tm), F32)],
        compiler_params=_params("parallel"),
    )(x2, g, wqt, sk)


def _peer_expert_kernel(hn_ref, s_ref, gt_ref, tau_ref, u_ref, vt_ref, x_ref, o_ref, acc_sc, wa_sc, *, te):
    c = pl.program_id(1)
    nk = PEER_KEYS
    half_rows = PEER_HEADS * nk

    @pl.when(c == 0)
    def _():
        acc_sc[...] = jnp.zeros(acc_sc.shape, F32)

    act = _gelu(_dot_nt(u_ref[...], hn_ref[...]))
    for il in range(te // nk):
        i = c * (te // nk) + il
        w = jnp.zeros((nk, act.shape[1]), F32)
        for h in range(PEER_HEADS):
            s1row = s_ref[pl.ds(h * nk + i, 1), :]
            g1row = gt_ref[pl.ds(h * nk + i, 1), :]
            lo = half_rows + h * nk
            hit = s_ref[lo:lo + nk, :] + s1row >= tau_ref[h:h + 1, :]
            w = w + jnp.where(hit, gt_ref[lo:lo + nk, :], 0.0) * g1row
        wa_sc[il * nk:(il + 1) * nk, :] = (w * act[il * nk:(il + 1) * nk]).astype(BF16)
    acc_sc[...] += _dot(vt_ref[...], wa_sc[...])

    @pl.when(c == pl.num_programs(1) - 1)
    def _():
        o_ref[...] = x_ref[...] + acc_sc[...].T


def _peer_expert(hn, sT, gT, tau, u, vt, x2, tm, te):
    T, D = x2.shape
    ne = u.shape[0]
    nq = sT.shape[0]
    tok = lambda r: pl.BlockSpec((r, tm), lambda i, c: (0, i))
    row = pl.BlockSpec((tm, D), lambda i, c: (i, 0))
    return pl.pallas_call(
        functools.partial(_peer_expert_kernel, te=te),
        grid=(T // tm, ne // te),
        in_specs=[row, tok(nq), tok(nq), tok(PEER_HEADS),
                  pl.BlockSpec((te, D), lambda i, c: (c, 0)),
                  pl.BlockSpec((D, te), lambda i, c: (0, c)),
                  row],
        out_specs=row,
        out_shape=jax.ShapeDtypeStruct((T, D), F32),
        scratch_shapes=[pltpu.VMEM((D, tm), F32), pltpu.VMEM((te, tm), BF16)],
        compiler_params=_params("parallel", "arbitrary"),
    )(hn, sT, gT, tau, u, vt, x2)


def _final_norm_kernel(x_ref, g_ref, o_ref):
    o_ref[...] = _rms(x_ref[...], g_ref[...])


def _final_norm(x2, g, tm):
    T, D = x2.shape
    return pl.pallas_call(
        _final_norm_kernel,
        grid=(T // tm,),
        in_specs=[pl.BlockSpec((tm, D), lambda i: (i, 0)), _full((1, D))],
        out_specs=pl.BlockSpec((tm, D), lambda i: (i, 0)),
        out_shape=jax.ShapeDtypeStruct((T, D), F32),
        compiler_params=_params("parallel"),
    )(x2, g)


def _rope_tables(positions):
    B, S = positions.shape
    freqs = ROPE_THETA ** (-jnp.arange(ROT_HALF, dtype=F32) * 2.0 / ROT_DIM)
    ang = positions.astype(F32)[:, :, None] * freqs
    cos, sin = jnp.cos(ang), jnp.sin(ang)
    ones = jnp.ones((B, S, HEAD_DIM - ROT_DIM), F32)
    zeros8 = jnp.zeros((B, S, ROT_HALF), F32)
    zeros = jnp.zeros((B, S, HEAD_DIM - ROT_DIM), F32)
    cos_h = jnp.concatenate([cos, cos, ones], axis=-1)
    sa_h = jnp.concatenate([zeros8, sin, zeros], axis=-1)
    sb_h = jnp.concatenate([-sin, zeros8, zeros], axis=-1)
    rep = LANES // HEAD_DIM
    tile = lambda t: jnp.tile(t, (1, 1, rep)).reshape(B * S, LANES)
    return tile(cos_h), tile(sa_h), tile(sb_h)


def _overlap_matrix(n_chunk, n_sel):
    ci = np.arange(n_chunk)[:, None] * CMP_STRIDE
    sj = np.arange(LANES)[None, :] * SEL_BLOCK
    ov = (ci < sj + SEL_BLOCK) & (ci + CMP_BLOCK > sj) & (np.arange(LANES)[None, :] < n_sel)
    ov = ov & (np.arange(n_chunk)[:, None] < n_chunk - 1)
    return jnp.asarray(ov, BF16)


def kernel(x, mem, positions, norm_mix_g, w_in, cmp_pe, cmp_w1, cmp_w2, conv_w, conv_b, conv_ln_g, conv_ln_b, conv_pw_w, conv_pw_b, pool_w, pool_scale, w_out, norm_xa_g, norm_mem_g, xa_wq, xa_wkv, xa_wo, norm_ffn_g, peer_wq, peer_subkeys, peer_u, peer_v, final_g):
    B, S, D = x.shape
    T = B * S
    depth = w_in.shape[0]
    M = mem.shape[1]
    dc = conv_w.shape[-1]
    n_chunk = S // CMP_STRIDE
    n_sel = S // SEL_BLOCK
    assert S % 512 == 0 and n_sel <= LANES and n_sel >= SEL_TOPN and D % LANES == 0

    cos_t, sa_t, sb_t = _rope_tables(positions)
    overlap = _overlap_matrix(n_chunk, n_sel)
    row = lambda v: v.reshape(1, -1)
    x2 = x.reshape(T, D)
    mem2 = mem.reshape(B * M, D)
    n_gate = 3 * NSA_HEADS

    for l in range(depth):
        w = w_in[l]
        w_pad = jnp.concatenate([w[:, :D_QKV + n_gate], jnp.zeros((D, LANES - n_gate), F32), w[:, D_QKV + n_gate:]],
                                axis=1).astype(BF16)
        q, kv, misc = _inproj(x2, row(norm_mix_g[l]), w_pad, cos_t, sa_t, sb_t, tm=512)
        xc = kv[:, :2 * LANES].reshape(B, S, 2, NSA_KV_HEADS, HEAD_DIM)
        xc = jnp.transpose(xc, (2, 0, 3, 1, 4)).reshape(2, B * NSA_KV_HEADS, n_chunk, CMP_STRIDE * HEAD_DIM)
        pe = cmp_pe[l].reshape(2, 2, CMP_STRIDE * HEAD_DIM)
        cmp = _compress(xc, pe, cmp_w1[l].astype(BF16), cmp_w2[l].astype(BF16))
        cmp = cmp.reshape(2, B, NSA_KV_HEADS, n_chunk, HEAD_DIM)
        cmp = jnp.transpose(cmp, (0, 1, 3, 2, 4)).reshape(2, B, n_chunk, LANES).astype(BF16)
        o_nsa = _nsa(q, cmp[0], cmp[1], kv.reshape(B, S, -1), misc, overlap, B, S, tq=128)
        pg = dc // len(POOL_WINDOWS)
        plw = jnp.zeros((dc, dc), F32)
        for gi in range(len(POOL_WINDOWS)):
            plw = plw.at[gi * pg:(gi + 1) * pg, gi * pg:(gi + 1) * pg].set(pool_w[l, gi])
        x2 = _mixout(misc, o_nsa, x2, conv_w[l], row(conv_b[l]), row(conv_ln_g[l]), row(conv_ln_b[l]),
                     conv_pw_w[l].astype(BF16), row(conv_pw_b[l]), plw.astype(BF16), row(pool_scale[l]),
                     w_out[l].astype(BF16), B, S, ts=512)
        memkv = _norm_matmul(mem2, row(norm_mem_g[l]), xa_wkv[l].astype(BF16), tm=256, out_dtype=BF16)
        x2 = _xattn(x2, row(norm_xa_g[l]), xa_wq[l].astype(BF16), memkv.reshape(B, M, 2 * D),
                    xa_wo[l].astype(BF16), B, S, tm=512)
        hn, sT, gT, tau = _peer_route(x2, row(norm_ffn_g[l]), peer_wq[l].T.astype(BF16),
                                      peer_subkeys[l].astype(BF16), tm=512)
        x2 = _peer_expert(hn, sT, gT, tau, peer_u[l].astype(BF16), peer_v[l].T.astype(BF16), x2, tm=512, te=512)
    return _final_norm(x2, row(final_g), tm=512).reshape(B, S, D)
```

```python
import functools
import math

import jax
import jax.numpy as jnp
import numpy as np
from jax import lax
from jax.experimental import pallas as pl
from jax.experimental.pallas import tpu as pltpu

F32 = jnp.float32
BF16 = jnp.bfloat16

NSA_HEADS = 8
NSA_KV_HEADS = 2
NSA_GROUP = NSA_HEADS // NSA_KV_HEADS
HEAD_DIM = 64
D_NSA = NSA_HEADS * HEAD_DIM
D_KV = NSA_KV_HEADS * HEAD_DIM
ROT_DIM = HEAD_DIM // 4
ROT_HALF = ROT_DIM // 2
ROPE_THETA = 500000.0
CMP_BLOCK = 32
CMP_STRIDE = 16
SEL_BLOCK = 64
SEL_TOPN = 16
WINDOW = 512
CONV_WIDTH = 31
POOL_WINDOWS = (2, 4, 8, 16)
XA_HEADS = 4
PEER_HEADS = 8
PEER_KEYS = 128
PEER_TOPK = 16
EPS = 1e-6
NEG = -1e30
FORCE = 1e4
LOWEST = -3.0e38

LANES = 128
VMEM_LIMIT = 48 * 1024 * 1024

NT_DIMS = (((1,), (1,)), ((), ()))


def _params(*sem):
    return pltpu.CompilerParams(dimension_semantics=sem, vmem_limit_bytes=VMEM_LIMIT)


def _full(shape):
    nd = len(shape)
    return pl.BlockSpec(shape, lambda *_: (0,) * nd)


def _rms(x, g):
    return x * lax.rsqrt(jnp.mean(x * x, axis=-1, keepdims=True) + EPS) * g


def _gelu(x):
    c = math.sqrt(2.0 / math.pi)
    return 0.5 * x * (1.0 + jnp.tanh(c * (x + 0.044715 * (x * x * x))))


def _dot(a, b):
    return jnp.dot(a, b, preferred_element_type=F32)


def _dot_nt(a, b):
    return lax.dot_general(a, b, NT_DIMS, preferred_element_type=F32)


N_ROPE_Q = D_NSA // LANES
N_KV_CHUNKS = 6
D_QKV = D_NSA + N_KV_CHUNKS * D_KV


def _inproj_kernel(x_ref, g_ref, w_ref, cos_ref, sa_ref, sb_ref, q_ref, kv_ref, misc_ref):
    y = _rms(x_ref[...], g_ref[...])
    z = _dot(y.astype(BF16), w_ref[...])
    cos, sa, sb = cos_ref[...], sa_ref[...], sb_ref[...]

    def rope(c):
        return c * cos + pltpu.roll(c, ROT_HALF, 1) * sa + pltpu.roll(c, LANES - ROT_HALF, 1) * sb

    scale = HEAD_DIM ** -0.5
    for j in range(N_ROPE_Q):
        q_ref[:, j * LANES:(j + 1) * LANES] = (rope(z[:, j * LANES:(j + 1) * LANES]) * scale).astype(BF16)
    for j in range(N_KV_CHUNKS):
        c = z[:, D_NSA + j * LANES:D_NSA + (j + 1) * LANES]
        if j % 2 == 0:
            c = rope(c)
        kv_ref[:, j * LANES:(j + 1) * LANES] = c.astype(BF16)
    misc_ref[...] = z[:, D_QKV:]


def _inproj(x2, g, w_pad, cos_t, sa_t, sb_t, tm):
    T, D = x2.shape
    n = w_pad.shape[1]
    n_misc = n - D_QKV
    row = lambda w: pl.BlockSpec((tm, w), lambda i: (i, 0))
    return pl.pallas_call(
        _inproj_kernel,
        grid=(T // tm,),
        in_specs=[row(D), _full((1, D)), _full((D, n)), row(LANES), row(LANES), row(LANES)],
        out_specs=[row(D_NSA), row(N_KV_CHUNKS * LANES), row(n_misc)],
        out_shape=[jax.ShapeDtypeStruct((T, D_NSA), BF16),
                   jax.ShapeDtypeStruct((T, N_KV_CHUNKS * LANES), BF16),
                   jax.ShapeDtypeStruct((T, n_misc), F32)],
        compiler_params=_params("parallel"),
    )(x2, g, w_pad, cos_t, sa_t, sb_t)


def _compress_kernel(x_ref, pe_ref, w1_ref, w2_ref, o_ref):
    x = x_ref[0, 0].astype(F32)
    pe = pe_ref[0]
    half = x.shape[1]
    a = _dot((x + pe[0:1]).astype(BF16), w1_ref[0, :half])
    b = _dot((x + pe[1:2]).astype(BF16), w1_ref[0, half:])
    n = x.shape[0]
    hid = _gelu(a + pltpu.roll(b, n - 1, 0))
    o_ref[0, 0] = _dot(hid.astype(BF16), w2_ref[0])


def _compress(xc, pe, w1, w2):
    two, nb, nch, width = xc.shape
    hid = w1.shape[-1]
    return pl.pallas_call(
        _compress_kernel,
        grid=(two, nb),
        in_specs=[pl.BlockSpec((1, 1, nch, width), lambda w, i: (w, i, 0, 0)),
                  pl.BlockSpec((1, 2, width), lambda w, i: (w, 0, 0)),
                  pl.BlockSpec((1, 2 * width, hid), lambda w, i: (w, 0, 0)),
                  pl.BlockSpec((1, hid, HEAD_DIM), lambda w, i: (w, 0, 0))],
        out_specs=pl.BlockSpec((1, 1, nch, HEAD_DIM), lambda w, i: (w, i, 0, 0)),
        out_shape=jax.ShapeDtypeStruct((two, nb, nch, HEAD_DIM), F32),
        compiler_params=_params("parallel", "parallel"),
    )(xc, pe, w1, w2)


def _nsa_kernel(q_ref, kcmp_ref, vcmp_ref, ks_ref, vs_ref, kw_ref, vw_ref, gate_ref, ov_ref, o_ref,
                m_sc, l_sc, acc_sc, *, tq, kc_sel, kc_win):
    t0 = pl.program_id(1) * tq
    rows = NSA_GROUP * tq
    lane = lax.broadcasted_iota(jnp.int32, (tq, LANES), 1)
    tpos = t0 + lax.broadcasted_iota(jnp.int32, (tq, 1), 0)
    gates = jax.nn.sigmoid(gate_ref[...])
    n_cmp = kcmp_ref.shape[1]

    def flash(k_ref, v_ref, qs, c_lo, c_hi, kc, mask_fn):
        m_sc[...] = jnp.full((rows, 1), NEG, F32)
        l_sc[...] = jnp.zeros((rows, 1), F32)
        acc_sc[...] = jnp.zeros((rows, LANES), F32)

        def body(c, carry):
            k0 = pl.multiple_of(c * kc, kc)
            kb = k_ref[0, pl.ds(k0, kc), :]
            vb = v_ref[0, pl.ds(k0, kc), :]
            s = _dot_nt(qs, kb).reshape(NSA_GROUP, tq, kc)
            kpos = k0 + lax.broadcasted_iota(jnp.int32, (tq, kc), 1)
            bias = jnp.where(mask_fn(k0, kpos), 0.0, NEG)
            s = (s + bias[None]).reshape(rows, kc)
            m_old = m_sc[...]
            m_new = jnp.maximum(m_old, jnp.max(s, axis=1, keepdims=True))
            alpha = jnp.exp(m_old - m_new)
            p = jnp.exp(s - m_new)
            l_sc[...] = alpha * l_sc[...] + jnp.sum(p, axis=1, keepdims=True)
            acc_sc[...] = alpha * acc_sc[...] + _dot(p.astype(BF16), vb)
            m_sc[...] = m_new
            return carry

        lax.fori_loop(c_lo, c_hi, body, 0)
        return acc_sc[...] / l_sc[...]

    for k in range(NSA_KV_HEADS):
        keep = (lane >= HEAD_DIM) if k == 1 else (lane < HEAD_DIM)
        parts = []
        for g in range(NSA_GROUP):
            hh = k * NSA_GROUP + g
            c = q_ref[:, (hh // 2) * LANES:(hh // 2 + 1) * LANES].astype(F32)
            if hh % 2 != k:
                c = pltpu.roll(c, HEAD_DIM, 1)
            parts.append(jnp.where(keep, c, 0.0))
        qs = jnp.concatenate(parts, axis=0).astype(BF16)

        s = _dot_nt(qs, kcmp_ref[0]).reshape(NSA_GROUP, tq, n_cmp)
        cmp_end = lax.broadcasted_iota(jnp.int32, (tq, n_cmp), 1) * CMP_STRIDE + (CMP_BLOCK - 1)
        valid = cmp_end <= tpos
        s = s + jnp.where(valid, 0.0, NEG)[None]
        e = jnp.exp(s - jnp.max(s, axis=-1, keepdims=True)) * valid.astype(F32)[None]
        l = jnp.sum(e, axis=-1, keepdims=True)
        p3 = e / jnp.where(l > 0.0, l, 1.0)
        o_cmp = _dot(p3.reshape(rows, n_cmp).astype(BF16), vcmp_ref[0])

        psum = p3[0] + p3[1] + p3[2] + p3[3]
        ov = ov_ref[...]
        p_hi = psum.astype(BF16)
        r1 = psum - p_hi.astype(F32)
        p_mid = r1.astype(BF16)
        p_lo = (r1 - p_mid.astype(F32)).astype(BF16)
        imp = _dot(p_hi, ov) + _dot(p_mid, ov) + _dot(p_lo, ov)
        imp = imp.T
        blk = lax.broadcasted_iota(jnp.int32, (LANES, tq), 0)
        tpos_t = t0 + lax.broadcasted_iota(jnp.int32, (1, tq), 1)
        cur = tpos_t // SEL_BLOCK
        forced = (blk == 0) | (blk == cur) | (blk == cur - 1)
        imp = jnp.where(blk * SEL_BLOCK <= tpos_t, jnp.where(forced, FORCE, imp), NEG)
        n_sel = ks_ref.shape[1] // SEL_BLOCK
        rank = jnp.zeros((LANES, tq), jnp.int32)
        for jp in range(n_sel):
            one = imp[jp:jp + 1, :]
            beats = (one > imp) | ((one == imp) & (blk > jp))
            rank = rank + beats.astype(jnp.int32)
        sel = (rank < SEL_TOPN).astype(F32).T.astype(BF16)

        def sel_mask(k0, kpos):
            jrow = lax.broadcasted_iota(jnp.int32, (LANES, kc_sel), 0)
            kcol = lax.broadcasted_iota(jnp.int32, (LANES, kc_sel), 1)
            expand = (jrow == k0 // SEL_BLOCK + kcol // SEL_BLOCK).astype(BF16)
            return (_dot(sel, expand) > 0.5) & (kpos <= tpos)

        o_sel = flash(ks_ref, vs_ref, qs, 0, (t0 + tq + kc_sel - 1) // kc_sel, kc_sel, sel_mask)

        def win_mask(k0, kpos):
            return (kpos <= tpos) & (kpos > tpos - WINDOW)

        w0 = pl.multiple_of(jnp.maximum(t0 + tq - kc_win, 0), tq)
        sw = _dot_nt(qs, kw_ref[0, pl.ds(w0, kc_win), :]).reshape(NSA_GROUP, tq, kc_win)
        wpos = w0 + lax.broadcasted_iota(jnp.int32, (tq, kc_win), 1)
        sw = (sw + jnp.where(win_mask(w0, wpos), 0.0, NEG)[None]).reshape(rows, kc_win)
        pw = jnp.exp(sw - jnp.max(sw, axis=1, keepdims=True))
        o_win = _dot(pw.astype(BF16), vw_ref[0, pl.ds(w0, kc_win), :]) / jnp.sum(pw, axis=1, keepdims=True)

        outs = []
        for g in range(NSA_GROUP):
            r = slice(g * tq, (g + 1) * tq)
            gi = (k * NSA_GROUP + g) * 3
            og = (gates[:, gi:gi + 1] * o_cmp[r] + gates[:, gi + 1:gi + 2] * o_sel[r]
                  + gates[:, gi + 2:gi + 3] * o_win[r])
            if g % 2 != k:
                og = pltpu.roll(og, HEAD_DIM, 1)
            outs.append(og)
        for j in range(NSA_GROUP // 2):
            chunk = jnp.where(lane < HEAD_DIM, outs[2 * j], outs[2 * j + 1])
            cj = k * (NSA_GROUP // 2) + j
            o_ref[:, cj * LANES:(cj + 1) * LANES] = chunk.astype(BF16)


def _nsa(q, kcmp, vcmp, kv, gates_misc, overlap, B, S, tq):
    kc_sel, kc_win = 512, WINDOW + tq
    nq = S // tq
    n_cmp = kcmp.shape[1]
    rows = NSA_GROUP * tq
    kvspec = lambda j: pl.BlockSpec((1, S, LANES), lambda b, i: (b, 0, j))
    return pl.pallas_call(
        functools.partial(_nsa_kernel, tq=tq, kc_sel=kc_sel, kc_win=kc_win),
        grid=(B, nq),
        in_specs=[pl.BlockSpec((tq, D_NSA), lambda b, i: (b * nq + i, 0)),
                  pl.BlockSpec((1, n_cmp, LANES), lambda b, i: (b, 0, 0)),
                  pl.BlockSpec((1, n_cmp, LANES), lambda b, i: (b, 0, 0)),
                  kvspec(2), kvspec(3), kvspec(4), kvspec(5),
                  pl.BlockSpec((tq, LANES), lambda b, i: (b * nq + i, 0)),
                  _full(overlap.shape)],
        out_specs=pl.BlockSpec((tq, D_NSA), lambda b, i: (b * nq + i, 0)),
        out_shape=jax.ShapeDtypeStruct((B * S, D_NSA), BF16),
        scratch_shapes=[pltpu.VMEM((rows, 1), F32), pltpu.VMEM((rows, 1), F32), pltpu.VMEM((rows, LANES), F32)],
        compiler_params=_params("parallel", "parallel"),
    )(q, kcmp, vcmp, kv, kv, kv, kv, gates_misc, overlap)


HALO = 32


def _mixout_kernel(mc_ref, mp_ref, on_ref, x_ref, cw_ref, cb_ref, lg_ref, lb_ref, pw_ref, pb_ref, plw_ref,
                   pls_ref, wo_ref, o_ref, hbuf, pbuf, *, ts, dc):
    i = pl.program_id(1)
    first = i == 0
    off_a, off_b, off_p = LANES, LANES + dc, LANES + 2 * dc

    def glu(ref, r):
        return ref[r, off_a:off_a + dc] * jax.nn.sigmoid(ref[r, off_b:off_b + dc])

    tail = slice(ts - HALO, ts)
    hbuf[0:HALO, :] = jnp.where(first, 0.0, glu(mp_ref, tail))
    hbuf[HALO:, :] = glu(mc_ref, slice(None))
    pbuf[0:HALO, :] = jnp.where(first, 0.0, mp_ref[tail, off_p:off_p + dc])
    pcur = mc_ref[:, off_p:off_p + dc]
    pbuf[HALO:, :] = pcur

    acc = jnp.zeros((ts, dc), F32) + cb_ref[...]
    for w in range(CONV_WIDTH):
        acc = acc + hbuf[pl.ds(HALO - (CONV_WIDTH - 1) + w, ts), :] * cw_ref[w:w + 1, :]
    mu = jnp.mean(acc, axis=-1, keepdims=True)
    xc = acc - mu
    y = xc * lax.rsqrt(jnp.mean(xc * xc, axis=-1, keepdims=True) + EPS) * lg_ref[...] + lb_ref[...]
    y = y * jax.nn.sigmoid(y)
    o_conv = _dot(y.astype(BF16), pw_ref[...]) + pb_ref[...]

    tglob = i * ts + lax.broadcasted_iota(jnp.int32, (ts, dc), 0)
    lane = lax.broadcasted_iota(jnp.int32, (ts, dc), 1)
    pg = dc // len(POOL_WINDOWS)
    run = pcur
    d = 1
    mean = jnp.zeros((ts, dc), F32)
    for gi, w in enumerate(POOL_WINDOWS):
        while d < w:
            run = run + pbuf[pl.ds(HALO - d, ts), :]
            d += 1
        cnt = jnp.minimum(tglob + 1, w).astype(F32)
        mean = jnp.where(lane // pg == gi, run / cnt, mean)
    o_pool = _dot((mean - pcur).astype(BF16), plw_ref[...]) * pls_ref[...]

    dn = on_ref.shape[1]
    o_ref[...] = (x_ref[...] + _dot(on_ref[...], wo_ref[0:dn, :])
                  + _dot(o_conv.astype(BF16), wo_ref[dn:dn + dc, :])
                  + _dot(o_pool.astype(BF16), wo_ref[dn + dc:, :]))


def _mixout(misc, o_nsa, x2, cw, cb, lg, lb, pw, pb, plw, pls, wo, B, S, ts):
    T, D = x2.shape
    dc = cw.shape[1]
    ns = S // ts
    nm = misc.shape[1]
    cur = lambda w: pl.BlockSpec((ts, w), lambda b, i: (b * ns + i, 0))
    prev = pl.BlockSpec((ts, nm), lambda b, i: (b * ns + jnp.maximum(i - 1, 0), 0))
    return pl.pallas_call(
        functools.partial(_mixout_kernel, ts=ts, dc=dc),
        grid=(B, ns),
        in_specs=[cur(nm), prev, cur(o_nsa.shape[1]), cur(D), _full(cw.shape), _full(cb.shape), _full(lg.shape),
                  _full(lb.shape), _full(pw.shape), _full(pb.shape), _full(plw.shape), _full(pls.shape),
                  _full(wo.shape)],
        out_specs=cur(D),
        out_shape=jax.ShapeDtypeStruct((T, D), F32),
        scratch_shapes=[pltpu.VMEM((ts + HALO, dc), F32), pltpu.VMEM((ts + HALO, dc), F32)],
        compiler_params=_params("parallel", "parallel"),
    )(misc, misc, o_nsa, x2, cw, cb, lg, lb, pw, pb, plw, pls, wo)


def _norm_matmul_kernel(x_ref, g_ref, w_ref, o_ref):
    o_ref[...] = _dot(_rms(x_ref[...], g_ref[...]).astype(BF16), w_ref[...]).astype(o_ref.dtype)


def _norm_matmul(x2, g, w, tm, out_dtype):
    T, D = x2.shape
    n = w.shape[1]
    return pl.pallas_call(
        _norm_matmul_kernel,
        grid=(T // tm,),
        in_specs=[pl.BlockSpec((tm, D), lambda i: (i, 0)), _full((1, D)), _full((D, n))],
        out_specs=pl.BlockSpec((tm, n), lambda i: (i, 0)),
        out_shape=jax.ShapeDtypeStruct((T, n), out_dtype),
        compiler_params=_params("parallel"),
    )(x2, g, w)


def _xattn_kernel(x_ref, g_ref, wq_ref, k_ref, v_ref, wo_ref, o_ref):
    x = x_ref[...]
    D = x.shape[1]
    dh = D // XA_HEADS
    q = _dot(_rms(x, g_ref[...]).astype(BF16), wq_ref[...]) * (dh ** -0.5)
    outs = []
    for h in range(XA_HEADS):
        c = slice(h * dh, (h + 1) * dh)
        s = _dot_nt(q[:, c].astype(BF16), k_ref[0, :, c])
        e = jnp.exp(s - jnp.max(s, axis=-1, keepdims=True))
        p = e / jnp.sum(e, axis=-1, keepdims=True)
        outs.append(_dot(p.astype(BF16), v_ref[0, :, c]))
    o = jnp.concatenate(outs, axis=1)
    o_ref[...] = x + _dot(o.astype(BF16), wo_ref[...])


def _xattn(x2, g, wq, memkv, wo, B, S, tm):
    T, D = x2.shape
    ns = S // tm
    M = memkv.shape[1]
    return pl.pallas_call(
        _xattn_kernel,
        grid=(B, ns),
        in_specs=[pl.BlockSpec((tm, D), lambda b, i: (b * ns + i, 0)), _full((1, D)), _full((D, D)),
                  pl.BlockSpec((1, M, D), lambda b, i: (b, 0, 0)),
                  pl.BlockSpec((1, M, D), lambda b, i: (b, 0, 1)),
                  _full((D, D))],
        out_specs=pl.BlockSpec((tm, D), lambda b, i: (b * ns + i, 0)),
        out_shape=jax.ShapeDtypeStruct((T, D), F32),
        compiler_params=_params("parallel", "parallel"),
    )(x2, g, wq, memkv, memkv, wo)


PAIR_LIST = tuple((a, b) for a in range(PEER_TOPK) for b in range(PEER_TOPK) if (a + 1) * (b + 1) <= PEER_TOPK)
N_PAIR_ROWS = -(-len(PAIR_LIST) // 8) * 8


def _top_rows(v, n):
    R = v.shape[0]
    ridx = lax.broadcasted_iota(jnp.int32, v.shape, 0)
    outs = []
    for r in range(n):
        m = jnp.max(v, axis=0, keepdims=True)
        outs.append(m)
        if r + 1 < n:
            first = jnp.min(jnp.where(v == m, ridx, R), axis=0, keepdims=True)
            v = jnp.where(ridx == first, LOWEST, v)
    return outs


def _peer_route_kernel(x_ref, g_ref, wqt_ref, sk_ref, hn_ref, s_ref, gt_ref, tau_ref, cand_sc, cw_sc):
    hn = _rms(x_ref[...], g_ref[...]).astype(BF16)
    hn_ref[...] = hn
    qt = _dot_nt(wqt_ref[...], hn)
    nk = PEER_KEYS
    half_rows = PEER_HEADS * nk
    cand_sc[...] = jnp.full(cand_sc.shape, LOWEST, F32)
    cw_sc[...] = jnp.zeros(cw_sc.shape, F32)
    for h in range(PEER_HEADS):
        r0 = h * 2 * nk
        s1 = _dot(sk_ref[0], qt[r0:r0 + nk].astype(BF16))
        s2 = _dot(sk_ref[1], qt[r0 + nk:r0 + 2 * nk].astype(BF16))
        top1 = _top_rows(s1, PEER_TOPK)
        top2 = _top_rows(s2, PEER_TOPK)
        e1 = [jnp.exp(t - top1[0]) for t in top1]
        e2 = [jnp.exp(t - top2[0]) for t in top2]
        for r, (a, b) in enumerate(PAIR_LIST):
            cand_sc[r:r + 1, :] = top1[a] + top2[b]
            cw_sc[r:r + 1, :] = e1[a] * e2[b]
        cand = cand_sc[...]
        tau = _top_rows(cand, PEER_TOPK)[-1]
        z = jnp.sum(jnp.where(cand >= tau, cw_sc[...], 0.0), axis=0, keepdims=True)
        s_ref[h * nk:(h + 1) * nk, :] = s1
        s_ref[half_rows + h * nk:half_rows + (h + 1) * nk, :] = s2
        gt_ref[h * nk:(h + 1) * nk, :] = jnp.exp(s1 - top1[0]) * (1.0 / z)
        gt_ref[half_rows + h * nk:half_rows + (h + 1) * nk, :] = jnp.exp(s2 - top2[0])
        tau_ref[h:h + 1, :] = tau


def _peer_route(x2, g, wqt, sk, tm):
    T, D = x2.shape
    nq = wqt.shape[0]
    col = lambda r: pl.BlockSpec((r, tm), lambda i: (0, i))
    return pl.pallas_call(
        _peer_route_kernel,
        grid=(T // tm,),
        in_specs=[pl.BlockSpec((tm, D), lambda i: (i, 0)), _full((1, D)), _full(wqt.shape), _full(sk.shape)],
        out_specs=[pl.BlockSpec((tm, D), lambda i: (i, 0)), col(nq), col(nq), col(PEER_HEADS)],
        out_shape=[jax.ShapeDtypeStruct((T, D), BF16), jax.ShapeDtypeStruct((nq, T), F32),
                   jax.ShapeDtypeStruct((nq, T), F32), jax.ShapeDtypeStruct((PEER_HEADS, T), F32)],
        scratch_shapes=[pltpu.VMEM((N_PAIR_ROWS, tm), F32), pltpu.VMEM((N_PAIR_ROWS, tm), F32)],
        compiler_params=_params("parallel"),
    )(x2, g, wqt, sk)


def _peer_expert_kernel(hn_ref, s_ref, gt_ref, tau_ref, u_ref, vt_ref, x_ref, o_ref, acc_sc, wa_sc, *, te, sub):
    c = pl.program_id(1)
    nk = PEER_KEYS
    half_rows = PEER_HEADS * nk

    @pl.when(c == 0)
    def _():
        acc_sc[...] = jnp.zeros(acc_sc.shape, F32)

    hn = hn_ref[...]
    tm = hn.shape[0]

    def pre_act(j):
        return _dot_nt(u_ref[j * sub:(j + 1) * sub, :], hn)

    nxt = pre_act(0)
    for j in range(te // sub):
        act = nxt
        if j + 1 < te // sub:
            nxt = pre_act(j + 1)
        for il in range(sub // nk):
            i = c * (te // nk) + j * (sub // nk) + il
            w = jnp.zeros((nk, tm), F32)
            for h in range(PEER_HEADS):
                s1row = s_ref[pl.ds(h * nk + i, 1), :]
                g1row = gt_ref[pl.ds(h * nk + i, 1), :]
                lo = half_rows + h * nk
                hit = s_ref[lo:lo + nk, :] + s1row >= tau_ref[h:h + 1, :]
                w = w + jnp.where(hit, gt_ref[lo:lo + nk, :], 0.0) * g1row
            wa_sc[j, il * nk:(il + 1) * nk, :] = (w * _gelu(act[il * nk:(il + 1) * nk])).astype(BF16)
        acc_sc[...] += _dot(vt_ref[:, j * sub:(j + 1) * sub], wa_sc[j])

    @pl.when(c == pl.num_programs(1) - 1)
    def _():
        o_ref[...] = x_ref[...] + acc_sc[...].T


def _peer_expert(hn, sT, gT, tau, u, vt, x2, tm, te, sub):
    T, D = x2.shape
    ne = u.shape[0]
    nq = sT.shape[0]
    tok = lambda r: pl.BlockSpec((r, tm), lambda i, c: (0, i))
    row = pl.BlockSpec((tm, D), lambda i, c: (i, 0))
    return pl.pallas_call(
        functools.partial(_peer_expert_kernel, te=te, sub=sub),
        grid=(T // tm, ne // te),
        in_specs=[row, tok(nq), tok(nq), tok(PEER_HEADS),
                  pl.BlockSpec((te, D), lambda i, c: (c, 0)),
                  pl.BlockSpec((D, te), lambda i, c: (0, c)),
                  row],
        out_specs=row,
        out_shape=jax.ShapeDtypeStruct((T, D), F32),
        scratch_shapes=[pltpu.VMEM((D, tm), F32), pltpu.VMEM((te // sub, sub, tm), BF16)],
        compiler_params=_params("parallel", "arbitrary"),
    )(hn, sT, gT, tau, u, vt, x2)


def _final_norm_kernel(x_ref, g_ref, o_ref):
    o_ref[...] = _rms(x_ref[...], g_ref[...])


def _final_norm(x2, g, tm):
    T, D = x2.shape
    return pl.pallas_call(
        _final_norm_kernel,
        grid=(T // tm,),
        in_specs=[pl.BlockSpec((tm, D), lambda i: (i, 0)), _full((1, D))],
        out_specs=pl.BlockSpec((tm, D), lambda i: (i, 0)),
        out_shape=jax.ShapeDtypeStruct((T, D), F32),
        compiler_params=_params("parallel"),
    )(x2, g)


def _rope_tables(positions):
    B, S = positions.shape
    freqs = ROPE_THETA ** (-jnp.arange(ROT_HALF, dtype=F32) * 2.0 / ROT_DIM)
    ang = positions.astype(F32)[:, :, None] * freqs
    cos, sin = jnp.cos(ang), jnp.sin(ang)
    ones = jnp.ones((B, S, HEAD_DIM - ROT_DIM), F32)
    zeros8 = jnp.zeros((B, S, ROT_HALF), F32)
    zeros = jnp.zeros((B, S, HEAD_DIM - ROT_DIM), F32)
    cos_h = jnp.concatenate([cos, cos, ones], axis=-1)
    sa_h = jnp.concatenate([zeros8, sin, zeros], axis=-1)
    sb_h = jnp.concatenate([-sin, zeros8, zeros], axis=-1)
    rep = LANES // HEAD_DIM
    tile = lambda t: jnp.tile(t, (1, 1, rep)).reshape(B * S, LANES)
    return tile(cos_h), tile(sa_h), tile(sb_h)


def _overlap_matrix(n_chunk, n_sel):
    ci = np.arange(n_chunk)[:, None] * CMP_STRIDE
    sj = np.arange(LANES)[None, :] * SEL_BLOCK
    ov = (ci < sj + SEL_BLOCK) & (ci + CMP_BLOCK > sj) & (np.arange(LANES)[None, :] < n_sel)
    ov = ov & (np.arange(n_chunk)[:, None] < n_chunk - 1)
    return jnp.asarray(ov, BF16)


def kernel(x, mem, positions, norm_mix_g, w_in, cmp_pe, cmp_w1, cmp_w2, conv_w, conv_b, conv_ln_g, conv_ln_b, conv_pw_w, conv_pw_b, pool_w, pool_scale, w_out, norm_xa_g, norm_mem_g, xa_wq, xa_wkv, xa_wo, norm_ffn_g, peer_wq, peer_subkeys, peer_u, peer_v, final_g):
    B, S, D = x.shape
    T = B * S
    depth = w_in.shape[0]
    M = mem.shape[1]
    dc = conv_w.shape[-1]
    n_chunk = S // CMP_STRIDE
    n_sel = S // SEL_BLOCK
    assert S % 512 == 0 and n_sel <= LANES and n_sel >= SEL_TOPN and D % LANES == 0

    cos_t, sa_t, sb_t = _rope_tables(positions)
    overlap = _overlap_matrix(n_chunk, n_sel)
    row = lambda v: v.reshape(1, -1)
    x2 = x.reshape(T, D)
    mem2 = mem.reshape(B * M, D)
    n_gate = 3 * NSA_HEADS

    for l in range(depth):
        w = w_in[l]
        w_pad = jnp.concatenate([w[:, :D_QKV + n_gate], jnp.zeros((D, LANES - n_gate), F32), w[:, D_QKV + n_gate:]],
                                axis=1).astype(BF16)
        q, kv, misc = _inproj(x2, row(norm_mix_g[l]), w_pad, cos_t, sa_t, sb_t, tm=512)
        xc = kv[:, :2 * LANES].reshape(B, S, 2, NSA_KV_HEADS, HEAD_DIM)
        xc = jnp.transpose(xc, (2, 0, 3, 1, 4)).reshape(2, B * NSA_KV_HEADS, n_chunk, CMP_STRIDE * HEAD_DIM)
        pe = cmp_pe[l].reshape(2, 2, CMP_STRIDE * HEAD_DIM)
        cmp = _compress(xc, pe, cmp_w1[l].astype(BF16), cmp_w2[l].astype(BF16))
        cmp = cmp.reshape(2, B, NSA_KV_HEADS, n_chunk, HEAD_DIM)
        cmp = jnp.transpose(cmp, (0, 1, 3, 2, 4)).reshape(2, B, n_chunk, LANES).astype(BF16)
        o_nsa = _nsa(q, cmp[0], cmp[1], kv.reshape(B, S, -1), misc, overlap, B, S, tq=128)
        pg = dc // len(POOL_WINDOWS)
        plw = jnp.zeros((dc, dc), F32)
        for gi in range(len(POOL_WINDOWS)):
            plw = plw.at[gi * pg:(gi + 1) * pg, gi * pg:(gi + 1) * pg].set(pool_w[l, gi])
        x2 = _mixout(misc, o_nsa, x2, conv_w[l], row(conv_b[l]), row(conv_ln_g[l]), row(conv_ln_b[l]),
                     conv_pw_w[l].astype(BF16), row(conv_pw_b[l]), plw.astype(BF16), row(pool_scale[l]),
                     w_out[l].astype(BF16), B, S, ts=512)
        memkv = _norm_matmul(mem2, row(norm_mem_g[l]), xa_wkv[l].astype(BF16), tm=256, out_dtype=BF16)
        x2 = _xattn(x2, row(norm_xa_g[l]), xa_wq[l].astype(BF16), memkv.reshape(B, M, 2 * D),
                    xa_wo[l].astype(BF16), B, S, tm=512)
        hn, sT, gT, tau = _peer_route(x2, row(norm_ffn_g[l]), peer_wq[l].T.astype(BF16),
                                      peer_subkeys[l].astype(BF16), tm=512)
        x2 = _peer_expert(hn, sT, gT, tau, peer_u[l].astype(BF16), peer_v[l].T.astype(BF16), x2, tm=512, te=1024, sub=256)
    return _final_norm(x2, row(final_g), tm=512).reshape(B, S, D)
```

```python
import functools
import math

import jax
import jax.numpy as jnp
import numpy as np
from jax import lax
from jax.experimental import pallas as pl
from jax.experimental.pallas import tpu as pltpu

F32 = jnp.float32
BF16 = jnp.bfloat16

NSA_HEADS = 8
NSA_KV_HEADS = 2
NSA_GROUP = NSA_HEADS // NSA_KV_HEADS
HEAD_DIM = 64
D_NSA = NSA_HEADS * HEAD_DIM
D_KV = NSA_KV_HEADS * HEAD_DIM
ROT_DIM = HEAD_DIM // 4
ROT_HALF = ROT_DIM // 2
ROPE_THETA = 500000.0
CMP_BLOCK = 32
CMP_STRIDE = 16
SEL_BLOCK = 64
SEL_TOPN = 16
WINDOW = 512
CONV_WIDTH = 31
POOL_WINDOWS = (2, 4, 8, 16)
XA_HEADS = 4
PEER_HEADS = 8
PEER_KEYS = 128
PEER_TOPK = 16
EPS = 1e-6
NEG = -1e30
FORCE = 1e4
LOWEST = -3.0e38

LANES = 128
VMEM_LIMIT = 48 * 1024 * 1024

NT_DIMS = (((1,), (1,)), ((), ()))


def _params(*sem):
    return pltpu.CompilerParams(dimension_semantics=sem, vmem_limit_bytes=VMEM_LIMIT)


def _full(shape):
    nd = len(shape)
    return pl.BlockSpec(shape, lambda *_: (0,) * nd)


def _rms(x, g):
    return x * lax.rsqrt(jnp.mean(x * x, axis=-1, keepdims=True) + EPS) * g


def _gelu(x):
    c = math.sqrt(2.0 / math.pi)
    return 0.5 * x * (1.0 + jnp.tanh(c * (x + 0.044715 * (x * x * x))))


def _dot(a, b):
    return jnp.dot(a, b, preferred_element_type=F32)


def _dot_nt(a, b):
    return lax.dot_general(a, b, NT_DIMS, preferred_element_type=F32)


N_ROPE_Q = D_NSA // LANES
N_KV_CHUNKS = 6
D_QKV = D_NSA + N_KV_CHUNKS * D_KV


def _inproj_kernel(x_ref, g_ref, w_ref, cos_ref, sa_ref, sb_ref, q_ref, kv_ref, misc_ref):
    y = _rms(x_ref[...], g_ref[...])
    z = _dot(y.astype(BF16), w_ref[...])
    cos, sa, sb = cos_ref[...], sa_ref[...], sb_ref[...]

    def rope(c):
        return c * cos + pltpu.roll(c, ROT_HALF, 1) * sa + pltpu.roll(c, LANES - ROT_HALF, 1) * sb

    scale = HEAD_DIM ** -0.5
    for j in range(N_ROPE_Q):
        q_ref[:, j * LANES:(j + 1) * LANES] = (rope(z[:, j * LANES:(j + 1) * LANES]) * scale).astype(BF16)
    for j in range(N_KV_CHUNKS):
        c = z[:, D_NSA + j * LANES:D_NSA + (j + 1) * LANES]
        if j % 2 == 0:
            c = rope(c)
        kv_ref[:, j * LANES:(j + 1) * LANES] = c.astype(BF16)
    misc_ref[...] = z[:, D_QKV:]


def _inproj(x2, g, w_pad, cos_t, sa_t, sb_t, tm):
    T, D = x2.shape
    n = w_pad.shape[1]
    n_misc = n - D_QKV
    row = lambda w: pl.BlockSpec((tm, w), lambda i: (i, 0))
    return pl.pallas_call(
        _inproj_kernel,
        grid=(T // tm,),
        in_specs=[row(D), _full((1, D)), _full((D, n)), row(LANES), row(LANES), row(LANES)],
        out_specs=[row(D_NSA), row(N_KV_CHUNKS * LANES), row(n_misc)],
        out_shape=[jax.ShapeDtypeStruct((T, D_NSA), BF16),
                   jax.ShapeDtypeStruct((T, N_KV_CHUNKS * LANES), BF16),
                   jax.ShapeDtypeStruct((T, n_misc), F32)],
        compiler_params=_params("parallel"),
    )(x2, g, w_pad, cos_t, sa_t, sb_t)


def _compress_kernel(x_ref, pe_ref, w1_ref, w2_ref, o_ref):
    x = x_ref[0, 0].astype(F32)
    pe = pe_ref[0]
    half = x.shape[1]
    a = _dot((x + pe[0:1]).astype(BF16), w1_ref[0, :half])
    b = _dot((x + pe[1:2]).astype(BF16), w1_ref[0, half:])
    n = x.shape[0]
    hid = _gelu(a + pltpu.roll(b, n - 1, 0))
    o_ref[0, 0] = _dot(hid.astype(BF16), w2_ref[0])


def _compress(xc, pe, w1, w2):
    two, nb, nch, width = xc.shape
    hid = w1.shape[-1]
    return pl.pallas_call(
        _compress_kernel,
        grid=(two, nb),
        in_specs=[pl.BlockSpec((1, 1, nch, width), lambda w, i: (w, i, 0, 0)),
                  pl.BlockSpec((1, 2, width), lambda w, i: (w, 0, 0)),
                  pl.BlockSpec((1, 2 * width, hid), lambda w, i: (w, 0, 0)),
                  pl.BlockSpec((1, hid, HEAD_DIM), lambda w, i: (w, 0, 0))],
        out_specs=pl.BlockSpec((1, 1, nch, HEAD_DIM), lambda w, i: (w, i, 0, 0)),
        out_shape=jax.ShapeDtypeStruct((two, nb, nch, HEAD_DIM), F32),
        compiler_params=_params("parallel", "parallel"),
    )(xc, pe, w1, w2)


def _nsa_kernel(q_ref, kcmp_ref, vcmp_ref, ks_ref, vs_ref, kw_ref, vw_ref, gate_ref, ov_ref, o_ref,
                m_sc, l_sc, acc_sc, *, tq, kc_sel, kc_win):
    t0 = pl.program_id(1) * tq
    rows = NSA_GROUP * tq
    lane = lax.broadcasted_iota(jnp.int32, (tq, LANES), 1)
    tpos = t0 + lax.broadcasted_iota(jnp.int32, (tq, 1), 0)
    gates = jax.nn.sigmoid(gate_ref[...])
    n_cmp = kcmp_ref.shape[1]

    def flash(k_ref, v_ref, qs, c_lo, c_hi, kc, mask_fn):
        m_sc[...] = jnp.full((rows, 1), NEG, F32)
        l_sc[...] = jnp.zeros((rows, 1), F32)
        acc_sc[...] = jnp.zeros((rows, LANES), F32)

        def body(c, carry):
            k0 = pl.multiple_of(c * kc, kc)
            kb = k_ref[0, pl.ds(k0, kc), :]
            vb = v_ref[0, pl.ds(k0, kc), :]
            s = _dot_nt(qs, kb).reshape(NSA_GROUP, tq, kc)
            kpos = k0 + lax.broadcasted_iota(jnp.int32, (tq, kc), 1)
            bias = jnp.where(mask_fn(k0, kpos), 0.0, NEG)
            s = (s + bias[None]).reshape(rows, kc)
            m_old = m_sc[...]
            m_new = jnp.maximum(m_old, jnp.max(s, axis=1, keepdims=True))
            alpha = jnp.exp(m_old - m_new)
            p = jnp.exp(s - m_new)
            l_sc[...] = alpha * l_sc[...] + jnp.sum(p, axis=1, keepdims=True)
            acc_sc[...] = alpha * acc_sc[...] + _dot(p.astype(BF16), vb)
            m_sc[...] = m_new
            return carry

        lax.fori_loop(c_lo, c_hi, body, 0)
        return acc_sc[...] / l_sc[...]

    for k in range(NSA_KV_HEADS):
        keep = (lane >= HEAD_DIM) if k == 1 else (lane < HEAD_DIM)
        parts = []
        for g in range(NSA_GROUP):
            hh = k * NSA_GROUP + g
            c = q_ref[:, (hh // 2) * LANES:(hh // 2 + 1) * LANES].astype(F32)
            if hh % 2 != k:
                c = pltpu.roll(c, HEAD_DIM, 1)
            parts.append(jnp.where(keep, c, 0.0))
        qs = jnp.concatenate(parts, axis=0).astype(BF16)

        s = _dot_nt(qs, kcmp_ref[0]).reshape(NSA_GROUP, tq, n_cmp)
        cmp_end = lax.broadcasted_iota(jnp.int32, (tq, n_cmp), 1) * CMP_STRIDE + (CMP_BLOCK - 1)
        valid = cmp_end <= tpos
        s = s + jnp.where(valid, 0.0, NEG)[None]
        e = jnp.exp(s - jnp.max(s, axis=-1, keepdims=True)) * valid.astype(F32)[None]
        l = jnp.sum(e, axis=-1, keepdims=True)
        p3 = e / jnp.where(l > 0.0, l, 1.0)
        o_cmp = _dot(p3.reshape(rows, n_cmp).astype(BF16), vcmp_ref[0])

        psum = p3[0] + p3[1] + p3[2] + p3[3]
        ov = ov_ref[...]
        p_hi = psum.astype(BF16)
        r1 = psum - p_hi.astype(F32)
        p_mid = r1.astype(BF16)
        p_lo = (r1 - p_mid.astype(F32)).astype(BF16)
        imp = _dot(p_hi, ov) + _dot(p_mid, ov) + _dot(p_lo, ov)
        imp = imp.T
        blk = lax.broadcasted_iota(jnp.int32, (LANES, tq), 0)
        tpos_t = t0 + lax.broadcasted_iota(jnp.int32, (1, tq), 1)
        cur = tpos_t // SEL_BLOCK
        forced = (blk == 0) | (blk == cur) | (blk == cur - 1)
        imp = jnp.where(blk * SEL_BLOCK <= tpos_t, jnp.where(forced, FORCE, imp), NEG)
        n_sel = ks_ref.shape[1] // SEL_BLOCK
        rank = jnp.zeros((LANES, tq), jnp.int32)
        for jp in range(n_sel):
            one = imp[jp:jp + 1, :]
            beats = (one > imp) | ((one == imp) & (blk > jp))
            rank = rank + beats.astype(jnp.int32)
        sel = (rank < SEL_TOPN).astype(F32).T.astype(BF16)

        def sel_mask(k0, kpos):
            jrow = lax.broadcasted_iota(jnp.int32, (LANES, kc_sel), 0)
            kcol = lax.broadcasted_iota(jnp.int32, (LANES, kc_sel), 1)
            expand = (jrow == k0 // SEL_BLOCK + kcol // SEL_BLOCK).astype(BF16)
            return (_dot(sel, expand) > 0.5) & (kpos <= tpos)

        o_sel = flash(ks_ref, vs_ref, qs, 0, (t0 + tq + kc_sel - 1) // kc_sel, kc_sel, sel_mask)

        def win_mask(k0, kpos):
            return (kpos <= tpos) & (kpos > tpos - WINDOW)

        w0 = pl.multiple_of(jnp.maximum(t0 + tq - kc_win, 0), tq)
        sw = _dot_nt(qs, kw_ref[0, pl.ds(w0, kc_win), :]).reshape(NSA_GROUP, tq, kc_win)
        wpos = w0 + lax.broadcasted_iota(jnp.int32, (tq, kc_win), 1)
        sw = (sw + jnp.where(win_mask(w0, wpos), 0.0, NEG)[None]).reshape(rows, kc_win)
        pw = jnp.exp(sw - jnp.max(sw, axis=1, keepdims=True))
        o_win = _dot(pw.astype(BF16), vw_ref[0, pl.ds(w0, kc_win), :]) / jnp.sum(pw, axis=1, keepdims=True)

        outs = []
        for g in range(NSA_GROUP):
            r = slice(g * tq, (g + 1) * tq)
            gi = (k * NSA_GROUP + g) * 3
            og = (gates[:, gi:gi + 1] * o_cmp[r] + gates[:, gi + 1:gi + 2] * o_sel[r]
                  + gates[:, gi + 2:gi + 3] * o_win[r])
            if g % 2 != k:
                og = pltpu.roll(og, HEAD_DIM, 1)
            outs.append(og)
        for j in range(NSA_GROUP // 2):
            chunk = jnp.where(lane < HEAD_DIM, outs[2 * j], outs[2 * j + 1])
            cj = k * (NSA_GROUP // 2) + j
            o_ref[:, cj * LANES:(cj + 1) * LANES] = chunk.astype(BF16)


def _nsa(q, kcmp, vcmp, kv, gates_misc, overlap, B, S, tq):
    kc_sel, kc_win = 512, WINDOW + tq
    nq = S // tq
    n_cmp = kcmp.shape[1]
    rows = NSA_GROUP * tq
    kvspec = lambda j: pl.BlockSpec((1, S, LANES), lambda b, i: (b, 0, j))
    return pl.pallas_call(
        functools.partial(_nsa_kernel, tq=tq, kc_sel=kc_sel, kc_win=kc_win),
        grid=(B, nq),
        in_specs=[pl.BlockSpec((tq, D_NSA), lambda b, i: (b * nq + i, 0)),
                  pl.BlockSpec((1, n_cmp, LANES), lambda b, i: (b, 0, 0)),
                  pl.BlockSpec((1, n_cmp, LANES), lambda b, i: (b, 0, 0)),
                  kvspec(2), kvspec(3), kvspec(4), kvspec(5),
                  pl.BlockSpec((tq, LANES), lambda b, i: (b * nq + i, 0)),
                  _full(overlap.shape)],
        out_specs=pl.BlockSpec((tq, D_NSA), lambda b, i: (b * nq + i, 0)),
        out_shape=jax.ShapeDtypeStruct((B * S, D_NSA), BF16),
        scratch_shapes=[pltpu.VMEM((rows, 1), F32), pltpu.VMEM((rows, 1), F32), pltpu.VMEM((rows, LANES), F32)],
        compiler_params=_params("parallel", "parallel"),
    )(q, kcmp, vcmp, kv, kv, kv, kv, gates_misc, overlap)


HALO = 32


def _mixout_kernel(mc_ref, mp_ref, on_ref, x_ref, cw_ref, cb_ref, lg_ref, lb_ref, pw_ref, pb_ref, plw_ref,
                   pls_ref, wo_ref, o_ref, hbuf, pbuf, *, ts, dc):
    i = pl.program_id(1)
    first = i == 0
    off_a, off_b, off_p = LANES, LANES + dc, LANES + 2 * dc

    def glu(ref, r):
        return ref[r, off_a:off_a + dc] * jax.nn.sigmoid(ref[r, off_b:off_b + dc])

    tail = slice(ts - HALO, ts)
    hbuf[0:HALO, :] = jnp.where(first, 0.0, glu(mp_ref, tail))
    hbuf[HALO:, :] = glu(mc_ref, slice(None))
    pbuf[0:HALO, :] = jnp.where(first, 0.0, mp_ref[tail, off_p:off_p + dc])
    pcur = mc_ref[:, off_p:off_p + dc]
    pbuf[HALO:, :] = pcur

    acc = jnp.zeros((ts, dc), F32) + cb_ref[...]
    for w in range(CONV_WIDTH):
        acc = acc + hbuf[pl.ds(HALO - (CONV_WIDTH - 1) + w, ts), :] * cw_ref[w:w + 1, :]
    mu = jnp.mean(acc, axis=-1, keepdims=True)
    xc = acc - mu
    y = xc * lax.rsqrt(jnp.mean(xc * xc, axis=-1, keepdims=True) + EPS) * lg_ref[...] + lb_ref[...]
    y = y * jax.nn.sigmoid(y)
    o_conv = _dot(y.astype(BF16), pw_ref[...]) + pb_ref[...]

    tglob = i * ts + lax.broadcasted_iota(jnp.int32, (ts, dc), 0)
    lane = lax.broadcasted_iota(jnp.int32, (ts, dc), 1)
    pg = dc // len(POOL_WINDOWS)
    run = pcur
    d = 1
    mean = jnp.zeros((ts, dc), F32)
    for gi, w in enumerate(POOL_WINDOWS):
        while d < w:
            run = run + pbuf[pl.ds(HALO - d, ts), :]
            d += 1
        cnt = jnp.minimum(tglob + 1, w).astype(F32)
        mean = jnp.where(lane // pg == gi, run / cnt, mean)
    o_pool = _dot((mean - pcur).astype(BF16), plw_ref[...]) * pls_ref[...]

    dn = on_ref.shape[1]
    o_ref[...] = (x_ref[...] + _dot(on_ref[...], wo_ref[0:dn, :])
                  + _dot(o_conv.astype(BF16), wo_ref[dn:dn + dc, :])
                  + _dot(o_pool.astype(BF16), wo_ref[dn + dc:, :]))


def _mixout(misc, o_nsa, x2, cw, cb, lg, lb, pw, pb, plw, pls, wo, B, S, ts):
    T, D = x2.shape
    dc = cw.shape[1]
    ns = S // ts
    nm = misc.shape[1]
    cur = lambda w: pl.BlockSpec((ts, w), lambda b, i: (b * ns + i, 0))
    prev = pl.BlockSpec((ts, nm), lambda b, i: (b * ns + jnp.maximum(i - 1, 0), 0))
    return pl.pallas_call(
        functools.partial(_mixout_kernel, ts=ts, dc=dc),
        grid=(B, ns),
        in_specs=[cur(nm), prev, cur(o_nsa.shape[1]), cur(D), _full(cw.shape), _full(cb.shape), _full(lg.shape),
                  _full(lb.shape), _full(pw.shape), _full(pb.shape), _full(plw.shape), _full(pls.shape),
                  _full(wo.shape)],
        out_specs=cur(D),
        out_shape=jax.ShapeDtypeStruct((T, D), F32),
        scratch_shapes=[pltpu.VMEM((ts + HALO, dc), F32), pltpu.VMEM((ts + HALO, dc), F32)],
        compiler_params=_params("parallel", "parallel"),
    )(misc, misc, o_nsa, x2, cw, cb, lg, lb, pw, pb, plw, pls, wo)


def _norm_matmul_kernel(x_ref, g_ref, w_ref, o_ref):
    o_ref[...] = _dot(_rms(x_ref[...], g_ref[...]).astype(BF16), w_ref[...]).astype(o_ref.dtype)


def _norm_matmul(x2, g, w, tm, out_dtype):
    T, D = x2.shape
    n = w.shape[1]
    return pl.pallas_call(
        _norm_matmul_kernel,
        grid=(T // tm,),
        in_specs=[pl.BlockSpec((tm, D), lambda i: (i, 0)), _full((1, D)), _full((D, n))],
        out_specs=pl.BlockSpec((tm, n), lambda i: (i, 0)),
        out_shape=jax.ShapeDtypeStruct((T, n), out_dtype),
        compiler_params=_params("parallel"),
    )(x2, g, w)


def _xattn_kernel(x_ref, g_ref, wq_ref, k_ref, v_ref, wo_ref, o_ref):
    x = x_ref[...]
    D = x.shape[1]
    dh = D // XA_HEADS
    q = _dot(_rms(x, g_ref[...]).astype(BF16), wq_ref[...]) * (dh ** -0.5)
    outs = []
    for h in range(XA_HEADS):
        c = slice(h * dh, (h + 1) * dh)
        s = _dot_nt(q[:, c].astype(BF16), k_ref[0, :, c])
        e = jnp.exp(s - jnp.max(s, axis=-1, keepdims=True))
        p = e / jnp.sum(e, axis=-1, keepdims=True)
        outs.append(_dot(p.astype(BF16), v_ref[0, :, c]))
    o = jnp.concatenate(outs, axis=1)
    o_ref[...] = x + _dot(o.astype(BF16), wo_ref[...])


def _xattn(x2, g, wq, memkv, wo, B, S, tm):
    T, D = x2.shape
    ns = S // tm
    M = memkv.shape[1]
    return pl.pallas_call(
        _xattn_kernel,
        grid=(B, ns),
        in_specs=[pl.BlockSpec((tm, D), lambda b, i: (b * ns + i, 0)), _full((1, D)), _full((D, D)),
                  pl.BlockSpec((1, M, D), lambda b, i: (b, 0, 0)),
                  pl.BlockSpec((1, M, D), lambda b, i: (b, 0, 1)),
                  _full((D, D))],
        out_specs=pl.BlockSpec((tm, D), lambda b, i: (b * ns + i, 0)),
        out_shape=jax.ShapeDtypeStruct((T, D), F32),
        compiler_params=_params("parallel", "parallel"),
    )(x2, g, wq, memkv, memkv, wo)


PAIR_LIST = tuple((a, b) for a in range(PEER_TOPK) for b in range(PEER_TOPK) if (a + 1) * (b + 1) <= PEER_TOPK)
N_PAIR_ROWS = -(-len(PAIR_LIST) // 8) * 8
PAIR_COUNT = tuple(PEER_TOPK // (a + 1) for a in range(PEER_TOPK))
PAIR_START = tuple(sum(PAIR_COUNT[:a]) for a in range(PEER_TOPK))


def _top_rows(v, n):
    R = v.shape[0]
    ridx = lax.broadcasted_iota(jnp.int32, v.shape, 0)
    rank = jnp.full(v.shape, float(n), F32)
    vals, firsts = [], []
    for r in range(n):
        m = jnp.max(v, axis=0, keepdims=True)
        first = jnp.min(jnp.where(v == m, ridx, R), axis=0, keepdims=True)
        taken = ridx == first
        v = jnp.where(taken, LOWEST, v)
        rank = jnp.where(taken, float(r), rank)
        vals.append(m)
        firsts.append(first)
    return vals, firsts, rank


def _peer_route_kernel(x_ref, g_ref, wqt_ref, sk_ref, hn_ref, c1_ref, g1_ref, r2_ref, g2_ref, cand_sc, cw_sc):
    hn = _rms(x_ref[...], g_ref[...]).astype(BF16)
    hn_ref[...] = hn
    qt = _dot_nt(wqt_ref[...], hn)
    nk = PEER_KEYS
    cand_sc[...] = jnp.full(cand_sc.shape, LOWEST, F32)
    cw_sc[...] = jnp.zeros(cw_sc.shape, F32)
    ridx = lax.broadcasted_iota(jnp.int32, (nk, qt.shape[1]), 0)
    for h in range(PEER_HEADS):
        r0 = h * 2 * nk
        s1 = _dot(sk_ref[0], qt[r0:r0 + nk].astype(BF16))
        s2 = _dot(sk_ref[1], qt[r0 + nk:r0 + 2 * nk].astype(BF16))
        top1, first1, _ = _top_rows(s1, PEER_TOPK)
        top2, _, rank2 = _top_rows(s2, PEER_TOPK)
        e1 = [jnp.exp(t - top1[0]) for t in top1]
        e2 = [jnp.exp(t - top2[0]) for t in top2]
        for r, (a, b) in enumerate(PAIR_LIST):
            cand_sc[r:r + 1, :] = top1[a] + top2[b]
            cw_sc[r:r + 1, :] = e1[a] * e2[b]
        _, _, crank = _top_rows(cand_sc[...], PEER_TOPK)
        chosen = (crank < float(PEER_TOPK)).astype(F32)
        z = jnp.sum(chosen * cw_sc[...], axis=0, keepdims=True)
        count1 = jnp.zeros((nk, qt.shape[1]), F32)
        for a in range(PEER_TOPK):
            n_a = jnp.sum(chosen[PAIR_START[a]:PAIR_START[a] + PAIR_COUNT[a]], axis=0, keepdims=True)
            count1 = jnp.where(ridx == first1[a], n_a, count1)
        rows = slice(h * nk, (h + 1) * nk)
        c1_ref[rows, :] = count1
        g1_ref[rows, :] = jnp.exp(s1 - top1[0]) * (1.0 / z)
        r2_ref[rows, :] = rank2.astype(BF16)
        g2_ref[rows, :] = jnp.exp(s2 - top2[0]).astype(BF16)


def _peer_route(x2, g, wqt, sk, tm):
    T, D = x2.shape
    nr = PEER_HEADS * PEER_KEYS
    col = pl.BlockSpec((nr, tm), lambda i: (0, i))
    return pl.pallas_call(
        _peer_route_kernel,
        grid=(T // tm,),
        in_specs=[pl.BlockSpec((tm, D), lambda i: (i, 0)), _full((1, D)), _full(wqt.shape), _full(sk.shape)],
        out_specs=[pl.BlockSpec((tm, D), lambda i: (i, 0)), col, col, col, col],
        out_shape=[jax.ShapeDtypeStruct((T, D), BF16), jax.ShapeDtypeStruct((nr, T), F32),
                   jax.ShapeDtypeStruct((nr, T), F32), jax.ShapeDtypeStruct((nr, T), BF16),
                   jax.ShapeDtypeStruct((nr, T), BF16)],
        scratch_shapes=[pltpu.VMEM((N_PAIR_ROWS, tm), F32), pltpu.VMEM((N_PAIR_ROWS, tm), F32)],
        compiler_params=_params("parallel"),
    )(x2, g, wqt, sk)


def _peer_expert_kernel(hn_ref, c1_ref, g1_ref, r2_ref, g2_ref, u_ref, vt_ref, x_ref, o_ref, acc_sc, wa_sc, *,
                        te, sub):
    c = pl.program_id(1)
    nk = PEER_KEYS

    @pl.when(c == 0)
    def _():
        acc_sc[...] = jnp.zeros(acc_sc.shape, F32)

    hn = hn_ref[...]
    tm = hn.shape[0]

    def pre_act(j):
        return _dot_nt(u_ref[j * sub:(j + 1) * sub, :], hn)

    nxt = pre_act(0)
    for j in range(te // sub):
        act = nxt.astype(BF16)
        if j + 1 < te // sub:
            nxt = pre_act(j + 1)
        for il in range(sub // nk):
            i = c * (te // nk) + j * (sub // nk) + il
            w = jnp.zeros((nk, tm), BF16)
            for h in range(PEER_HEADS):
                count = c1_ref[pl.ds(h * nk + i, 1), :].astype(BF16)
                g1row = g1_ref[pl.ds(h * nk + i, 1), :].astype(BF16)
                rows = slice(h * nk, (h + 1) * nk)
                w = w + jnp.where(r2_ref[rows, :] < count, g2_ref[rows, :], 0.0) * g1row
            wa_sc[j, il * nk:(il + 1) * nk, :] = w * _gelu(act[il * nk:(il + 1) * nk])
        acc_sc[...] += _dot(vt_ref[:, j * sub:(j + 1) * sub], wa_sc[j])

    @pl.when(c == pl.num_programs(1) - 1)
    def _():
        o_ref[...] = x_ref[...] + acc_sc[...].T


def _peer_expert(hn, c1, g1, r2, g2, u, vt, x2, tm, te, sub):
    T, D = x2.shape
    ne = u.shape[0]
    tok = pl.BlockSpec((c1.shape[0], tm), lambda i, c: (0, i))
    row = pl.BlockSpec((tm, D), lambda i, c: (i, 0))
    return pl.pallas_call(
        functools.partial(_peer_expert_kernel, te=te, sub=sub),
        grid=(T // tm, ne // te),
        in_specs=[row, tok, tok, tok, tok,
                  pl.BlockSpec((te, D), lambda i, c: (c, 0)),
                  pl.BlockSpec((D, te), lambda i, c: (0, c)),
                  row],
        out_specs=row,
        out_shape=jax.ShapeDtypeStruct((T, D), F32),
        scratch_shapes=[pltpu.VMEM((D, tm), F32), pltpu.VMEM((te // sub, sub, tm), BF16)],
        compiler_params=_params("parallel", "arbitrary"),
    )(hn, c1, g1, r2, g2, u, vt, x2)


def _final_norm_kernel(x_ref, g_ref, o_ref):
    o_ref[...] = _rms(x_ref[...], g_ref[...])


def _final_norm(x2, g, tm):
    T, D = x2.shape
    return pl.pallas_call(
        _final_norm_kernel,
        grid=(T // tm,),
        in_specs=[pl.BlockSpec((tm, D), lambda i: (i, 0)), _full((1, D))],
        out_specs=pl.BlockSpec((tm, D), lambda i: (i, 0)),
        out_shape=jax.ShapeDtypeStruct((T, D), F32),
        compiler_params=_params("parallel"),
    )(x2, g)


def _rope_tables(positions):
    B, S = positions.shape
    freqs = ROPE_THETA ** (-jnp.arange(ROT_HALF, dtype=F32) * 2.0 / ROT_DIM)
    ang = positions.astype(F32)[:, :, None] * freqs
    cos, sin = jnp.cos(ang), jnp.sin(ang)
    ones = jnp.ones((B, S, HEAD_DIM - ROT_DIM), F32)
    zeros8 = jnp.zeros((B, S, ROT_HALF), F32)
    zeros = jnp.zeros((B, S, HEAD_DIM - ROT_DIM), F32)
    cos_h = jnp.concatenate([cos, cos, ones], axis=-1)
    sa_h = jnp.concatenate([zeros8, sin, zeros], axis=-1)
    sb_h = jnp.concatenate([-sin, zeros8, zeros], axis=-1)
    rep = LANES // HEAD_DIM
    tile = lambda t: jnp.tile(t, (1, 1, rep)).reshape(B * S, LANES)
    return tile(cos_h), tile(sa_h), tile(sb_h)


def _overlap_matrix(n_chunk, n_sel):
    ci = np.arange(n_chunk)[:, None] * CMP_STRIDE
    sj = np.arange(LANES)[None, :] * SEL_BLOCK
    ov = (ci < sj + SEL_BLOCK) & (ci + CMP_BLOCK > sj) & (np.arange(LANES)[None, :] < n_sel)
    ov = ov & (np.arange(n_chunk)[:, None] < n_chunk - 1)
    return jnp.asarray(ov, BF16)


def kernel(x, mem, positions, norm_mix_g, w_in, cmp_pe, cmp_w1, cmp_w2, conv_w, conv_b, conv_ln_g, conv_ln_b, conv_pw_w, conv_pw_b, pool_w, pool_scale, w_out, norm_xa_g, norm_mem_g, xa_wq, xa_wkv, xa_wo, norm_ffn_g, peer_wq, peer_subkeys, peer_u, peer_v, final_g):
    B, S, D = x.shape
    T = B * S
    depth = w_in.shape[0]
    M = mem.shape[1]
    dc = conv_w.shape[-1]
    n_chunk = S // CMP_STRIDE
    n_sel = S // SEL_BLOCK
    assert S % 512 == 0 and n_sel <= LANES and n_sel >= SEL_TOPN and D % LANES == 0

    cos_t, sa_t, sb_t = _rope_tables(positions)
    overlap = _overlap_matrix(n_chunk, n_sel)
    row = lambda v: v.reshape(1, -1)
    x2 = x.reshape(T, D)
    mem2 = mem.reshape(B * M, D)
    n_gate = 3 * NSA_HEADS

    for l in range(depth):
        w = w_in[l]
        w_pad = jnp.concatenate([w[:, :D_QKV + n_gate], jnp.zeros((D, LANES - n_gate), F32), w[:, D_QKV + n_gate:]],
                                axis=1).astype(BF16)
        q, kv, misc = _inproj(x2, row(norm_mix_g[l]), w_pad, cos_t, sa_t, sb_t, tm=512)
        xc = kv[:, :2 * LANES].reshape(B, S, 2, NSA_KV_HEADS, HEAD_DIM)
        xc = jnp.transpose(xc, (2, 0, 3, 1, 4)).reshape(2, B * NSA_KV_HEADS, n_chunk, CMP_STRIDE * HEAD_DIM)
        pe = cmp_pe[l].reshape(2, 2, CMP_STRIDE * HEAD_DIM)
        cmp = _compress(xc, pe, cmp_w1[l].astype(BF16), cmp_w2[l].astype(BF16))
        cmp = cmp.reshape(2, B, NSA_KV_HEADS, n_chunk, HEAD_DIM)
        cmp = jnp.transpose(cmp, (0, 1, 3, 2, 4)).reshape(2, B, n_chunk, LANES).astype(BF16)
        o_nsa = _nsa(q, cmp[0], cmp[1], kv.reshape(B, S, -1), misc, overlap, B, S, tq=128)
        pg = dc // len(POOL_WINDOWS)
        plw = jnp.zeros((dc, dc), F32)
        for gi in range(len(POOL_WINDOWS)):
            plw = plw.at[gi * pg:(gi + 1) * pg, gi * pg:(gi + 1) * pg].set(pool_w[l, gi])
        x2 = _mixout(misc, o_nsa, x2, conv_w[l], row(conv_b[l]), row(conv_ln_g[l]), row(conv_ln_b[l]),
                     conv_pw_w[l].astype(BF16), row(conv_pw_b[l]), plw.astype(BF16), row(pool_scale[l]),
                     w_out[l].astype(BF16), B, S, ts=512)
        memkv = _norm_matmul(mem2, row(norm_mem_g[l]), xa_wkv[l].astype(BF16), tm=256, out_dtype=BF16)
        x2 = _xattn(x2, row(norm_xa_g[l]), xa_wq[l].astype(BF16), memkv.reshape(B, M, 2 * D),
                    xa_wo[l].astype(BF16), B, S, tm=512)
        hn, c1, g1, r2, g2 = _peer_route(x2, row(norm_ffn_g[l]), peer_wq[l].T.astype(BF16),
                                         peer_subkeys[l].astype(BF16), tm=512)
        x2 = _peer_expert(hn, c1, g1, r2, g2, peer_u[l].astype(BF16), peer_v[l].T.astype(BF16), x2,
                          tm=512, te=1024, sub=256)
    return _final_norm(x2, row(final_g), tm=512).reshape(B, S, D)
```

```python
import functools
import math

import jax
import jax.numpy as jnp
import numpy as np
from jax import lax
from jax.experimental import pallas as pl
from jax.experimental.pallas import tpu as pltpu

F32 = jnp.float32
BF16 = jnp.bfloat16

NSA_HEADS = 8
NSA_KV_HEADS = 2
NSA_GROUP = NSA_HEADS // NSA_KV_HEADS
HEAD_DIM = 64
D_NSA = NSA_HEADS * HEAD_DIM
D_KV = NSA_KV_HEADS * HEAD_DIM
ROT_DIM = HEAD_DIM // 4
ROT_HALF = ROT_DIM // 2
ROPE_THETA = 500000.0
CMP_BLOCK = 32
CMP_STRIDE = 16
SEL_BLOCK = 64
SEL_TOPN = 16
WINDOW = 512
CONV_WIDTH = 31
POOL_WINDOWS = (2, 4, 8, 16)
XA_HEADS = 4
PEER_HEADS = 8
PEER_KEYS = 128
PEER_TOPK = 16
EPS = 1e-6
NEG = -1e30
FORCE = 1e4
LOWEST = -3.0e38

LANES = 128
VMEM_LIMIT = 56 * 1024 * 1024

NT_DIMS = (((1,), (1,)), ((), ()))


def _params(*sem):
    return pltpu.CompilerParams(dimension_semantics=sem, vmem_limit_bytes=VMEM_LIMIT)


def _full(shape):
    nd = len(shape)
    return pl.BlockSpec(shape, lambda *_: (0,) * nd)


def _rms(x, g):
    return x * lax.rsqrt(jnp.mean(x * x, axis=-1, keepdims=True) + EPS) * g


def _gelu(x):
    c = math.sqrt(2.0 / math.pi)
    return 0.5 * x * (1.0 + jnp.tanh(c * (x + 0.044715 * (x * x * x))))


def _dot(a, b):
    return jnp.dot(a, b, preferred_element_type=F32)


def _dot_nt(a, b):
    return lax.dot_general(a, b, NT_DIMS, preferred_element_type=F32)


N_ROPE_Q = D_NSA // LANES
N_KV_CHUNKS = 6
D_QKV = D_NSA + N_KV_CHUNKS * D_KV
N_KV_OUT = 4 + 2 * NSA_KV_HEADS


def _inproj_kernel(x_ref, g_ref, w_ref, cos_ref, sa_ref, sb_ref, q_ref, kv_ref, misc_ref):
    y = _rms(x_ref[...], g_ref[...])
    z = _dot(y.astype(BF16), w_ref[...])
    cos, sa, sb = cos_ref[...], sa_ref[...], sb_ref[...]

    def rope(c):
        return c * cos + pltpu.roll(c, ROT_HALF, 1) * sa + pltpu.roll(c, LANES - ROT_HALF, 1) * sb

    scale = HEAD_DIM ** -0.5
    for j in range(N_ROPE_Q):
        q_ref[:, j * LANES:(j + 1) * LANES] = (rope(z[:, j * LANES:(j + 1) * LANES]) * scale).astype(BF16)
    lane = lax.broadcasted_iota(jnp.int32, (z.shape[0], LANES), 1)
    out = 0
    for j in range(N_KV_CHUNKS):
        c = z[:, D_NSA + j * LANES:D_NSA + (j + 1) * LANES]
        if j % 2 == 0:
            c = rope(c)
        if j < 2 or j % 2 == 0:
            kv_ref[:, out * LANES:(out + 1) * LANES] = c.astype(BF16)
            out += 1
    for j in (3, 5):
        c = z[:, D_NSA + j * LANES:D_NSA + (j + 1) * LANES]
        for k in range(NSA_KV_HEADS):
            own = (lane >= HEAD_DIM) if k == 1 else (lane < HEAD_DIM)
            kv_ref[:, out * LANES:(out + 1) * LANES] = jnp.where(own, c, 1.0).astype(BF16)
            out += 1
    misc_ref[...] = z[:, D_QKV:]


def _inproj(x2, g, w_pad, cos_t, sa_t, sb_t, tm):
    T, D = x2.shape
    n = w_pad.shape[1]
    n_misc = n - D_QKV
    row = lambda w: pl.BlockSpec((tm, w), lambda i: (i, 0))
    return pl.pallas_call(
        _inproj_kernel,
        grid=(T // tm,),
        in_specs=[row(D), _full((1, D)), _full((D, n)), row(LANES), row(LANES), row(LANES)],
        out_specs=[row(D_NSA), row(N_KV_OUT * LANES), row(n_misc)],
        out_shape=[jax.ShapeDtypeStruct((T, D_NSA), BF16),
                   jax.ShapeDtypeStruct((T, N_KV_OUT * LANES), BF16),
                   jax.ShapeDtypeStruct((T, n_misc), F32)],
        compiler_params=_params("parallel"),
    )(x2, g, w_pad, cos_t, sa_t, sb_t)


def _compress_kernel(x_ref, pe_ref, w1_ref, w2_ref, o_ref):
    x = x_ref[0, 0].astype(F32)
    pe = pe_ref[0]
    half = x.shape[1]
    a = _dot((x + pe[0:1]).astype(BF16), w1_ref[0, :half])
    b = _dot((x + pe[1:2]).astype(BF16), w1_ref[0, half:])
    n = x.shape[0]
    hid = _gelu(a + pltpu.roll(b, n - 1, 0))
    o_ref[0, 0] = _dot(hid.astype(BF16), w2_ref[0])


def _compress(xc, pe, w1, w2):
    two, nb, nch, width = xc.shape
    hid = w1.shape[-1]
    return pl.pallas_call(
        _compress_kernel,
        grid=(two, nb),
        in_specs=[pl.BlockSpec((1, 1, nch, width), lambda w, i: (w, i, 0, 0)),
                  pl.BlockSpec((1, 2, width), lambda w, i: (w, 0, 0)),
                  pl.BlockSpec((1, 2 * width, hid), lambda w, i: (w, 0, 0)),
                  pl.BlockSpec((1, hid, HEAD_DIM), lambda w, i: (w, 0, 0))],
        out_specs=pl.BlockSpec((1, 1, nch, HEAD_DIM), lambda w, i: (w, i, 0, 0)),
        out_shape=jax.ShapeDtypeStruct((two, nb, nch, HEAD_DIM), F32),
        compiler_params=_params("parallel", "parallel"),
    )(xc, pe, w1, w2)


def _nsa_kernel(q_ref, kcmp_ref, vcmp_ref, ks_ref, kw_ref, vs0_ref, vs1_ref, vw0_ref, vw1_ref, gate_ref, ov_ref,
                o_ref, s_sc, mrun_sc, acc_sc, *, tq, kc_sel, kc_win):
    t0 = pl.program_id(1) * tq
    rows = NSA_GROUP * tq
    lane = lax.broadcasted_iota(jnp.int32, (tq, LANES), 1)
    lane_r = lax.broadcasted_iota(jnp.int32, (rows, LANES), 1)
    tpos = t0 + lax.broadcasted_iota(jnp.int32, (tq, 1), 0)
    gates = jax.nn.sigmoid(gate_ref[...])
    n_cmp = kcmp_ref.shape[1]
    n_sel = ks_ref.shape[1] // SEL_BLOCK
    vs_refs, vw_refs = (vs0_ref, vs1_ref), (vw0_ref, vw1_ref)

    def normalise(acc, own):
        return acc / jnp.where(own, pltpu.roll(acc, HEAD_DIM, 1), 1.0)

    def own_lanes(lanes, k):
        return (lanes >= HEAD_DIM) if k == 1 else (lanes < HEAD_DIM)

    def queries(k):
        parts = []
        for g in range(NSA_GROUP):
            hh = k * NSA_GROUP + g
            c = q_ref[:, (hh // 2) * LANES:(hh // 2 + 1) * LANES].astype(F32)
            if hh % 2 != k:
                c = pltpu.roll(c, HEAD_DIM, 1)
            parts.append(jnp.where(own_lanes(lane, k), c, 0.0))
        return jnp.concatenate(parts, axis=0).astype(BF16)

    def compressed(qs):
        s = _dot_nt(qs, kcmp_ref[0]).reshape(NSA_GROUP, tq, n_cmp)
        cmp_end = lax.broadcasted_iota(jnp.int32, (tq, n_cmp), 1) * CMP_STRIDE + (CMP_BLOCK - 1)
        valid = cmp_end <= tpos
        s = s + jnp.where(valid, 0.0, NEG)[None]
        e = jnp.exp(s - jnp.max(s, axis=-1, keepdims=True)) * valid.astype(F32)[None]
        l = jnp.sum(e, axis=-1, keepdims=True)
        p3 = e / jnp.where(l > 0.0, l, 1.0)
        o_cmp = _dot(p3.reshape(rows, n_cmp).astype(BF16), vcmp_ref[0])

        psum = p3[0] + p3[1] + p3[2] + p3[3]
        ov = ov_ref[...]
        p_hi = psum.astype(BF16)
        r1 = psum - p_hi.astype(F32)
        p_mid = r1.astype(BF16)
        p_lo = (r1 - p_mid.astype(F32)).astype(BF16)
        imp = _dot(p_hi, ov) + _dot(p_mid, ov) + _dot(p_lo, ov)
        imp = imp.T[:n_sel]
        blk = lax.broadcasted_iota(jnp.int32, (n_sel, tq), 0)
        tpos_t = t0 + lax.broadcasted_iota(jnp.int32, (1, tq), 1)
        cur = tpos_t // SEL_BLOCK
        forced = (blk == 0) | (blk == cur) | (blk == cur - 1)
        imp = jnp.where(blk * SEL_BLOCK <= tpos_t, jnp.where(forced, FORCE, imp), NEG)
        rank = jnp.zeros((n_sel, tq), jnp.int32)
        for jp in range(n_sel):
            one = imp[jp:jp + 1, :]
            beats = (one > imp) | ((one == imp) & (blk > jp))
            rank = rank + beats.astype(jnp.int32)
        sel_t = jnp.concatenate([(rank < SEL_TOPN).astype(F32), jnp.zeros((LANES - n_sel, tq), F32)], axis=0)
        return o_cmp, sel_t.T.astype(BF16)

    def window(k, qs):
        w0 = pl.multiple_of(jnp.maximum(t0 + tq - kc_win, 0), tq)
        sw = _dot_nt(qs, kw_ref[0, pl.ds(w0, kc_win), :]).reshape(NSA_GROUP, tq, kc_win)
        wpos = w0 + lax.broadcasted_iota(jnp.int32, (tq, kc_win), 1)
        in_win = (wpos <= tpos) & (wpos > tpos - WINDOW)
        sw = (sw + jnp.where(in_win, 0.0, NEG)[None]).reshape(rows, kc_win)
        pw = jnp.exp(sw - jnp.max(sw, axis=1, keepdims=True))
        return normalise(_dot(pw.astype(BF16), vw_refs[k][0, pl.ds(w0, kc_win), :]), own_lanes(lane_r, k))

    heads = range(NSA_KV_HEADS)
    qs = [queries(k) for k in heads]
    o_cmp, sel = zip(*[compressed(qs[k]) for k in heads])
    o_win = [window(k, qs[k]) for k in heads]

    n_chunks = (t0 + tq + kc_sel - 1) // kc_sel
    mrun_sc[...] = jnp.full(mrun_sc.shape, NEG, F32)

    def scores(c, carry):
        k0 = pl.multiple_of(c * kc_sel, kc_sel)
        jrow = lax.broadcasted_iota(jnp.int32, (LANES, kc_sel), 0)
        kcol = lax.broadcasted_iota(jnp.int32, (LANES, kc_sel), 1)
        expand = (jrow == k0 // SEL_BLOCK + kcol // SEL_BLOCK).astype(BF16)
        causal = k0 + lax.broadcasted_iota(jnp.int32, (tq, kc_sel), 1) <= tpos
        kb = ks_ref[0, pl.ds(k0, kc_sel), :]
        for k in heads:
            seen = (_dot(sel[k], expand) > 0.5) & causal
            s = _dot_nt(qs[k], kb).reshape(NSA_GROUP, tq, kc_sel)
            s = (s + jnp.where(seen, 0.0, NEG)[None]).reshape(rows, kc_sel)
            s_sc[k, c] = s
            m = mrun_sc[k]
            for j in range(kc_sel // LANES):
                m = jnp.maximum(m, s[:, j * LANES:(j + 1) * LANES])
            mrun_sc[k] = m
        return carry

    lax.fori_loop(0, n_chunks, scores, 0)
    m_sel = [jnp.broadcast_to(jnp.max(mrun_sc[k], axis=1, keepdims=True), (rows, LANES)) for k in heads]
    acc_sc[...] = jnp.zeros(acc_sc.shape, F32)

    def weigh(c, carry):
        k0 = pl.multiple_of(c * kc_sel, kc_sel)
        for k in heads:
            s = s_sc[k, c]
            p = jnp.concatenate([jnp.exp(s[:, j * LANES:(j + 1) * LANES] - m_sel[k])
                                 for j in range(kc_sel // LANES)], axis=1)
            acc_sc[k] += _dot(p.astype(BF16), vs_refs[k][0, pl.ds(k0, kc_sel), :])
        return carry

    lax.fori_loop(0, n_chunks, weigh, 0)

    for k in heads:
        o_sel = normalise(acc_sc[k], own_lanes(lane_r, k))
        outs = []
        for g in range(NSA_GROUP):
            r = slice(g * tq, (g + 1) * tq)
            gi = (k * NSA_GROUP + g) * 3
            og = (gates[:, gi:gi + 1] * o_cmp[k][r] + gates[:, gi + 1:gi + 2] * o_sel[r]
                  + gates[:, gi + 2:gi + 3] * o_win[k][r])
            if g % 2 != k:
                og = pltpu.roll(og, HEAD_DIM, 1)
            outs.append(og)
        for j in range(NSA_GROUP // 2):
            chunk = jnp.where(lane < HEAD_DIM, outs[2 * j], outs[2 * j + 1])
            cj = k * (NSA_GROUP // 2) + j
            o_ref[:, cj * LANES:(cj + 1) * LANES] = chunk.astype(BF16)


def _nsa(q, kcmp, vcmp, kv, gates_misc, overlap, B, S, tq):
    kc_sel, kc_win = 512, WINDOW + tq
    nq = S // tq
    n_cmp = kcmp.shape[1]
    rows = NSA_GROUP * tq
    kvspec = lambda j: pl.BlockSpec((1, S, LANES), lambda b, i: (b, 0, j))
    return pl.pallas_call(
        functools.partial(_nsa_kernel, tq=tq, kc_sel=kc_sel, kc_win=kc_win),
        grid=(B, nq),
        in_specs=[pl.BlockSpec((tq, D_NSA), lambda b, i: (b * nq + i, 0)),
                  pl.BlockSpec((1, n_cmp, LANES), lambda b, i: (b, 0, 0)),
                  pl.BlockSpec((1, n_cmp, LANES), lambda b, i: (b, 0, 0)),
                  kvspec(2), kvspec(3), kvspec(4), kvspec(5), kvspec(6), kvspec(7),
                  pl.BlockSpec((tq, LANES), lambda b, i: (b * nq + i, 0)),
                  _full(overlap.shape)],
        out_specs=pl.BlockSpec((tq, D_NSA), lambda b, i: (b * nq + i, 0)),
        out_shape=jax.ShapeDtypeStruct((B * S, D_NSA), BF16),
        scratch_shapes=[pltpu.VMEM((NSA_KV_HEADS, S // kc_sel, rows, kc_sel), F32),
                        pltpu.VMEM((NSA_KV_HEADS, rows, LANES), F32), pltpu.VMEM((NSA_KV_HEADS, rows, LANES), F32)],
        compiler_params=_params("parallel", "parallel"),
    )(q, kcmp, vcmp, kv, kv, kv, kv, kv, kv, gates_misc, overlap)


HALO = 32


def _mixout_kernel(mc_ref, mp_ref, on_ref, x_ref, cw_ref, cb_ref, lg_ref, lb_ref, pw_ref, pb_ref, plw_ref,
                   pls_ref, wo_ref, o_ref, hbuf, pbuf, *, ts, dc):
    i = pl.program_id(1)
    first = i == 0
    off_a, off_b, off_p = LANES, LANES + dc, LANES + 2 * dc

    def glu(ref, r):
        return ref[r, off_a:off_a + dc] * jax.nn.sigmoid(ref[r, off_b:off_b + dc])

    tail = slice(ts - HALO, ts)
    hbuf[0:HALO, :] = jnp.where(first, 0.0, glu(mp_ref, tail))
    hbuf[HALO:, :] = glu(mc_ref, slice(None))
    pbuf[0:HALO, :] = jnp.where(first, 0.0, mp_ref[tail, off_p:off_p + dc])
    pcur = mc_ref[:, off_p:off_p + dc]
    pbuf[HALO:, :] = pcur

    acc = jnp.zeros((ts, dc), F32) + cb_ref[...]
    for w in range(CONV_WIDTH):
        acc = acc + hbuf[pl.ds(HALO - (CONV_WIDTH - 1) + w, ts), :] * cw_ref[w:w + 1, :]
    mu = jnp.mean(acc, axis=-1, keepdims=True)
    xc = acc - mu
    y = xc * lax.rsqrt(jnp.mean(xc * xc, axis=-1, keepdims=True) + EPS) * lg_ref[...] + lb_ref[...]
    y = y * jax.nn.sigmoid(y)
    o_conv = _dot(y.astype(BF16), pw_ref[...]) + pb_ref[...]

    tglob = i * ts + lax.broadcasted_iota(jnp.int32, (ts, dc), 0)
    lane = lax.broadcasted_iota(jnp.int32, (ts, dc), 1)
    pg = dc // len(POOL_WINDOWS)
    run = pcur
    d = 1
    mean = jnp.zeros((ts, dc), F32)
    for gi, w in enumerate(POOL_WINDOWS):
        while d < w:
            run = run + pbuf[pl.ds(HALO - d, ts), :]
            d += 1
        cnt = jnp.minimum(tglob + 1, w).astype(F32)
        mean = jnp.where(lane // pg == gi, run / cnt, mean)
    o_pool = _dot((mean - pcur).astype(BF16), plw_ref[...]) * pls_ref[...]

    dn = on_ref.shape[1]
    o_ref[...] = (x_ref[...] + _dot(on_ref[...], wo_ref[0:dn, :])
                  + _dot(o_conv.astype(BF16), wo_ref[dn:dn + dc, :])
                  + _dot(o_pool.astype(BF16), wo_ref[dn + dc:, :]))


def _mixout(misc, o_nsa, x2, cw, cb, lg, lb, pw, pb, plw, pls, wo, B, S, ts):
    T, D = x2.shape
    dc = cw.shape[1]
    ns = S // ts
    nm = misc.shape[1]
    cur = lambda w: pl.BlockSpec((ts, w), lambda b, i: (b * ns + i, 0))
    prev = pl.BlockSpec((ts, nm), lambda b, i: (b * ns + jnp.maximum(i - 1, 0), 0))
    return pl.pallas_call(
        functools.partial(_mixout_kernel, ts=ts, dc=dc),
        grid=(B, ns),
        in_specs=[cur(nm), prev, cur(o_nsa.shape[1]), cur(D), _full(cw.shape), _full(cb.shape), _full(lg.shape),
                  _full(lb.shape), _full(pw.shape), _full(pb.shape), _full(plw.shape), _full(pls.shape),
                  _full(wo.shape)],
        out_specs=cur(D),
        out_shape=jax.ShapeDtypeStruct((T, D), F32),
        scratch_shapes=[pltpu.VMEM((ts + HALO, dc), F32), pltpu.VMEM((ts + HALO, dc), F32)],
        compiler_params=_params("parallel", "parallel"),
    )(misc, misc, o_nsa, x2, cw, cb, lg, lb, pw, pb, plw, pls, wo)


def _norm_matmul_kernel(x_ref, g_ref, w_ref, o_ref):
    o_ref[...] = _dot(_rms(x_ref[...], g_ref[...]).astype(BF16), w_ref[...]).astype(o_ref.dtype)


def _norm_matmul(x2, g, w, tm, out_dtype):
    T, D = x2.shape
    n = w.shape[1]
    return pl.pallas_call(
        _norm_matmul_kernel,
        grid=(T // tm,),
        in_specs=[pl.BlockSpec((tm, D), lambda i: (i, 0)), _full((1, D)), _full((D, n))],
        out_specs=pl.BlockSpec((tm, n), lambda i: (i, 0)),
        out_shape=jax.ShapeDtypeStruct((T, n), out_dtype),
        compiler_params=_params("parallel"),
    )(x2, g, w)


def _xattn_kernel(x_ref, g_ref, wq_ref, k_ref, v_ref, wo_ref, o_ref):
    x = x_ref[...]
    D = x.shape[1]
    dh = D // XA_HEADS
    q = _dot(_rms(x, g_ref[...]).astype(BF16), wq_ref[...]) * (dh ** -0.5)
    outs = []
    for h in range(XA_HEADS):
        c = slice(h * dh, (h + 1) * dh)
        s = _dot_nt(q[:, c].astype(BF16), k_ref[0, :, c])
        e = jnp.exp(s - jnp.max(s, axis=-1, keepdims=True))
        p = e / jnp.sum(e, axis=-1, keepdims=True)
        outs.append(_dot(p.astype(BF16), v_ref[0, :, c]))
    o = jnp.concatenate(outs, axis=1)
    o_ref[...] = x + _dot(o.astype(BF16), wo_ref[...])


def _xattn(x2, g, wq, memkv, wo, B, S, tm):
    T, D = x2.shape
    ns = S // tm
    M = memkv.shape[1]
    return pl.pallas_call(
        _xattn_kernel,
        grid=(B, ns),
        in_specs=[pl.BlockSpec((tm, D), lambda b, i: (b * ns + i, 0)), _full((1, D)), _full((D, D)),
                  pl.BlockSpec((1, M, D), lambda b, i: (b, 0, 0)),
                  pl.BlockSpec((1, M, D), lambda b, i: (b, 0, 1)),
                  _full((D, D))],
        out_specs=pl.BlockSpec((tm, D), lambda b, i: (b * ns + i, 0)),
        out_shape=jax.ShapeDtypeStruct((T, D), F32),
        compiler_params=_params("parallel", "parallel"),
    )(x2, g, wq, memkv, memkv, wo)


PAIR_LIST = tuple((a, b) for a in range(PEER_TOPK) for b in range(PEER_TOPK) if (a + 1) * (b + 1) <= PEER_TOPK)
N_PAIR_ROWS = -(-len(PAIR_LIST) // 8) * 8
PAIR_COUNT = tuple(PEER_TOPK // (a + 1) for a in range(PEER_TOPK))
PAIR_START = tuple(sum(PAIR_COUNT[:a]) for a in range(PEER_TOPK))


def _top_rows_exact(v, n):
    R = v.shape[0]
    ridx = lax.broadcasted_iota(jnp.int32, v.shape, 0).astype(F32)
    rank = jnp.full(v.shape, float(n), F32)
    vals = []
    for r in range(n):
        m = jnp.max(v, axis=0, keepdims=True)
        first = jnp.min(jnp.where(v == m, ridx, float(R)), axis=0, keepdims=True)
        taken = ridx == first
        v = jnp.where(taken, LOWEST, v)
        rank = jnp.where(taken, float(r), rank)
        vals.append(m)
    return vals, rank


def _top_rows_distinct(v, n):
    rank = jnp.full(v.shape, float(n), F32)
    vals = []
    for r in range(n):
        m = jnp.max(v, axis=0, keepdims=True)
        taken = v == m
        v = jnp.where(taken, LOWEST, v)
        rank = jnp.where(taken, float(r), rank)
        vals.append(m)
    return vals, rank


def _peer_route_kernel(x_ref, g_ref, wqt_ref, sk_ref, hn_ref, c1_ref, g1_ref, r2_ref, g2_ref, s_sc, top_sc,
                       rank_sc, cand_sc, cw_sc):
    hn = _rms(x_ref[...], g_ref[...]).astype(BF16)
    hn_ref[...] = hn
    qt = _dot_nt(wqt_ref[...], hn)
    nk, n = PEER_KEYS, PEER_TOPK
    n_lists = 2 * PEER_HEADS

    def put(li, vals, rank):
        for r in range(n):
            top_sc[li * n + r:li * n + r + 1, :] = vals[r]
        rank_sc[li * nk:(li + 1) * nk, :] = rank

    repeated = []
    for li in range(n_lists):
        s = _dot(sk_ref[li % 2], qt[li * nk:(li + 1) * nk].astype(BF16))
        s_sc[li * nk:(li + 1) * nk, :] = s
        vals, rank = _top_rows_distinct(s, n)
        put(li, vals, rank)
        ranked = jnp.sum((rank < float(n)).astype(F32), axis=0, keepdims=True)
        repeated.append(jnp.max(ranked) > float(n))
    for li in range(n_lists):
        @pl.when(repeated[li])
        def _(li=li):
            put(li, *_top_rows_exact(s_sc[li * nk:(li + 1) * nk, :], n))

    cand_sc[...] = jnp.full(cand_sc.shape, LOWEST, F32)
    cw_sc[...] = jnp.zeros(cw_sc.shape, F32)
    for h in range(PEER_HEADS):
        l1, l2 = 2 * h, 2 * h + 1
        top1 = [top_sc[l1 * n + a:l1 * n + a + 1, :] for a in range(n)]
        top2 = [top_sc[l2 * n + a:l2 * n + a + 1, :] for a in range(n)]
        e1 = [jnp.exp(t - top1[0]) for t in top1]
        e2 = [jnp.exp(t - top2[0]) for t in top2]
        for r, (a, b) in enumerate(PAIR_LIST):
            cand_sc[h, r:r + 1, :] = top1[a] + top2[b]
            cw_sc[h, r:r + 1, :] = e1[a] * e2[b]
        _, crank = _top_rows_exact(cand_sc[h], n)
        chosen = (crank < float(n)).astype(F32)
        z = jnp.sum(chosen * cw_sc[h], axis=0, keepdims=True)
        rank1 = rank_sc[l1 * nk:(l1 + 1) * nk, :]
        count1 = jnp.zeros(rank1.shape, F32)
        for a in range(n):
            n_a = jnp.sum(chosen[PAIR_START[a]:PAIR_START[a] + PAIR_COUNT[a]], axis=0, keepdims=True)
            count1 = jnp.where(rank1 == float(a), n_a, count1)
        rows = slice(h * nk, (h + 1) * nk)
        c1_ref[rows, :] = count1
        g1_ref[rows, :] = jnp.exp(s_sc[l1 * nk:(l1 + 1) * nk, :] - top1[0]) * (1.0 / z)
        r2_ref[rows, :] = rank_sc[l2 * nk:(l2 + 1) * nk, :].astype(BF16)
        g2_ref[rows, :] = jnp.exp(s_sc[l2 * nk:(l2 + 1) * nk, :] - top2[0]).astype(BF16)


def _peer_route(x2, g, wqt, sk, tm):
    T, D = x2.shape
    nr = PEER_HEADS * PEER_KEYS
    col = pl.BlockSpec((nr, tm), lambda i: (0, i))
    return pl.pallas_call(
        _peer_route_kernel,
        grid=(T // tm,),
        in_specs=[pl.BlockSpec((tm, D), lambda i: (i, 0)), _full((1, D)), _full(wqt.shape), _full(sk.shape)],
        out_specs=[pl.BlockSpec((tm, D), lambda i: (i, 0)), col, col, col, col],
        out_shape=[jax.ShapeDtypeStruct((T, D), BF16), jax.ShapeDtypeStruct((nr, T), F32),
                   jax.ShapeDtypeStruct((nr, T), F32), jax.ShapeDtypeStruct((nr, T), BF16),
                   jax.ShapeDtypeStruct((nr, T), BF16)],
        scratch_shapes=[pltpu.VMEM((2 * nr, tm), F32), pltpu.VMEM((2 * PEER_HEADS * PEER_TOPK, tm), F32),
                        pltpu.VMEM((2 * nr, tm), F32), pltpu.VMEM((PEER_HEADS, N_PAIR_ROWS, tm), F32),
                        pltpu.VMEM((PEER_HEADS, N_PAIR_ROWS, tm), F32)],
        compiler_params=_params("parallel"),
    )(x2, g, wqt, sk)


def _peer_expert_kernel(hn_ref, c1_ref, g1_ref, r2_ref, g2_ref, u_ref, vt_ref, x_ref, o_ref, acc_sc, wa_sc, *,
                        te, sub):
    c = pl.program_id(1)
    nk = PEER_KEYS

    @pl.when(c == 0)
    def _():
        acc_sc[...] = jnp.zeros(acc_sc.shape, F32)

    hn = hn_ref[...]
    tm = hn.shape[0]

    def pre_act(j):
        return _dot_nt(u_ref[j * sub:(j + 1) * sub, :], hn)

    nxt = pre_act(0)
    for j in range(te // sub):
        act = nxt.astype(BF16)
        if j + 1 < te // sub:
            nxt = pre_act(j + 1)
        for il in range(sub // nk):
            i = c * (te // nk) + j * (sub // nk) + il
            w = jnp.zeros((nk, tm), BF16)
            for h in range(PEER_HEADS):
                count = c1_ref[pl.ds(h * nk + i, 1), :].astype(BF16)
                g1row = g1_ref[pl.ds(h * nk + i, 1), :].astype(BF16)
                rows = slice(h * nk, (h + 1) * nk)
                w = w + jnp.where(r2_ref[rows, :] < count, g2_ref[rows, :], 0.0) * g1row
            wa_sc[j, il * nk:(il + 1) * nk, :] = w * _gelu(act[il * nk:(il + 1) * nk])
        acc_sc[...] += _dot(vt_ref[:, j * sub:(j + 1) * sub], wa_sc[j])

    @pl.when(c == pl.num_programs(1) - 1)
    def _():
        o_ref[...] = x_ref[...] + acc_sc[...].T


def _peer_expert(hn, c1, g1, r2, g2, u, vt, x2, tm, te, sub):
    T, D = x2.shape
    ne = u.shape[0]
    tok = pl.BlockSpec((c1.shape[0], tm), lambda i, c: (0, i))
    row = pl.BlockSpec((tm, D), lambda i, c: (i, 0))
    return pl.pallas_call(
        functools.partial(_peer_expert_kernel, te=te, sub=sub),
        grid=(T // tm, ne // te),
        in_specs=[row, tok, tok, tok, tok,
                  pl.BlockSpec((te, D), lambda i, c: (c, 0)),
                  pl.BlockSpec((D, te), lambda i, c: (0, c)),
                  row],
        out_specs=row,
        out_shape=jax.ShapeDtypeStruct((T, D), F32),
        scratch_shapes=[pltpu.VMEM((D, tm), F32), pltpu.VMEM((te // sub, sub, tm), BF16)],
        compiler_params=_params("parallel", "arbitrary"),
    )(hn, c1, g1, r2, g2, u, vt, x2)


def _final_norm_kernel(x_ref, g_ref, o_ref):
    o_ref[...] = _rms(x_ref[...], g_ref[...])


def _final_norm(x2, g, tm):
    T, D = x2.shape
    return pl.pallas_call(
        _final_norm_kernel,
        grid=(T // tm,),
        in_specs=[pl.BlockSpec((tm, D), lambda i: (i, 0)), _full((1, D))],
        out_specs=pl.BlockSpec((tm, D), lambda i: (i, 0)),
        out_shape=jax.ShapeDtypeStruct((T, D), F32),
        compiler_params=_params("parallel"),
    )(x2, g)


def _rope_tables(positions):
    B, S = positions.shape
    freqs = ROPE_THETA ** (-jnp.arange(ROT_HALF, dtype=F32) * 2.0 / ROT_DIM)
    ang = positions.astype(F32)[:, :, None] * freqs
    cos, sin = jnp.cos(ang), jnp.sin(ang)
    ones = jnp.ones((B, S, HEAD_DIM - ROT_DIM), F32)
    zeros8 = jnp.zeros((B, S, ROT_HALF), F32)
    zeros = jnp.zeros((B, S, HEAD_DIM - ROT_DIM), F32)
    cos_h = jnp.concatenate([cos, cos, ones], axis=-1)
    sa_h = jnp.concatenate([zeros8, sin, zeros], axis=-1)
    sb_h = jnp.concatenate([-sin, zeros8, zeros], axis=-1)
    rep = LANES // HEAD_DIM
    tile = lambda t: jnp.tile(t, (1, 1, rep)).reshape(B * S, LANES)
    return tile(cos_h), tile(sa_h), tile(sb_h)


def _overlap_matrix(n_chunk, n_sel):
    ci = np.arange(n_chunk)[:, None] * CMP_STRIDE
    sj = np.arange(LANES)[None, :] * SEL_BLOCK
    ov = (ci < sj + SEL_BLOCK) & (ci + CMP_BLOCK > sj) & (np.arange(LANES)[None, :] < n_sel)
    ov = ov & (np.arange(n_chunk)[:, None] < n_chunk - 1)
    return jnp.asarray(ov, BF16)


def kernel(x, mem, positions, norm_mix_g, w_in, cmp_pe, cmp_w1, cmp_w2, conv_w, conv_b, conv_ln_g, conv_ln_b, conv_pw_w, conv_pw_b, pool_w, pool_scale, w_out, norm_xa_g, norm_mem_g, xa_wq, xa_wkv, xa_wo, norm_ffn_g, peer_wq, peer_subkeys, peer_u, peer_v, final_g):
    B, S, D = x.shape
    T = B * S
    depth = w_in.shape[0]
    M = mem.shape[1]
    dc = conv_w.shape[-1]
    n_chunk = S // CMP_STRIDE
    n_sel = S // SEL_BLOCK
    assert S % 512 == 0 and n_sel <= LANES and n_sel >= SEL_TOPN and D % LANES == 0

    cos_t, sa_t, sb_t = _rope_tables(positions)
    overlap = _overlap_matrix(n_chunk, n_sel)
    row = lambda v: v.reshape(1, -1)
    x2 = x.reshape(T, D)
    mem2 = mem.reshape(B * M, D)
    n_gate = 3 * NSA_HEADS

    for l in range(depth):
        w = w_in[l]
        w_pad = jnp.concatenate([w[:, :D_QKV + n_gate], jnp.zeros((D, LANES - n_gate), F32), w[:, D_QKV + n_gate:]],
                                axis=1).astype(BF16)
        q, kv, misc = _inproj(x2, row(norm_mix_g[l]), w_pad, cos_t, sa_t, sb_t, tm=512)
        xc = kv[:, :2 * LANES].reshape(B, S, 2, NSA_KV_HEADS, HEAD_DIM)
        xc = jnp.transpose(xc, (2, 0, 3, 1, 4)).reshape(2, B * NSA_KV_HEADS, n_chunk, CMP_STRIDE * HEAD_DIM)
        pe = cmp_pe[l].reshape(2, 2, CMP_STRIDE * HEAD_DIM)
        cmp = _compress(xc, pe, cmp_w1[l].astype(BF16), cmp_w2[l].astype(BF16))
        cmp = cmp.reshape(2, B, NSA_KV_HEADS, n_chunk, HEAD_DIM)
        cmp = jnp.transpose(cmp, (0, 1, 3, 2, 4)).reshape(2, B, n_chunk, LANES).astype(BF16)
        o_nsa = _nsa(q, cmp[0], cmp[1], kv.reshape(B, S, -1), misc, overlap, B, S, tq=128)
        pg = dc // len(POOL_WINDOWS)
        plw = jnp.zeros((dc, dc), F32)
        for gi in range(len(POOL_WINDOWS)):
            plw = plw.at[gi * pg:(gi + 1) * pg, gi * pg:(gi + 1) * pg].set(pool_w[l, gi])
        x2 = _mixout(misc, o_nsa, x2, conv_w[l], row(conv_b[l]), row(conv_ln_g[l]), row(conv_ln_b[l]),
                     conv_pw_w[l].astype(BF16), row(conv_pw_b[l]), plw.astype(BF16), row(pool_scale[l]),
                     w_out[l].astype(BF16), B, S, ts=512)
        memkv = _norm_matmul(mem2, row(norm_mem_g[l]), xa_wkv[l].astype(BF16), tm=256, out_dtype=BF16)
        x2 = _xattn(x2, row(norm_xa_g[l]), xa_wq[l].astype(BF16), memkv.reshape(B, M, 2 * D),
                    xa_wo[l].astype(BF16), B, S, tm=512)
        hn, c1, g1, r2, g2 = _peer_route(x2, row(norm_ffn_g[l]), peer_wq[l].T.astype(BF16),
                                         peer_subkeys[l].astype(BF16), tm=512)
        x2 = _peer_expert(hn, c1, g1, r2, g2, peer_u[l].astype(BF16), peer_v[l].T.astype(BF16), x2,
                          tm=512, te=2048, sub=256)
    return _final_norm(x2, row(final_g), tm=512).reshape(B, S, D)
```

```python
import functools
import math

import jax
import jax.numpy as jnp
import numpy as np
from jax import lax
from jax.experimental import pallas as pl
from jax.experimental.pallas import tpu as pltpu

F32 = jnp.float32
BF16 = jnp.bfloat16

NSA_HEADS = 8
NSA_KV_HEADS = 2
NSA_GROUP = NSA_HEADS // NSA_KV_HEADS
HEAD_DIM = 64
D_NSA = NSA_HEADS * HEAD_DIM
D_KV = NSA_KV_HEADS * HEAD_DIM
ROT_DIM = HEAD_DIM // 4
ROT_HALF = ROT_DIM // 2
ROPE_THETA = 500000.0
CMP_BLOCK = 32
CMP_STRIDE = 16
SEL_BLOCK = 64
SEL_TOPN = 16
WINDOW = 512
CONV_WIDTH = 31
POOL_WINDOWS = (2, 4, 8, 16)
XA_HEADS = 4
PEER_HEADS = 8
PEER_KEYS = 128
PEER_TOPK = 16
EPS = 1e-6
NEG = -1e30
FORCE = 1e4
LOWEST = -3.0e38

LANES = 128
VMEM_LIMIT = 56 * 1024 * 1024

NT_DIMS = (((1,), (1,)), ((), ()))


def _params(*sem):
    return pltpu.CompilerParams(dimension_semantics=sem, vmem_limit_bytes=VMEM_LIMIT)


def _full(shape):
    nd = len(shape)
    return pl.BlockSpec(shape, lambda *_: (0,) * nd)


def _rms(x, g):
    return x * lax.rsqrt(jnp.mean(x * x, axis=-1, keepdims=True) + EPS) * g


def _gelu(x):
    c = math.sqrt(2.0 / math.pi)
    return 0.5 * x * (1.0 + jnp.tanh(c * (x + 0.044715 * (x * x * x))))


def _dot(a, b):
    return jnp.dot(a, b, preferred_element_type=F32)


def _dot_nt(a, b):
    return lax.dot_general(a, b, NT_DIMS, preferred_element_type=F32)


N_ROPE_Q = D_NSA // LANES
N_KV_CHUNKS = 6
D_QKV = D_NSA + N_KV_CHUNKS * D_KV
N_KV_OUT = 4 + 2 * NSA_KV_HEADS


def _inproj_kernel(x_ref, g_ref, w_ref, cos_ref, sa_ref, sb_ref, q_ref, kv_ref, misc_ref):
    y = _rms(x_ref[...], g_ref[...])
    z = _dot(y.astype(BF16), w_ref[...])
    cos, sa, sb = cos_ref[...], sa_ref[...], sb_ref[...]

    def rope(c):
        return c * cos + pltpu.roll(c, ROT_HALF, 1) * sa + pltpu.roll(c, LANES - ROT_HALF, 1) * sb

    scale = HEAD_DIM ** -0.5
    for j in range(N_ROPE_Q):
        q_ref[:, j * LANES:(j + 1) * LANES] = (rope(z[:, j * LANES:(j + 1) * LANES]) * scale).astype(BF16)
    lane = lax.broadcasted_iota(jnp.int32, (z.shape[0], LANES), 1)
    out = 0
    for j in range(N_KV_CHUNKS):
        c = z[:, D_NSA + j * LANES:D_NSA + (j + 1) * LANES]
        if j % 2 == 0:
            c = rope(c)
        if j < 2 or j % 2 == 0:
            kv_ref[:, out * LANES:(out + 1) * LANES] = c.astype(BF16)
            out += 1
    for j in (3, 5):
        c = z[:, D_NSA + j * LANES:D_NSA + (j + 1) * LANES]
        for k in range(NSA_KV_HEADS):
            own = (lane >= HEAD_DIM) if k == 1 else (lane < HEAD_DIM)
            kv_ref[:, out * LANES:(out + 1) * LANES] = jnp.where(own, c, 1.0).astype(BF16)
            out += 1
    misc_ref[...] = z[:, D_QKV:]


def _inproj(x2, g, w_pad, cos_t, sa_t, sb_t, tm):
    T, D = x2.shape
    n = w_pad.shape[1]
    n_misc = n - D_QKV
    row = lambda w: pl.BlockSpec((tm, w), lambda i: (i, 0))
    return pl.pallas_call(
        _inproj_kernel,
        grid=(T // tm,),
        in_specs=[row(D), _full((1, D)), _full((D, n)), row(LANES), row(LANES), row(LANES)],
        out_specs=[row(D_NSA), row(N_KV_OUT * LANES), row(n_misc)],
        out_shape=[jax.ShapeDtypeStruct((T, D_NSA), BF16),
                   jax.ShapeDtypeStruct((T, N_KV_OUT * LANES), BF16),
                   jax.ShapeDtypeStruct((T, n_misc), F32)],
        compiler_params=_params("parallel"),
    )(x2, g, w_pad, cos_t, sa_t, sb_t)


def _compress_kernel(x_ref, pe_ref, w1_ref, w2_ref, o_ref):
    x = x_ref[0, 0].astype(F32)
    pe = pe_ref[0]
    half = x.shape[1]
    a = _dot((x + pe[0:1]).astype(BF16), w1_ref[0, :half])
    b = _dot((x + pe[1:2]).astype(BF16), w1_ref[0, half:])
    n = x.shape[0]
    hid = _gelu(a + pltpu.roll(b, n - 1, 0))
    o_ref[0, 0] = _dot(hid.astype(BF16), w2_ref[0])


def _compress(xc, pe, w1, w2):
    two, nb, nch, width = xc.shape
    hid = w1.shape[-1]
    return pl.pallas_call(
        _compress_kernel,
        grid=(two, nb),
        in_specs=[pl.BlockSpec((1, 1, nch, width), lambda w, i: (w, i, 0, 0)),
                  pl.BlockSpec((1, 2, width), lambda w, i: (w, 0, 0)),
                  pl.BlockSpec((1, 2 * width, hid), lambda w, i: (w, 0, 0)),
                  pl.BlockSpec((1, hid, HEAD_DIM), lambda w, i: (w, 0, 0))],
        out_specs=pl.BlockSpec((1, 1, nch, HEAD_DIM), lambda w, i: (w, i, 0, 0)),
        out_shape=jax.ShapeDtypeStruct((two, nb, nch, HEAD_DIM), F32),
        compiler_params=_params("parallel", "parallel"),
    )(xc, pe, w1, w2)


def _nsa_kernel(q_ref, kcmp_ref, vcmp_ref, ks_ref, kw_ref, vs0_ref, vs1_ref, vw0_ref, vw1_ref, gate_ref, ov_ref,
                o_ref, s_sc, mrun_sc, acc_sc, *, tq, kc_sel, kc_win):
    t0 = pl.program_id(1) * tq
    rows = NSA_GROUP * tq
    lane = lax.broadcasted_iota(jnp.int32, (tq, LANES), 1)
    lane_r = lax.broadcasted_iota(jnp.int32, (rows, LANES), 1)
    tpos = t0 + lax.broadcasted_iota(jnp.int32, (tq, 1), 0)
    gates = jax.nn.sigmoid(gate_ref[...])
    n_cmp = kcmp_ref.shape[1]
    n_sel = ks_ref.shape[1] // SEL_BLOCK
    vs_refs, vw_refs = (vs0_ref, vs1_ref), (vw0_ref, vw1_ref)

    def normalise(acc, own):
        return acc / jnp.where(own, pltpu.roll(acc, HEAD_DIM, 1), 1.0)

    def own_lanes(lanes, k):
        return (lanes >= HEAD_DIM) if k == 1 else (lanes < HEAD_DIM)

    def queries(k):
        parts = []
        for g in range(NSA_GROUP):
            hh = k * NSA_GROUP + g
            c = q_ref[:, (hh // 2) * LANES:(hh // 2 + 1) * LANES].astype(F32)
            if hh % 2 != k:
                c = pltpu.roll(c, HEAD_DIM, 1)
            parts.append(jnp.where(own_lanes(lane, k), c, 0.0))
        return jnp.concatenate(parts, axis=0).astype(BF16)

    def compressed(qs):
        s = _dot_nt(qs, kcmp_ref[0]).reshape(NSA_GROUP, tq, n_cmp)
        cmp_end = lax.broadcasted_iota(jnp.int32, (tq, n_cmp), 1) * CMP_STRIDE + (CMP_BLOCK - 1)
        valid = cmp_end <= tpos
        s = s + jnp.where(valid, 0.0, NEG)[None]
        e = jnp.exp(s - jnp.max(s, axis=-1, keepdims=True)) * valid.astype(F32)[None]
        l = jnp.sum(e, axis=-1, keepdims=True)
        p3 = e / jnp.where(l > 0.0, l, 1.0)
        o_cmp = _dot(p3.reshape(rows, n_cmp).astype(BF16), vcmp_ref[0])

        psum = p3[0] + p3[1] + p3[2] + p3[3]
        ov = ov_ref[...]
        p_hi = psum.astype(BF16)
        r1 = psum - p_hi.astype(F32)
        p_mid = r1.astype(BF16)
        p_lo = (r1 - p_mid.astype(F32)).astype(BF16)
        imp = _dot(p_hi, ov) + _dot(p_mid, ov) + _dot(p_lo, ov)
        imp = imp.T[:n_sel]
        blk = lax.broadcasted_iota(jnp.int32, (n_sel, tq), 0)
        tpos_t = t0 + lax.broadcasted_iota(jnp.int32, (1, tq), 1)
        cur = tpos_t // SEL_BLOCK
        forced = (blk == 0) | (blk == cur) | (blk == cur - 1)
        imp = jnp.where(blk * SEL_BLOCK <= tpos_t, jnp.where(forced, FORCE, imp), NEG)
        rank = jnp.zeros((n_sel, tq), jnp.int32)
        for jp in range(n_sel):
            one = imp[jp:jp + 1, :]
            beats = (one > imp) | ((one == imp) & (blk > jp))
            rank = rank + beats.astype(jnp.int32)
        sel_t = jnp.concatenate([(rank < SEL_TOPN).astype(F32), jnp.zeros((LANES - n_sel, tq), F32)], axis=0)
        return o_cmp, sel_t.T.astype(BF16)

    def window(k, qs):
        w0 = pl.multiple_of(jnp.maximum(t0 + tq - kc_win, 0), tq)
        sw = _dot_nt(qs, kw_ref[0, pl.ds(w0, kc_win), :]).reshape(NSA_GROUP, tq, kc_win)
        wpos = w0 + lax.broadcasted_iota(jnp.int32, (tq, kc_win), 1)
        in_win = (wpos <= tpos) & (wpos > tpos - WINDOW)
        sw = (sw + jnp.where(in_win, 0.0, NEG)[None]).reshape(rows, kc_win)
        pw = jnp.exp(sw - jnp.max(sw, axis=1, keepdims=True))
        return normalise(_dot(pw.astype(BF16), vw_refs[k][0, pl.ds(w0, kc_win), :]), own_lanes(lane_r, k))

    heads = range(NSA_KV_HEADS)
    qs = [queries(k) for k in heads]
    o_cmp, sel = zip(*[compressed(qs[k]) for k in heads])
    o_win = [window(k, qs[k]) for k in heads]

    n_chunks = (t0 + tq + kc_sel - 1) // kc_sel
    mrun_sc[...] = jnp.full(mrun_sc.shape, NEG, F32)

    def scores(c, carry):
        k0 = pl.multiple_of(c * kc_sel, kc_sel)
        jrow = lax.broadcasted_iota(jnp.int32, (LANES, kc_sel), 0)
        kcol = lax.broadcasted_iota(jnp.int32, (LANES, kc_sel), 1)
        expand = (jrow == k0 // SEL_BLOCK + kcol // SEL_BLOCK).astype(BF16)
        causal = k0 + lax.broadcasted_iota(jnp.int32, (tq, kc_sel), 1) <= tpos
        kb = ks_ref[0, pl.ds(k0, kc_sel), :]
        for k in heads:
            seen = (_dot(sel[k], expand) > 0.5) & causal
            s = _dot_nt(qs[k], kb).reshape(NSA_GROUP, tq, kc_sel)
            s = (s + jnp.where(seen, 0.0, NEG)[None]).reshape(rows, kc_sel)
            s_sc[k, c] = s
            m = mrun_sc[k]
            for j in range(kc_sel // LANES):
                m = jnp.maximum(m, s[:, j * LANES:(j + 1) * LANES])
            mrun_sc[k] = m
        return carry

    lax.fori_loop(0, n_chunks, scores, 0)
    m_sel = [jnp.broadcast_to(jnp.max(mrun_sc[k], axis=1, keepdims=True), (rows, LANES)) for k in heads]
    acc_sc[...] = jnp.zeros(acc_sc.shape, F32)

    def weigh(c, carry):
        k0 = pl.multiple_of(c * kc_sel, kc_sel)
        for k in heads:
            s = s_sc[k, c]
            p = jnp.concatenate([jnp.exp(s[:, j * LANES:(j + 1) * LANES] - m_sel[k])
                                 for j in range(kc_sel // LANES)], axis=1)
            acc_sc[k] += _dot(p.astype(BF16), vs_refs[k][0, pl.ds(k0, kc_sel), :])
        return carry

    lax.fori_loop(0, n_chunks, weigh, 0)

    for k in heads:
        o_sel = normalise(acc_sc[k], own_lanes(lane_r, k))
        outs = []
        for g in range(NSA_GROUP):
            r = slice(g * tq, (g + 1) * tq)
            gi = (k * NSA_GROUP + g) * 3
            og = (gates[:, gi:gi + 1] * o_cmp[k][r] + gates[:, gi + 1:gi + 2] * o_sel[r]
                  + gates[:, gi + 2:gi + 3] * o_win[k][r])
            if g % 2 != k:
                og = pltpu.roll(og, HEAD_DIM, 1)
            outs.append(og)
        for j in range(NSA_GROUP // 2):
            chunk = jnp.where(lane < HEAD_DIM, outs[2 * j], outs[2 * j + 1])
            cj = k * (NSA_GROUP // 2) + j
            o_ref[:, cj * LANES:(cj + 1) * LANES] = chunk.astype(BF16)


def _nsa(q, kcmp, vcmp, kv, gates_misc, overlap, B, S, tq):
    kc_sel, kc_win = 512, WINDOW + tq
    nq = S // tq
    n_cmp = kcmp.shape[1]
    rows = NSA_GROUP * tq
    kvspec = lambda j: pl.BlockSpec((1, S, LANES), lambda b, i: (b, 0, j))
    return pl.pallas_call(
        functools.partial(_nsa_kernel, tq=tq, kc_sel=kc_sel, kc_win=kc_win),
        grid=(B, nq),
        in_specs=[pl.BlockSpec((tq, D_NSA), lambda b, i: (b * nq + i, 0)),
                  pl.BlockSpec((1, n_cmp, LANES), lambda b, i: (b, 0, 0)),
                  pl.BlockSpec((1, n_cmp, LANES), lambda b, i: (b, 0, 0)),
                  kvspec(2), kvspec(3), kvspec(4), kvspec(5), kvspec(6), kvspec(7),
                  pl.BlockSpec((tq, LANES), lambda b, i: (b * nq + i, 0)),
                  _full(overlap.shape)],
        out_specs=pl.BlockSpec((tq, D_NSA), lambda b, i: (b * nq + i, 0)),
        out_shape=jax.ShapeDtypeStruct((B * S, D_NSA), BF16),
        scratch_shapes=[pltpu.VMEM((NSA_KV_HEADS, S // kc_sel, rows, kc_sel), F32),
                        pltpu.VMEM((NSA_KV_HEADS, rows, LANES), F32), pltpu.VMEM((NSA_KV_HEADS, rows, LANES), F32)],
        compiler_params=_params("parallel", "parallel"),
    )(q, kcmp, vcmp, kv, kv, kv, kv, kv, kv, gates_misc, overlap)


HALO = 32


def _mixout_kernel(mc_ref, mp_ref, on_ref, x_ref, cw_ref, cb_ref, lg_ref, lb_ref, pw_ref, pb_ref, plw_ref,
                   pls_ref, wo_ref, o_ref, hbuf, pbuf, *, ts, dc):
    i = pl.program_id(1)
    first = i == 0
    off_a, off_b, off_p = LANES, LANES + dc, LANES + 2 * dc

    def glu(ref, r):
        return ref[r, off_a:off_a + dc] * jax.nn.sigmoid(ref[r, off_b:off_b + dc])

    tail = slice(ts - HALO, ts)
    hbuf[0:HALO, :] = jnp.where(first, 0.0, glu(mp_ref, tail))
    hbuf[HALO:, :] = glu(mc_ref, slice(None))
    pbuf[0:HALO, :] = jnp.where(first, 0.0, mp_ref[tail, off_p:off_p + dc])
    pcur = mc_ref[:, off_p:off_p + dc]
    pbuf[HALO:, :] = pcur

    acc = jnp.zeros((ts, dc), F32) + cb_ref[...]
    for w in range(CONV_WIDTH):
        acc = acc + hbuf[pl.ds(HALO - (CONV_WIDTH - 1) + w, ts), :] * cw_ref[w:w + 1, :]
    mu = jnp.mean(acc, axis=-1, keepdims=True)
    xc = acc - mu
    y = xc * lax.rsqrt(jnp.mean(xc * xc, axis=-1, keepdims=True) + EPS) * lg_ref[...] + lb_ref[...]
    y = y * jax.nn.sigmoid(y)
    o_conv = _dot(y.astype(BF16), pw_ref[...]) + pb_ref[...]

    tglob = i * ts + lax.broadcasted_iota(jnp.int32, (ts, dc), 0)
    lane = lax.broadcasted_iota(jnp.int32, (ts, dc), 1)
    pg = dc // len(POOL_WINDOWS)
    run = pcur
    d = 1
    mean = jnp.zeros((ts, dc), F32)
    for gi, w in enumerate(POOL_WINDOWS):
        while d < w:
            run = run + pbuf[pl.ds(HALO - d, ts), :]
            d += 1
        cnt = jnp.minimum(tglob + 1, w).astype(F32)
        mean = jnp.where(lane // pg == gi, run / cnt, mean)
    o_pool = _dot((mean - pcur).astype(BF16), plw_ref[...]) * pls_ref[...]

    dn = on_ref.shape[1]
    o_ref[...] = (x_ref[...] + _dot(on_ref[...], wo_ref[0:dn, :])
                  + _dot(o_conv.astype(BF16), wo_ref[dn:dn + dc, :])
                  + _dot(o_pool.astype(BF16), wo_ref[dn + dc:, :]))


def _mixout(misc, o_nsa, x2, cw, cb, lg, lb, pw, pb, plw, pls, wo, B, S, ts):
    T, D = x2.shape
    dc = cw.shape[1]
    ns = S // ts
    nm = misc.shape[1]
    cur = lambda w: pl.BlockSpec((ts, w), lambda b, i: (b * ns + i, 0))
    prev = pl.BlockSpec((ts, nm), lambda b, i: (b * ns + jnp.maximum(i - 1, 0), 0))
    return pl.pallas_call(
        functools.partial(_mixout_kernel, ts=ts, dc=dc),
        grid=(B, ns),
        in_specs=[cur(nm), prev, cur(o_nsa.shape[1]), cur(D), _full(cw.shape), _full(cb.shape), _full(lg.shape),
                  _full(lb.shape), _full(pw.shape), _full(pb.shape), _full(plw.shape), _full(pls.shape),
                  _full(wo.shape)],
        out_specs=cur(D),
        out_shape=jax.ShapeDtypeStruct((T, D), F32),
        scratch_shapes=[pltpu.VMEM((ts + HALO, dc), F32), pltpu.VMEM((ts + HALO, dc), F32)],
        compiler_params=_params("parallel", "parallel"),
    )(misc, misc, o_nsa, x2, cw, cb, lg, lb, pw, pb, plw, pls, wo)


def _norm_matmul_kernel(x_ref, g_ref, w_ref, o_ref):
    o_ref[...] = _dot(_rms(x_ref[...], g_ref[...]).astype(BF16), w_ref[...]).astype(o_ref.dtype)


def _norm_matmul(x2, g, w, tm, out_dtype):
    T, D = x2.shape
    n = w.shape[1]
    return pl.pallas_call(
        _norm_matmul_kernel,
        grid=(T // tm,),
        in_specs=[pl.BlockSpec((tm, D), lambda i: (i, 0)), _full((1, D)), _full((D, n))],
        out_specs=pl.BlockSpec((tm, n), lambda i: (i, 0)),
        out_shape=jax.ShapeDtypeStruct((T, n), out_dtype),
        compiler_params=_params("parallel"),
    )(x2, g, w)


def _xattn_kernel(x_ref, g_ref, wq_ref, k_ref, v_ref, wo_ref, o_ref):
    x = x_ref[...]
    D = x.shape[1]
    dh = D // XA_HEADS
    q = _dot(_rms(x, g_ref[...]).astype(BF16), wq_ref[...]) * (dh ** -0.5)
    outs = []
    for h in range(XA_HEADS):
        c = slice(h * dh, (h + 1) * dh)
        s = _dot_nt(q[:, c].astype(BF16), k_ref[0, :, c])
        e = jnp.exp(s - jnp.max(s, axis=-1, keepdims=True))
        p = e / jnp.sum(e, axis=-1, keepdims=True)
        outs.append(_dot(p.astype(BF16), v_ref[0, :, c]))
    o = jnp.concatenate(outs, axis=1)
    o_ref[...] = x + _dot(o.astype(BF16), wo_ref[...])


def _xattn(x2, g, wq, memkv, wo, B, S, tm):
    T, D = x2.shape
    ns = S // tm
    M = memkv.shape[1]
    return pl.pallas_call(
        _xattn_kernel,
        grid=(B, ns),
        in_specs=[pl.BlockSpec((tm, D), lambda b, i: (b * ns + i, 0)), _full((1, D)), _full((D, D)),
                  pl.BlockSpec((1, M, D), lambda b, i: (b, 0, 0)),
                  pl.BlockSpec((1, M, D), lambda b, i: (b, 0, 1)),
                  _full((D, D))],
        out_specs=pl.BlockSpec((tm, D), lambda b, i: (b * ns + i, 0)),
        out_shape=jax.ShapeDtypeStruct((T, D), F32),
        compiler_params=_params("parallel", "parallel"),
    )(x2, g, wq, memkv, memkv, wo)


PAIR_LIST = tuple((a, b) for a in range(PEER_TOPK) for b in range(PEER_TOPK) if (a + 1) * (b + 1) <= PEER_TOPK)
N_PAIR_ROWS = -(-len(PAIR_LIST) // 8) * 8
PAIR_COUNT = tuple(PEER_TOPK // (a + 1) for a in range(PEER_TOPK))
PAIR_START = tuple(sum(PAIR_COUNT[:a]) for a in range(PEER_TOPK))


def _top_rows_exact(v, n):
    R = v.shape[0]
    ridx = lax.broadcasted_iota(jnp.int32, v.shape, 0).astype(F32)
    rank = jnp.full(v.shape, float(n), F32)
    vals = []
    for r in range(n):
        m = jnp.max(v, axis=0, keepdims=True)
        first = jnp.min(jnp.where(v == m, ridx, float(R)), axis=0, keepdims=True)
        taken = ridx == first
        v = jnp.where(taken, LOWEST, v)
        rank = jnp.where(taken, float(r), rank)
        vals.append(m)
    return vals, rank


def _peer_route_kernel(x_ref, g_ref, wqt_ref, sk_ref, hn_ref, c1_ref, g1_ref, r2_ref, g2_ref, qt_sc, s_sc, top_sc,
                       rank_sc, cand_sc, cw_sc):
    hn = _rms(x_ref[...], g_ref[...]).astype(BF16)
    hn_ref[...] = hn
    qt_sc[...] = _dot_nt(wqt_ref[...], hn).astype(BF16)
    nk, n = PEER_KEYS, PEER_TOPK
    n_lists = 2 * PEER_HEADS

    def one_list(li, carry):
        r0 = pl.multiple_of(li * nk, nk)
        s = _dot(sk_ref[li % 2], qt_sc[pl.ds(r0, nk), :])
        s_sc[pl.ds(r0, nk), :] = s
        vals, rank = _top_rows_exact(s, n)
        for r in range(n):
            top_sc[pl.ds(li * n + r, 1), :] = vals[r]
        rank_sc[pl.ds(r0, nk), :] = rank
        return carry

    lax.fori_loop(0, n_lists, one_list, 0)

    cand_sc[...] = jnp.full(cand_sc.shape, LOWEST, F32)
    cw_sc[...] = jnp.zeros(cw_sc.shape, F32)
    for h in range(PEER_HEADS):
        l1, l2 = 2 * h, 2 * h + 1
        top1 = [top_sc[l1 * n + a:l1 * n + a + 1, :] for a in range(n)]
        top2 = [top_sc[l2 * n + a:l2 * n + a + 1, :] for a in range(n)]
        e1 = [jnp.exp(t - top1[0]) for t in top1]
        e2 = [jnp.exp(t - top2[0]) for t in top2]
        for r, (a, b) in enumerate(PAIR_LIST):
            cand_sc[h, r:r + 1, :] = top1[a] + top2[b]
            cw_sc[h, r:r + 1, :] = e1[a] * e2[b]
        _, crank = _top_rows_exact(cand_sc[h], n)
        chosen = (crank < float(n)).astype(F32)
        z = jnp.sum(chosen * cw_sc[h], axis=0, keepdims=True)
        rank1 = rank_sc[l1 * nk:(l1 + 1) * nk, :]
        count1 = jnp.zeros(rank1.shape, F32)
        for a in range(n):
            n_a = jnp.sum(chosen[PAIR_START[a]:PAIR_START[a] + PAIR_COUNT[a]], axis=0, keepdims=True)
            count1 = jnp.where(rank1 == float(a), n_a, count1)
        rows = slice(h * nk, (h + 1) * nk)
        c1_ref[rows, :] = count1
        g1_ref[rows, :] = jnp.exp(s_sc[l1 * nk:(l1 + 1) * nk, :] - top1[0]) * (1.0 / z)
        r2_ref[rows, :] = rank_sc[l2 * nk:(l2 + 1) * nk, :].astype(BF16)
        g2_ref[rows, :] = jnp.exp(s_sc[l2 * nk:(l2 + 1) * nk, :] - top2[0]).astype(BF16)


def _peer_route(x2, g, wqt, sk, tm):
    T, D = x2.shape
    nr = PEER_HEADS * PEER_KEYS
    col = pl.BlockSpec((nr, tm), lambda i: (0, i))
    return pl.pallas_call(
        _peer_route_kernel,
        grid=(T // tm,),
        in_specs=[pl.BlockSpec((tm, D), lambda i: (i, 0)), _full((1, D)), _full(wqt.shape), _full(sk.shape)],
        out_specs=[pl.BlockSpec((tm, D), lambda i: (i, 0)), col, col, col, col],
        out_shape=[jax.ShapeDtypeStruct((T, D), BF16), jax.ShapeDtypeStruct((nr, T), F32),
                   jax.ShapeDtypeStruct((nr, T), F32), jax.ShapeDtypeStruct((nr, T), BF16),
                   jax.ShapeDtypeStruct((nr, T), BF16)],
        scratch_shapes=[pltpu.VMEM((2 * nr, tm), BF16), pltpu.VMEM((2 * nr, tm), F32),
                        pltpu.VMEM((2 * PEER_HEADS * PEER_TOPK, tm), F32),
                        pltpu.VMEM((2 * nr, tm), F32), pltpu.VMEM((PEER_HEADS, N_PAIR_ROWS, tm), F32),
                        pltpu.VMEM((PEER_HEADS, N_PAIR_ROWS, tm), F32)],
        compiler_params=_params("parallel"),
    )(x2, g, wqt, sk)


def _peer_expert_kernel(hn_ref, c1_ref, g1_ref, r2_ref, g2_ref, u_ref, vt_ref, x_ref, o_ref, acc_sc, wa_sc, act_sc, *,
                        te, sub):
    c = pl.program_id(1)
    nk = PEER_KEYS

    @pl.when(c == 0)
    def _():
        acc_sc[...] = jnp.zeros(acc_sc.shape, F32)

    tm = hn_ref.shape[0]

    def pre_act(j):
        act_sc[j % 2] = _dot_nt(u_ref[j * sub:(j + 1) * sub, :], hn_ref[...]).astype(BF16)

    pre_act(0)
    for j in range(te // sub):
        if j + 1 < te // sub:
            pre_act(j + 1)
        act = act_sc[j % 2]
        for il in range(sub // nk):
            i = c * (te // nk) + j * (sub // nk) + il
            w = jnp.zeros((nk, tm), BF16)
            for h in range(PEER_HEADS):
                count = c1_ref[pl.ds(h * nk + i, 1), :].astype(BF16)
                g1row = g1_ref[pl.ds(h * nk + i, 1), :].astype(BF16)
                rows = slice(h * nk, (h + 1) * nk)
                w = w + jnp.where(r2_ref[rows, :] < count, g2_ref[rows, :], 0.0) * g1row
            wa_sc[j, il * nk:(il + 1) * nk, :] = w * _gelu(act[il * nk:(il + 1) * nk])
        acc_sc[...] += _dot(vt_ref[:, j * sub:(j + 1) * sub], wa_sc[j])

    @pl.when(c == pl.num_programs(1) - 1)
    def _():
        o_ref[...] = x_ref[...] + acc_sc[...].T


def _peer_expert(hn, c1, g1, r2, g2, u, vt, x2, tm, te, sub):
    T, D = x2.shape
    ne = u.shape[0]
    tok = pl.BlockSpec((c1.shape[0], tm), lambda i, c: (0, i))
    row = pl.BlockSpec((tm, D), lambda i, c: (i, 0))
    return pl.pallas_call(
        functools.partial(_peer_expert_kernel, te=te, sub=sub),
        grid=(T // tm, ne // te),
        in_specs=[row, tok, tok, tok, tok,
                  pl.BlockSpec((te, D), lambda i, c: (c, 0)),
                  pl.BlockSpec((D, te), lambda i, c: (0, c)),
                  row],
        out_specs=row,
        out_shape=jax.ShapeDtypeStruct((T, D), F32),
        scratch_shapes=[pltpu.VMEM((D, tm), F32), pltpu.VMEM((te // sub, sub, tm), BF16),
                        pltpu.VMEM((2, sub, tm), BF16)],
        compiler_params=_params("parallel", "arbitrary"),
    )(hn, c1, g1, r2, g2, u, vt, x2)


def _final_norm_kernel(x_ref, g_ref, o_ref):
    o_ref[...] = _rms(x_ref[...], g_ref[...])


def _final_norm(x2, g, tm):
    T, D = x2.shape
    return pl.pallas_call(
        _final_norm_kernel,
        grid=(T // tm,),
        in_specs=[pl.BlockSpec((tm, D), lambda i: (i, 0)), _full((1, D))],
        out_specs=pl.BlockSpec((tm, D), lambda i: (i, 0)),
        out_shape=jax.ShapeDtypeStruct((T, D), F32),
        compiler_params=_params("parallel"),
    )(x2, g)


def _rope_tables(positions):
    B, S = positions.shape
    freqs = ROPE_THETA ** (-jnp.arange(ROT_HALF, dtype=F32) * 2.0 / ROT_DIM)
    ang = positions.astype(F32)[:, :, None] * freqs
    cos, sin = jnp.cos(ang), jnp.sin(ang)
    ones = jnp.ones((B, S, HEAD_DIM - ROT_DIM), F32)
    zeros8 = jnp.zeros((B, S, ROT_HALF), F32)
    zeros = jnp.zeros((B, S, HEAD_DIM - ROT_DIM), F32)
    cos_h = jnp.concatenate([cos, cos, ones], axis=-1)
    sa_h = jnp.concatenate([zeros8, sin, zeros], axis=-1)
    sb_h = jnp.concatenate([-sin, zeros8, zeros], axis=-1)
    rep = LANES // HEAD_DIM
    tile = lambda t: jnp.tile(t, (1, 1, rep)).reshape(B * S, LANES)
    return tile(cos_h), tile(sa_h), tile(sb_h)


def _overlap_matrix(n_chunk, n_sel):
    ci = np.arange(n_chunk)[:, None] * CMP_STRIDE
    sj = np.arange(LANES)[None, :] * SEL_BLOCK
    ov = (ci < sj + SEL_BLOCK) & (ci + CMP_BLOCK > sj) & (np.arange(LANES)[None, :] < n_sel)
    ov = ov & (np.arange(n_chunk)[:, None] < n_chunk - 1)
    return jnp.asarray(ov, BF16)


def kernel(x, mem, positions, norm_mix_g, w_in, cmp_pe, cmp_w1, cmp_w2, conv_w, conv_b, conv_ln_g, conv_ln_b, conv_pw_w, conv_pw_b, pool_w, pool_scale, w_out, norm_xa_g, norm_mem_g, xa_wq, xa_wkv, xa_wo, norm_ffn_g, peer_wq, peer_subkeys, peer_u, peer_v, final_g):
    B, S, D = x.shape
    T = B * S
    depth = w_in.shape[0]
    M = mem.shape[1]
    dc = conv_w.shape[-1]
    n_chunk = S // CMP_STRIDE
    n_sel = S // SEL_BLOCK
    assert S % 512 == 0 and n_sel <= LANES and n_sel >= SEL_TOPN and D % LANES == 0

    cos_t, sa_t, sb_t = _rope_tables(positions)
    overlap = _overlap_matrix(n_chunk, n_sel)
    row = lambda v: v.reshape(1, -1)
    x2 = x.reshape(T, D)
    mem2 = mem.reshape(B * M, D)
    n_gate = 3 * NSA_HEADS

    for l in range(depth):
        w = w_in[l]
        w_pad = jnp.concatenate([w[:, :D_QKV + n_gate], jnp.zeros((D, LANES - n_gate), F32), w[:, D_QKV + n_gate:]],
                                axis=1).astype(BF16)
        q, kv, misc = _inproj(x2, row(norm_mix_g[l]), w_pad, cos_t, sa_t, sb_t, tm=512)
        xc = kv[:, :2 * LANES].reshape(B, S, 2, NSA_KV_HEADS, HEAD_DIM)
        xc = jnp.transpose(xc, (2, 0, 3, 1, 4)).reshape(2, B * NSA_KV_HEADS, n_chunk, CMP_STRIDE * HEAD_DIM)
        pe = cmp_pe[l].reshape(2, 2, CMP_STRIDE * HEAD_DIM)
        cmp = _compress(xc, pe, cmp_w1[l].astype(BF16), cmp_w2[l].astype(BF16))
        cmp = cmp.reshape(2, B, NSA_KV_HEADS, n_chunk, HEAD_DIM)
        cmp = jnp.transpose(cmp, (0, 1, 3, 2, 4)).reshape(2, B, n_chunk, LANES).astype(BF16)
        o_nsa = _nsa(q, cmp[0], cmp[1], kv.reshape(B, S, -1), misc, overlap, B, S, tq=128)
        pg = dc // len(POOL_WINDOWS)
        plw = jnp.zeros((dc, dc), F32)
        for gi in range(len(POOL_WINDOWS)):
            plw = plw.at[gi * pg:(gi + 1) * pg, gi * pg:(gi + 1) * pg].set(pool_w[l, gi])
        x2 = _mixout(misc, o_nsa, x2, conv_w[l], row(conv_b[l]), row(conv_ln_g[l]), row(conv_ln_b[l]),
                     conv_pw_w[l].astype(BF16), row(conv_pw_b[l]), plw.astype(BF16), row(pool_scale[l]),
                     w_out[l].astype(BF16), B, S, ts=512)
        memkv = _norm_matmul(mem2, row(norm_mem_g[l]), xa_wkv[l].astype(BF16), tm=256, out_dtype=BF16)
        x2 = _xattn(x2, row(norm_xa_g[l]), xa_wq[l].astype(BF16), memkv.reshape(B, M, 2 * D),
                    xa_wo[l].astype(BF16), B, S, tm=512)
        hn, c1, g1, r2, g2 = _peer_route(x2, row(norm_ffn_g[l]), peer_wq[l].T.astype(BF16),
                                         peer_subkeys[l].astype(BF16), tm=512)
        x2 = _peer_expert(hn, c1, g1, r2, g2, peer_u[l].astype(BF16), peer_v[l].T.astype(BF16), x2,
                          tm=512, te=2048, sub=512)
    return _final_norm(x2, row(final_g), tm=512).reshape(B, S, D)
```

```python
import functools
import math

import jax
import jax.numpy as jnp
import numpy as np
from jax import lax
from jax.experimental import pallas as pl
from jax.experimental.pallas import tpu as pltpu

F32 = jnp.float32
BF16 = jnp.bfloat16

NSA_HEADS = 8
NSA_KV_HEADS = 2
NSA_GROUP = NSA_HEADS // NSA_KV_HEADS
HEAD_DIM = 64
D_NSA = NSA_HEADS * HEAD_DIM
D_KV = NSA_KV_HEADS * HEAD_DIM
ROT_DIM = HEAD_DIM // 4
ROT_HALF = ROT_DIM // 2
ROPE_THETA = 500000.0
CMP_BLOCK = 32
CMP_STRIDE = 16
SEL_BLOCK = 64
SEL_TOPN = 16
WINDOW = 512
CONV_WIDTH = 31
POOL_WINDOWS = (2, 4, 8, 16)
XA_HEADS = 4
PEER_HEADS = 8
PEER_KEYS = 128
PEER_TOPK = 16
EPS = 1e-6
NEG = -1e30
FORCE = 1e4
LOWEST = -3.0e38

LANES = 128
SUBLANES = 8
VMEM_LIMIT = 56 * 1024 * 1024

NT_DIMS = (((1,), (1,)), ((), ()))

TM_PROJ = 512
TM_MEM = 256
TQ_NSA = 128
KC_SEL = 512
TS_MIX = 512
TM_PEER = 512
EXPERT_SUBS = (256, 768, 768, 256)


def _params(*sem):
    return pltpu.CompilerParams(dimension_semantics=sem, vmem_limit_bytes=VMEM_LIMIT)


def _full(shape):
    nd = len(shape)
    return pl.BlockSpec(shape, lambda *_: (0,) * nd)


def _rms(x, g):
    return x * lax.rsqrt(jnp.mean(x * x, axis=-1, keepdims=True) + EPS) * g


def _gelu(x):
    c = math.sqrt(2.0 / math.pi)
    return 0.5 * x * (1.0 + jnp.tanh(c * (x + 0.044715 * (x * x * x))))


def _dot(a, b):
    return jnp.dot(a, b, preferred_element_type=F32)


def _dot_nt(a, b):
    return lax.dot_general(a, b, NT_DIMS, preferred_element_type=F32)


N_ROPE_Q = D_NSA // LANES
N_KV_CHUNKS = 6
D_QKV = D_NSA + N_KV_CHUNKS * D_KV
N_KV_OUT = 2 + 2 * NSA_KV_HEADS


def _inproj_kernel(x_ref, g_ref, w_ref, cos_ref, sa_ref, sb_ref, q_ref, kvc_ref, kv_ref, misc_ref):
    y = _rms(x_ref[...], g_ref[...])
    z = _dot(y.astype(BF16), w_ref[...])
    cos, sa, sb = cos_ref[...], sa_ref[...], sb_ref[...]

    def rope(c):
        return c * cos + pltpu.roll(c, ROT_HALF, 1) * sa + pltpu.roll(c, LANES - ROT_HALF, 1) * sb

    scale = HEAD_DIM ** -0.5
    for j in range(N_ROPE_Q):
        q_ref[:, j * LANES:(j + 1) * LANES] = (rope(z[:, j * LANES:(j + 1) * LANES]) * scale).astype(BF16)
    lane = lax.broadcasted_iota(jnp.int32, (z.shape[0], LANES), 1)
    out = 0
    for j in range(N_KV_CHUNKS):
        c = z[:, D_NSA + j * LANES:D_NSA + (j + 1) * LANES]
        if j % 2 == 0:
            c = rope(c)
        if j < 2:
            kvc_ref[:, j * LANES:(j + 1) * LANES] = c.astype(BF16)
        elif j % 2 == 0:
            kv_ref[:, out * LANES:(out + 1) * LANES] = c.astype(BF16)
            out += 1
    for j in (3, 5):
        c = z[:, D_NSA + j * LANES:D_NSA + (j + 1) * LANES]
        for k in range(NSA_KV_HEADS):
            own = (lane >= HEAD_DIM) if k == 1 else (lane < HEAD_DIM)
            kv_ref[:, out * LANES:(out + 1) * LANES] = jnp.where(own, c, 1.0).astype(BF16)
            out += 1
    misc_ref[...] = z[:, D_QKV:]


def _inproj(x2, g, w_pad, cos_t, sa_t, sb_t, tm):
    T, D = x2.shape
    n = w_pad.shape[1]
    n_misc = n - D_QKV
    row = lambda w: pl.BlockSpec((tm, w), lambda i: (i, 0))
    return pl.pallas_call(
        _inproj_kernel,
        grid=(T // tm,),
        in_specs=[row(D), _full((1, D)), _full((D, n)), row(LANES), row(LANES), row(LANES)],
        out_specs=[row(D_NSA), row(2 * LANES), row(N_KV_OUT * LANES), row(n_misc)],
        out_shape=[jax.ShapeDtypeStruct((T, D_NSA), BF16),
                   jax.ShapeDtypeStruct((T, 2 * LANES), BF16),
                   jax.ShapeDtypeStruct((T, N_KV_OUT * LANES), BF16),
                   jax.ShapeDtypeStruct((T, n_misc), F32)],
        compiler_params=_params("parallel"),
    )(x2, g, w_pad, cos_t, sa_t, sb_t)


def _compress_kernel(x_ref, pe_ref, w1_ref, w2_ref, o_ref):
    x = x_ref[0].astype(F32)
    n = x.shape[0]
    out = jnp.zeros((n, LANES), F32)
    for hd in range(NSA_KV_HEADS):
        a = _dot((x + pe_ref[0, hd, 0:1]).astype(BF16), w1_ref[0, hd, 0])
        b = _dot((x + pe_ref[0, hd, 1:2]).astype(BF16), w1_ref[0, hd, 1])
        hid = _gelu(a + pltpu.roll(b, n - 1, 0))
        out = out + _dot(hid.astype(BF16), w2_ref[0, hd])
    o_ref[0, 0] = out.astype(BF16)


def _compress(xc, pe, w1, w2):
    nb, nch, width = xc.shape
    hid = w1.shape[-1]
    return pl.pallas_call(
        _compress_kernel,
        grid=(2, nb),
        in_specs=[pl.BlockSpec((1, nch, width), lambda w, i: (i, 0, 0)),
                  pl.BlockSpec((1, NSA_KV_HEADS, 2, width), lambda w, i: (w, 0, 0, 0)),
                  pl.BlockSpec((1, NSA_KV_HEADS, 2, width, hid), lambda w, i: (w, 0, 0, 0, 0)),
                  pl.BlockSpec((1, NSA_KV_HEADS, hid, LANES), lambda w, i: (w, 0, 0, 0))],
        out_specs=pl.BlockSpec((1, 1, nch, LANES), lambda w, i: (w, i, 0, 0)),
        out_shape=jax.ShapeDtypeStruct((2, nb, nch, LANES), BF16),
        compiler_params=_params("parallel", "parallel"),
    )(xc, pe, w1, w2)


def _compress_weights(pe, w1, w2):
    hid = w1.shape[-1]
    eye_t = jnp.eye(2, dtype=F32)[:, None, None, None, :, None, None]
    eye_h = jnp.eye(NSA_KV_HEADS, dtype=F32)[None, :, None, None, None, :, None]
    per = (2, 1, 2, CMP_STRIDE, 1, 1, HEAD_DIM)
    width = CMP_STRIDE * 2 * NSA_KV_HEADS * HEAD_DIM
    pe_big = (pe.reshape(per) * eye_t * eye_h).reshape(2, NSA_KV_HEADS, 2, width)
    w1_big = (w1.reshape(per + (hid,)) * eye_t[..., None] * eye_h[..., None]).reshape(2, NSA_KV_HEADS, 2, width, hid)
    eye_o = jnp.eye(NSA_KV_HEADS, dtype=F32)[None, :, None, :, None]
    w2_big = (w2[:, None, :, None, :] * eye_o).reshape(2, NSA_KV_HEADS, hid, NSA_KV_HEADS * HEAD_DIM)
    return pe_big, w1_big.astype(BF16), w2_big.astype(BF16)


def _nsa_kernel(q_ref, kcmp_ref, vcmp_ref, ks_ref, kw_ref, vs0_ref, vs1_ref, vw0_ref, vw1_ref, gate_ref, ov_ref,
                o_ref, s_sc, mrun_sc, acc_sc, *, tq, kc_sel, kc_win):
    t0 = pl.program_id(1) * tq
    rows = NSA_GROUP * tq
    lane = lax.broadcasted_iota(jnp.int32, (tq, LANES), 1)
    lane_r = lax.broadcasted_iota(jnp.int32, (rows, LANES), 1)
    tpos = t0 + lax.broadcasted_iota(jnp.int32, (tq, 1), 0)
    gates = jax.nn.sigmoid(gate_ref[...])
    n_cmp = kcmp_ref.shape[1]
    n_sel = ks_ref.shape[1] // SEL_BLOCK
    vs_refs, vw_refs = (vs0_ref, vs1_ref), (vw0_ref, vw1_ref)

    def normalise(acc, own):
        return acc / jnp.where(own, pltpu.roll(acc, HEAD_DIM, 1), 1.0)

    def own_lanes(lanes, k):
        return (lanes >= HEAD_DIM) if k == 1 else (lanes < HEAD_DIM)

    def queries(k):
        parts = []
        for g in range(NSA_GROUP):
            hh = k * NSA_GROUP + g
            c = q_ref[:, (hh // 2) * LANES:(hh // 2 + 1) * LANES].astype(F32)
            if hh % 2 != k:
                c = pltpu.roll(c, HEAD_DIM, 1)
            parts.append(jnp.where(own_lanes(lane, k), c, 0.0))
        return jnp.concatenate(parts, axis=0).astype(BF16)

    def compressed(qs):
        s = _dot_nt(qs, kcmp_ref[0]).reshape(NSA_GROUP, tq, n_cmp)
        cmp_end = lax.broadcasted_iota(jnp.int32, (tq, n_cmp), 1) * CMP_STRIDE + (CMP_BLOCK - 1)
        valid = cmp_end <= tpos
        s = s + jnp.where(valid, 0.0, NEG)[None]
        e = jnp.exp(s - jnp.max(s, axis=-1, keepdims=True)) * valid.astype(F32)[None]
        l = jnp.sum(e, axis=-1, keepdims=True)
        p3 = e / jnp.where(l > 0.0, l, 1.0)
        o_cmp = _dot(p3.reshape(rows, n_cmp).astype(BF16), vcmp_ref[0])

        psum = p3[0] + p3[1] + p3[2] + p3[3]
        ov = ov_ref[...]
        p_hi = psum.astype(BF16)
        r1 = psum - p_hi.astype(F32)
        p_mid = r1.astype(BF16)
        p_lo = (r1 - p_mid.astype(F32)).astype(BF16)
        imp = _dot(p_hi, ov) + _dot(p_mid, ov) + _dot(p_lo, ov)
        imp = imp.T[:n_sel]
        blk = lax.broadcasted_iota(jnp.int32, (n_sel, tq), 0)
        tpos_t = t0 + lax.broadcasted_iota(jnp.int32, (1, tq), 1)
        cur = tpos_t // SEL_BLOCK
        forced = (blk == 0) | (blk == cur) | (blk == cur - 1)
        imp = jnp.where(blk * SEL_BLOCK <= tpos_t, jnp.where(forced, FORCE, imp), NEG)
        sub8 = lax.broadcasted_iota(jnp.int32, (SUBLANES, tq), 0)
        groups = [imp[r:r + SUBLANES] for r in range(0, n_sel, SUBLANES)]
        ranks = [jnp.zeros((SUBLANES, tq), jnp.int32) for _ in groups]
        for jp in range(n_sel):
            one = imp[jp:jp + 1, :]
            for gi, grp in enumerate(groups):
                lo = gi * SUBLANES
                if lo > jp:
                    beats = one >= grp
                elif lo + SUBLANES - 1 < jp:
                    beats = one > grp
                else:
                    beats = (one > grp) | ((one == grp) & (sub8 > jp - lo))
                ranks[gi] = ranks[gi] + beats.astype(jnp.int32)
        rank = jnp.concatenate(ranks, axis=0)
        sel_t = jnp.concatenate([(rank < SEL_TOPN).astype(F32), jnp.zeros((LANES - n_sel, tq), F32)], axis=0)
        return o_cmp, sel_t.T.astype(BF16)

    def window(k, qs):
        w0 = pl.multiple_of(jnp.maximum(t0 + tq - kc_win, 0), tq)
        sw = _dot_nt(qs, kw_ref[0, pl.ds(w0, kc_win), :]).reshape(NSA_GROUP, tq, kc_win)
        wpos = w0 + lax.broadcasted_iota(jnp.int32, (tq, kc_win), 1)
        in_win = (wpos <= tpos) & (wpos > tpos - WINDOW)
        sw = (sw + jnp.where(in_win, 0.0, NEG)[None]).reshape(rows, kc_win)
        pw = jnp.exp(sw - jnp.max(sw, axis=1, keepdims=True))
        return normalise(_dot(pw.astype(BF16), vw_refs[k][0, pl.ds(w0, kc_win), :]), own_lanes(lane_r, k))

    heads = range(NSA_KV_HEADS)
    qs = [queries(k) for k in heads]
    o_cmp, sel = zip(*[compressed(qs[k]) for k in heads])
    o_win = [window(k, qs[k]) for k in heads]

    n_chunks = (t0 + tq + kc_sel - 1) // kc_sel
    mrun_sc[...] = jnp.full(mrun_sc.shape, NEG, F32)

    def scores(c, carry):
        k0 = pl.multiple_of(c * kc_sel, kc_sel)
        jrow = lax.broadcasted_iota(jnp.int32, (LANES, kc_sel), 0)
        kcol = lax.broadcasted_iota(jnp.int32, (LANES, kc_sel), 1)
        expand = (jrow == k0 // SEL_BLOCK + kcol // SEL_BLOCK).astype(BF16)
        causal = k0 + lax.broadcasted_iota(jnp.int32, (tq, kc_sel), 1) <= tpos
        kb = ks_ref[0, pl.ds(k0, kc_sel), :]
        for k in heads:
            seen = (_dot(sel[k], expand) > 0.5) & causal
            s = _dot_nt(qs[k], kb).reshape(NSA_GROUP, tq, kc_sel)
            s = (s + jnp.where(seen, 0.0, NEG)[None]).reshape(rows, kc_sel)
            s_sc[k, c] = s
            m = mrun_sc[k]
            for j in range(kc_sel // LANES):
                m = jnp.maximum(m, s[:, j * LANES:(j + 1) * LANES])
            mrun_sc[k] = m
        return carry

    lax.fori_loop(0, n_chunks, scores, 0)
    m_sel = [jnp.broadcast_to(jnp.max(mrun_sc[k], axis=1, keepdims=True), (rows, LANES)) for k in heads]
    acc_sc[...] = jnp.zeros(acc_sc.shape, F32)

    def weigh(c, carry):
        k0 = pl.multiple_of(c * kc_sel, kc_sel)
        for k in heads:
            s = s_sc[k, c]
            p = jnp.concatenate([jnp.exp(s[:, j * LANES:(j + 1) * LANES] - m_sel[k])
                                 for j in range(kc_sel // LANES)], axis=1)
            acc_sc[k] += _dot(p.astype(BF16), vs_refs[k][0, pl.ds(k0, kc_sel), :])
        return carry

    lax.fori_loop(0, n_chunks, weigh, 0)

    for k in heads:
        o_sel = normalise(acc_sc[k], own_lanes(lane_r, k))
        outs = []
        for g in range(NSA_GROUP):
            r = slice(g * tq, (g + 1) * tq)
            gi = (k * NSA_GROUP + g) * 3
            og = (gates[:, gi:gi + 1] * o_cmp[k][r] + gates[:, gi + 1:gi + 2] * o_sel[r]
                  + gates[:, gi + 2:gi + 3] * o_win[k][r])
            if g % 2 != k:
                og = pltpu.roll(og, HEAD_DIM, 1)
            outs.append(og)
        for j in range(NSA_GROUP // 2):
            chunk = jnp.where(lane < HEAD_DIM, outs[2 * j], outs[2 * j + 1])
            cj = k * (NSA_GROUP // 2) + j
            o_ref[:, cj * LANES:(cj + 1) * LANES] = chunk.astype(BF16)


def _nsa(q, kcmp, vcmp, kv, gates_misc, overlap, B, S, tq):
    kc_sel, kc_win = KC_SEL, WINDOW + tq
    nq = S // tq
    n_cmp = kcmp.shape[1]
    rows = NSA_GROUP * tq
    kvspec = lambda j: pl.BlockSpec((1, S, LANES), lambda b, i: (b, 0, j))
    return pl.pallas_call(
        functools.partial(_nsa_kernel, tq=tq, kc_sel=kc_sel, kc_win=kc_win),
        grid=(B, nq),
        in_specs=[pl.BlockSpec((tq, D_NSA), lambda b, i: (b * nq + i, 0)),
                  pl.BlockSpec((1, n_cmp, LANES), lambda b, i: (b, 0, 0)),
                  pl.BlockSpec((1, n_cmp, LANES), lambda b, i: (b, 0, 0)),
                  kvspec(0), kvspec(1), kvspec(2), kvspec(3), kvspec(4), kvspec(5),
                  pl.BlockSpec((tq, LANES), lambda b, i: (b * nq + i, 0)),
                  _full(overlap.shape)],
        out_specs=pl.BlockSpec((tq, D_NSA), lambda b, i: (b * nq + i, 0)),
        out_shape=jax.ShapeDtypeStruct((B * S, D_NSA), BF16),
        scratch_shapes=[pltpu.VMEM((NSA_KV_HEADS, S // kc_sel, rows, kc_sel), F32),
                        pltpu.VMEM((NSA_KV_HEADS, rows, LANES), F32), pltpu.VMEM((NSA_KV_HEADS, rows, LANES), F32)],
        compiler_params=_params("parallel", "parallel"),
    )(q, kcmp, vcmp, kv, kv, kv, kv, kv, kv, gates_misc, overlap)


HALO = 32


def _mixout_kernel(mc_ref, mp_ref, on_ref, x_ref, cw_ref, cb_ref, lg_ref, lb_ref, pw_ref, pb_ref, plw_ref,
                   pls_ref, wo_ref, o_ref, hbuf, pbuf, *, ts, dc):
    i = pl.program_id(1)
    first = i == 0
    off_a, off_b, off_p = LANES, LANES + dc, LANES + 2 * dc

    def glu(ref, r):
        return ref[r, off_a:off_a + dc] * jax.nn.sigmoid(ref[r, off_b:off_b + dc])

    tail = slice(ts - HALO, ts)
    hbuf[0:HALO, :] = jnp.where(first, 0.0, glu(mp_ref, tail))
    hbuf[HALO:, :] = glu(mc_ref, slice(None))
    pbuf[0:HALO, :] = jnp.where(first, 0.0, mp_ref[tail, off_p:off_p + dc])
    pcur = mc_ref[:, off_p:off_p + dc]
    pbuf[HALO:, :] = pcur

    acc = jnp.zeros((ts, dc), F32) + cb_ref[...]
    for w in range(CONV_WIDTH):
        acc = acc + hbuf[pl.ds(HALO - (CONV_WIDTH - 1) + w, ts), :] * cw_ref[w:w + 1, :]
    mu = jnp.mean(acc, axis=-1, keepdims=True)
    xc = acc - mu
    y = xc * lax.rsqrt(jnp.mean(xc * xc, axis=-1, keepdims=True) + EPS) * lg_ref[...] + lb_ref[...]
    y = y * jax.nn.sigmoid(y)
    o_conv = _dot(y.astype(BF16), pw_ref[...]) + pb_ref[...]

    tglob = i * ts + lax.broadcasted_iota(jnp.int32, (ts, dc), 0)
    lane = lax.broadcasted_iota(jnp.int32, (ts, dc), 1)
    pg = dc // len(POOL_WINDOWS)
    run = pcur
    d = 1
    mean = jnp.zeros((ts, dc), F32)
    for gi, w in enumerate(POOL_WINDOWS):
        while d < w:
            run = run + pbuf[pl.ds(HALO - d, ts), :]
            d += 1
        cnt = jnp.minimum(tglob + 1, w).astype(F32)
        mean = jnp.where(lane // pg == gi, run / cnt, mean)
    o_pool = _dot((mean - pcur).astype(BF16), plw_ref[...]) * pls_ref[...]

    dn = on_ref.shape[1]
    o_ref[...] = (x_ref[...] + _dot(on_ref[...], wo_ref[0:dn, :])
                  + _dot(o_conv.astype(BF16), wo_ref[dn:dn + dc, :])
                  + _dot(o_pool.astype(BF16), wo_ref[dn + dc:, :]))


def _mixout(misc, o_nsa, x2, cw, cb, lg, lb, pw, pb, plw, pls, wo, B, S, ts):
    T, D = x2.shape
    dc = cw.shape[1]
    ns = S // ts
    nm = misc.shape[1]
    cur = lambda w: pl.BlockSpec((ts, w), lambda b, i: (b * ns + i, 0))
    prev = pl.BlockSpec((ts, nm), lambda b, i: (b * ns + jnp.maximum(i - 1, 0), 0))
    return pl.pallas_call(
        functools.partial(_mixout_kernel, ts=ts, dc=dc),
        grid=(B, ns),
        in_specs=[cur(nm), prev, cur(o_nsa.shape[1]), cur(D), _full(cw.shape), _full(cb.shape), _full(lg.shape),
                  _full(lb.shape), _full(pw.shape), _full(pb.shape), _full(plw.shape), _full(pls.shape),
                  _full(wo.shape)],
        out_specs=cur(D),
        out_shape=jax.ShapeDtypeStruct((T, D), F32),
        scratch_shapes=[pltpu.VMEM((ts + HALO, dc), F32), pltpu.VMEM((ts + HALO, dc), F32)],
        compiler_params=_params("parallel", "parallel"),
    )(misc, misc, o_nsa, x2, cw, cb, lg, lb, pw, pb, plw, pls, wo)


def _norm_matmul_kernel(x_ref, g_ref, w_ref, o_ref):
    o_ref[...] = _dot(_rms(x_ref[...], g_ref[...]).astype(BF16), w_ref[...]).astype(o_ref.dtype)


def _norm_matmul(x2, g, w, tm, out_dtype):
    T, D = x2.shape
    n = w.shape[1]
    return pl.pallas_call(
        _norm_matmul_kernel,
        grid=(T // tm,),
        in_specs=[pl.BlockSpec((tm, D), lambda i: (i, 0)), _full((1, D)), _full((D, n))],
        out_specs=pl.BlockSpec((tm, n), lambda i: (i, 0)),
        out_shape=jax.ShapeDtypeStruct((T, n), out_dtype),
        compiler_params=_params("parallel"),
    )(x2, g, w)


def _xattn_kernel(x_ref, g_ref, wq_ref, k_ref, v_ref, wo_ref, o_ref):
    x = x_ref[...]
    D = x.shape[1]
    dh = D // XA_HEADS
    q = _dot(_rms(x, g_ref[...]).astype(BF16), wq_ref[...]) * (dh ** -0.5)
    outs = []
    for h in range(XA_HEADS):
        c = slice(h * dh, (h + 1) * dh)
        s = _dot_nt(q[:, c].astype(BF16), k_ref[0, :, c])
        e = jnp.exp(s - jnp.max(s, axis=-1, keepdims=True))
        p = e / jnp.sum(e, axis=-1, keepdims=True)
        outs.append(_dot(p.astype(BF16), v_ref[0, :, c]))
    o = jnp.concatenate(outs, axis=1)
    o_ref[...] = x + _dot(o.astype(BF16), wo_ref[...])


def _xattn(x2, g, wq, memkv, wo, B, S, tm):
    T, D = x2.shape
    ns = S // tm
    M = memkv.shape[1]
    return pl.pallas_call(
        _xattn_kernel,
        grid=(B, ns),
        in_specs=[pl.BlockSpec((tm, D), lambda b, i: (b * ns + i, 0)), _full((1, D)), _full((D, D)),
                  pl.BlockSpec((1, M, D), lambda b, i: (b, 0, 0)),
                  pl.BlockSpec((1, M, D), lambda b, i: (b, 0, 1)),
                  _full((D, D))],
        out_specs=pl.BlockSpec((tm, D), lambda b, i: (b * ns + i, 0)),
        out_shape=jax.ShapeDtypeStruct((T, D), F32),
        compiler_params=_params("parallel", "parallel"),
    )(x2, g, wq, memkv, memkv, wo)


PAIR_LIST = tuple((a, b) for a in range(PEER_TOPK) for b in range(PEER_TOPK) if (a + 1) * (b + 1) <= PEER_TOPK)
N_PAIR_ROWS = -(-len(PAIR_LIST) // 8) * 8
PAIR_COUNT = tuple(PEER_TOPK // (a + 1) for a in range(PEER_TOPK))
PAIR_START = tuple(sum(PAIR_COUNT[:a]) for a in range(PEER_TOPK))


def _top_rows_exact(v, n):
    R = v.shape[0]
    ridx = lax.broadcasted_iota(jnp.int32, v.shape, 0).astype(F32)
    rank = jnp.full(v.shape, float(n), F32)
    vals = []
    for r in range(n):
        m = jnp.max(v, axis=0, keepdims=True)
        first = jnp.min(jnp.where(v == m, ridx, float(R)), axis=0, keepdims=True)
        taken = ridx == first
        v = jnp.where(taken, LOWEST, v)
        rank = jnp.where(taken, float(r), rank)
        vals.append(m)
    return vals, rank


def _peer_route_kernel(x_ref, g_ref, wqt_ref, sk_ref, hn_ref, c1_ref, g1_ref, r2_ref, g2_ref, qt_sc, s_sc, top_sc,
                       rank_sc, cand_sc, cw_sc):
    hn = _rms(x_ref[...], g_ref[...]).astype(BF16)
    hn_ref[...] = hn
    qt_sc[...] = _dot_nt(wqt_ref[...], hn).astype(BF16)
    nk, n = PEER_KEYS, PEER_TOPK
    n_lists = 2 * PEER_HEADS

    def one_list(li, carry):
        r0 = pl.multiple_of(li * nk, nk)
        s = _dot(sk_ref[li % 2], qt_sc[pl.ds(r0, nk), :])
        s_sc[pl.ds(r0, nk), :] = s
        vals, rank = _top_rows_exact(s, n)
        for r in range(n):
            top_sc[pl.ds(li * n + r, 1), :] = vals[r]
        rank_sc[pl.ds(r0, nk), :] = rank
        return carry

    lax.fori_loop(0, n_lists, one_list, 0)

    cand_sc[...] = jnp.full(cand_sc.shape, LOWEST, F32)
    cw_sc[...] = jnp.zeros(cw_sc.shape, F32)
    for h in range(PEER_HEADS):
        l1, l2 = 2 * h, 2 * h + 1
        top1 = [top_sc[l1 * n + a:l1 * n + a + 1, :] for a in range(n)]
        top2 = [top_sc[l2 * n + a:l2 * n + a + 1, :] for a in range(n)]
        e1 = [jnp.exp(t - top1[0]) for t in top1]
        e2 = [jnp.exp(t - top2[0]) for t in top2]
        for r, (a, b) in enumerate(PAIR_LIST):
            cand_sc[h, r:r + 1, :] = top1[a] + top2[b]
            cw_sc[h, r:r + 1, :] = e1[a] * e2[b]
        _, crank = _top_rows_exact(cand_sc[h], n)
        chosen = (crank < float(n)).astype(F32)
        z = jnp.sum(chosen * cw_sc[h], axis=0, keepdims=True)
        rank1 = rank_sc[l1 * nk:(l1 + 1) * nk, :]
        count1 = jnp.zeros(rank1.shape, F32)
        for a in range(n):
            n_a = jnp.sum(chosen[PAIR_START[a]:PAIR_START[a] + PAIR_COUNT[a]], axis=0, keepdims=True)
            count1 = jnp.where(rank1 == float(a), n_a, count1)
        rows = slice(h * nk, (h + 1) * nk)
        c1_ref[rows, :] = count1
        g1_ref[rows, :] = jnp.exp(s_sc[l1 * nk:(l1 + 1) * nk, :] - top1[0]) * (1.0 / z)
        r2_ref[rows, :] = rank_sc[l2 * nk:(l2 + 1) * nk, :].astype(BF16)
        g2_ref[rows, :] = jnp.exp(s_sc[l2 * nk:(l2 + 1) * nk, :] - top2[0]).astype(BF16)


def _peer_route(x2, g, wqt, sk, tm):
    T, D = x2.shape
    nr = PEER_HEADS * PEER_KEYS
    col = pl.BlockSpec((nr, tm), lambda i: (0, i))
    return pl.pallas_call(
        _peer_route_kernel,
        grid=(T // tm,),
        in_specs=[pl.BlockSpec((tm, D), lambda i: (i, 0)), _full((1, D)), _full(wqt.shape), _full(sk.shape)],
        out_specs=[pl.BlockSpec((tm, D), lambda i: (i, 0)), col, col, col, col],
        out_shape=[jax.ShapeDtypeStruct((T, D), BF16), jax.ShapeDtypeStruct((nr, T), F32),
                   jax.ShapeDtypeStruct((nr, T), F32), jax.ShapeDtypeStruct((nr, T), BF16),
                   jax.ShapeDtypeStruct((nr, T), BF16)],
        scratch_shapes=[pltpu.VMEM((2 * nr, tm), BF16), pltpu.VMEM((2 * nr, tm), F32),
                        pltpu.VMEM((2 * PEER_HEADS * PEER_TOPK, tm), F32),
                        pltpu.VMEM((2 * nr, tm), F32), pltpu.VMEM((PEER_HEADS, N_PAIR_ROWS, tm), F32),
                        pltpu.VMEM((PEER_HEADS, N_PAIR_ROWS, tm), F32)],
        compiler_params=_params("parallel"),
    )(x2, g, wqt, sk)


def _peer_expert_kernel(*refs, subs, final_norm):
    hn_ref, c1_ref, g1_ref, r2_ref, g2_ref, u_ref, vt_ref, x_ref = refs[:8]
    fg_ref = refs[8] if final_norm else None
    o_ref, acc_sc, wa_sc, act_sc = refs[-4:]
    c = pl.program_id(1)
    nk = PEER_KEYS
    te = sum(subs)
    offs = [sum(subs[:j]) for j in range(len(subs))]
    tm = hn_ref.shape[0]

    @pl.when(c == 0)
    def _():
        acc_sc[...] = jnp.zeros(acc_sc.shape, F32)

    def pre_act(j):
        o, rows = offs[j], subs[j]
        act_sc[j % 2, 0:rows, :] = _dot_nt(u_ref[o:o + rows, :], hn_ref[...]).astype(BF16)

    pre_act(0)
    for j, (o, rows) in enumerate(zip(offs, subs)):
        if j + 1 < len(subs):
            pre_act(j + 1)
        for il in range(rows // nk):
            i = c * (te // nk) + o // nk + il
            w = jnp.zeros((nk, tm), BF16)
            for h in range(PEER_HEADS):
                count = c1_ref[pl.ds(h * nk + i, 1), :].astype(BF16)
                g1row = g1_ref[pl.ds(h * nk + i, 1), :].astype(BF16)
                hr = slice(h * nk, (h + 1) * nk)
                w = w + jnp.where(r2_ref[hr, :] < count, g2_ref[hr, :], 0.0) * g1row
            er = slice(il * nk, (il + 1) * nk)
            wa_sc[o + il * nk:o + (il + 1) * nk, :] = w * _gelu(act_sc[j % 2, er, :])
        acc_sc[...] += _dot(vt_ref[:, o:o + rows], wa_sc[o:o + rows, :])

    @pl.when(c == pl.num_programs(1) - 1)
    def _():
        out = x_ref[...] + acc_sc[...].T
        o_ref[...] = _rms(out, fg_ref[...]) if final_norm else out


def _peer_expert(hn, c1, g1, r2, g2, u, vt, x2, final_g, tm, subs):
    T, D = x2.shape
    ne = u.shape[0]
    te = sum(subs)
    tok = pl.BlockSpec((c1.shape[0], tm), lambda i, c: (0, i))
    row = pl.BlockSpec((tm, D), lambda i, c: (i, 0))
    return pl.pallas_call(
        functools.partial(_peer_expert_kernel, subs=subs, final_norm=final_g is not None),
        grid=(T // tm, ne // te),
        in_specs=[row, tok, tok, tok, tok,
                  pl.BlockSpec((te, D), lambda i, c: (c, 0)),
                  pl.BlockSpec((D, te), lambda i, c: (0, c)),
                  row] + ([] if final_g is None else [_full((1, D))]),
        out_specs=row,
        out_shape=jax.ShapeDtypeStruct((T, D), F32),
        scratch_shapes=[pltpu.VMEM((D, tm), F32), pltpu.VMEM((te, tm), BF16),
                        pltpu.VMEM((2, max(subs), tm), BF16)],
        compiler_params=_params("parallel", "arbitrary"),
    )(hn, c1, g1, r2, g2, u, vt, x2, *([] if final_g is None else [final_g]))


def _rope_tables(positions):
    B, S = positions.shape
    freqs = ROPE_THETA ** (-jnp.arange(ROT_HALF, dtype=F32) * 2.0 / ROT_DIM)
    ang = positions.astype(F32)[:, :, None] * freqs
    cos, sin = jnp.cos(ang), jnp.sin(ang)
    ones = jnp.ones((B, S, HEAD_DIM - ROT_DIM), F32)
    zeros8 = jnp.zeros((B, S, ROT_HALF), F32)
    zeros = jnp.zeros((B, S, HEAD_DIM - ROT_DIM), F32)
    cos_h = jnp.concatenate([cos, cos, ones], axis=-1)
    sa_h = jnp.concatenate([zeros8, sin, zeros], axis=-1)
    sb_h = jnp.concatenate([-sin, zeros8, zeros], axis=-1)
    rep = LANES // HEAD_DIM
    tile = lambda t: jnp.tile(t, (1, 1, rep)).reshape(B * S, LANES)
    return tile(cos_h), tile(sa_h), tile(sb_h)


def _overlap_matrix(n_chunk, n_sel):
    ci = np.arange(n_chunk)[:, None] * CMP_STRIDE
    sj = np.arange(LANES)[None, :] * SEL_BLOCK
    ov = (ci < sj + SEL_BLOCK) & (ci + CMP_BLOCK > sj) & (np.arange(LANES)[None, :] < n_sel)
    ov = ov & (np.arange(n_chunk)[:, None] < n_chunk - 1)
    return jnp.asarray(ov, BF16)


def kernel(x, mem, positions, norm_mix_g, w_in, cmp_pe, cmp_w1, cmp_w2, conv_w, conv_b, conv_ln_g, conv_ln_b, conv_pw_w, conv_pw_b, pool_w, pool_scale, w_out, norm_xa_g, norm_mem_g, xa_wq, xa_wkv, xa_wo, norm_ffn_g, peer_wq, peer_subkeys, peer_u, peer_v, final_g):
    B, S, D = x.shape
    T = B * S
    depth = w_in.shape[0]
    M = mem.shape[1]
    dc = conv_w.shape[-1]
    n_chunk = S // CMP_STRIDE
    n_sel = S // SEL_BLOCK
    assert S % max(TM_PROJ, TS_MIX, KC_SEL, TM_PEER) == 0 and SEL_TOPN <= n_sel <= LANES and D % LANES == 0
    assert S >= WINDOW + TQ_NSA and peer_u.shape[1] % sum(EXPERT_SUBS) == 0

    cos_t, sa_t, sb_t = _rope_tables(positions)
    overlap = _overlap_matrix(n_chunk, n_sel)
    row = lambda v: v.reshape(1, -1)
    x2 = x.reshape(T, D)
    mem2 = mem.reshape(B * M, D)
    n_gate = 3 * NSA_HEADS

    w_pad = jnp.concatenate([w_in[:, :, :D_QKV + n_gate], jnp.zeros((depth, D, LANES - n_gate), F32),
                             w_in[:, :, D_QKV + n_gate:]], axis=2).astype(BF16)
    pg = dc // len(POOL_WINDOWS)
    plw = jnp.zeros((depth, dc, dc), F32)
    for gi in range(len(POOL_WINDOWS)):
        plw = plw.at[:, gi * pg:(gi + 1) * pg, gi * pg:(gi + 1) * pg].set(pool_w[:, gi])
    plw, conv_pw, wo = plw.astype(BF16), conv_pw_w.astype(BF16), w_out.astype(BF16)
    wkv, wq, wxo = xa_wkv.astype(BF16), xa_wq.astype(BF16), xa_wo.astype(BF16)
    pwq_t = jnp.swapaxes(peer_wq, 1, 2).astype(BF16)
    psk, pu = peer_subkeys.astype(BF16), peer_u.astype(BF16)
    pv_t = jnp.swapaxes(peer_v, 1, 2).astype(BF16)

    for l in range(depth):
        q, kvc, kv, misc = _inproj(x2, row(norm_mix_g[l]), w_pad[l], cos_t, sa_t, sb_t, tm=TM_PROJ)
        xc = kvc.reshape(B, n_chunk, CMP_STRIDE * 2 * LANES)
        cmp = _compress(xc, *_compress_weights(cmp_pe[l], cmp_w1[l], cmp_w2[l]))
        o_nsa = _nsa(q, cmp[0], cmp[1], kv.reshape(B, S, -1), misc, overlap, B, S, tq=TQ_NSA)
        x2 = _mixout(misc, o_nsa, x2, conv_w[l], row(conv_b[l]), row(conv_ln_g[l]), row(conv_ln_b[l]),
                     conv_pw[l], row(conv_pw_b[l]), plw[l], row(pool_scale[l]), wo[l], B, S, ts=TS_MIX)
        memkv = _norm_matmul(mem2, row(norm_mem_g[l]), wkv[l], tm=TM_MEM, out_dtype=BF16)
        x2 = _xattn(x2, row(norm_xa_g[l]), wq[l], memkv.reshape(B, M, 2 * D), wxo[l], B, S, tm=TM_PROJ)
        hn, c1, g1, r2, g2 = _peer_route(x2, row(norm_ffn_g[l]), pwq_t[l], psk[l], tm=TM_PEER)
        x2 = _peer_expert(hn, c1, g1, r2, g2, pu[l], pv_t[l], x2,
                          row(final_g) if l == depth - 1 else None, tm=TM_PEER, subs=EXPERT_SUBS)
    return x2.reshape(B, S, D)
```

```python
import functools
import math

import jax
import jax.numpy as jnp
import numpy as np
from jax import lax
from jax.experimental import pallas as pl
from jax.experimental.pallas import tpu as pltpu

F32 = jnp.float32
BF16 = jnp.bfloat16

NSA_HEADS = 8
NSA_KV_HEADS = 2
NSA_GROUP = NSA_HEADS // NSA_KV_HEADS
HEAD_DIM = 64
D_NSA = NSA_HEADS * HEAD_DIM
D_KV = NSA_KV_HEADS * HEAD_DIM
ROT_DIM = HEAD_DIM // 4
ROT_HALF = ROT_DIM // 2
ROPE_THETA = 500000.0
CMP_BLOCK = 32
CMP_STRIDE = 16
SEL_BLOCK = 64
SEL_TOPN = 16
WINDOW = 512
CONV_WIDTH = 31
POOL_WINDOWS = (2, 4, 8, 16)
XA_HEADS = 4
PEER_HEADS = 8
PEER_KEYS = 128
PEER_TOPK = 16
EPS = 1e-6
NEG = -1e30
FORCE = 1e4
LOWEST = -3.0e38

LANES = 128
SUBLANES = 8
VMEM_LIMIT = 56 * 1024 * 1024

NT_DIMS = (((1,), (1,)), ((), ()))

TM_PROJ = 512
TM_MEM = 256
TQ_NSA = 128
KC_SEL = 512
TS_MIX = 512
TM_PEER = 512
EXPERT_SUBS = (256, 768, 768, 256)


def _params(*sem):
    return pltpu.CompilerParams(dimension_semantics=sem, vmem_limit_bytes=VMEM_LIMIT)


def _full(shape):
    nd = len(shape)
    return pl.BlockSpec(shape, lambda *_: (0,) * nd)


def _rms(x, g):
    return x * lax.rsqrt(jnp.mean(x * x, axis=-1, keepdims=True) + EPS) * g


def _gelu(x):
    c = math.sqrt(2.0 / math.pi)
    return 0.5 * x * (1.0 + jnp.tanh(c * (x + 0.044715 * (x * x * x))))


def _dot(a, b):
    return jnp.dot(a, b, preferred_element_type=F32)


def _dot_nt(a, b):
    return lax.dot_general(a, b, NT_DIMS, preferred_element_type=F32)


N_ROPE_Q = D_NSA // LANES
N_KV_CHUNKS = 6
D_QKV = D_NSA + N_KV_CHUNKS * D_KV
N_KV_OUT = 2 + 2 * NSA_KV_HEADS


def _inproj_kernel(x_ref, g_ref, w_ref, cos_ref, sa_ref, sb_ref, q_ref, kvc_ref, kv_ref, misc_ref):
    y = _rms(x_ref[...], g_ref[...])
    z = _dot(y.astype(BF16), w_ref[...])
    cos, sa, sb = cos_ref[...], sa_ref[...], sb_ref[...]

    def rope(c):
        return c * cos + pltpu.roll(c, ROT_HALF, 1) * sa + pltpu.roll(c, LANES - ROT_HALF, 1) * sb

    scale = HEAD_DIM ** -0.5
    for j in range(N_ROPE_Q):
        q_ref[:, j * LANES:(j + 1) * LANES] = (rope(z[:, j * LANES:(j + 1) * LANES]) * scale).astype(BF16)
    lane = lax.broadcasted_iota(jnp.int32, (z.shape[0], LANES), 1)
    out = 0
    for j in range(N_KV_CHUNKS):
        c = z[:, D_NSA + j * LANES:D_NSA + (j + 1) * LANES]
        if j % 2 == 0:
            c = rope(c)
        if j < 2:
            kvc_ref[:, j * LANES:(j + 1) * LANES] = c.astype(BF16)
        elif j % 2 == 0:
            kv_ref[:, out * LANES:(out + 1) * LANES] = c.astype(BF16)
            out += 1
    for j in (3, 5):
        c = z[:, D_NSA + j * LANES:D_NSA + (j + 1) * LANES]
        for k in range(NSA_KV_HEADS):
            own = (lane >= HEAD_DIM) if k == 1 else (lane < HEAD_DIM)
            kv_ref[:, out * LANES:(out + 1) * LANES] = jnp.where(own, c, 1.0).astype(BF16)
            out += 1
    misc_ref[...] = z[:, D_QKV:]


def _inproj(x2, g, w_pad, cos_t, sa_t, sb_t, tm):
    T, D = x2.shape
    n = w_pad.shape[1]
    n_misc = n - D_QKV
    row = lambda w: pl.BlockSpec((tm, w), lambda i: (i, 0))
    return pl.pallas_call(
        _inproj_kernel,
        grid=(T // tm,),
        in_specs=[row(D), _full((1, D)), _full((D, n)), row(LANES), row(LANES), row(LANES)],
        out_specs=[row(D_NSA), row(2 * LANES), row(N_KV_OUT * LANES), row(n_misc)],
        out_shape=[jax.ShapeDtypeStruct((T, D_NSA), BF16),
                   jax.ShapeDtypeStruct((T, 2 * LANES), BF16),
                   jax.ShapeDtypeStruct((T, N_KV_OUT * LANES), BF16),
                   jax.ShapeDtypeStruct((T, n_misc), F32)],
        compiler_params=_params("parallel"),
    )(x2, g, w_pad, cos_t, sa_t, sb_t)


def _compress_kernel(x_ref, pe_ref, w1_ref, w2_ref, o_ref):
    x = x_ref[0].astype(F32)
    n = x.shape[0]
    out = jnp.zeros((n, LANES), F32)
    for hd in range(NSA_KV_HEADS):
        a = _dot((x + pe_ref[0, hd, 0:1]).astype(BF16), w1_ref[0, hd, 0])
        b = _dot((x + pe_ref[0, hd, 1:2]).astype(BF16), w1_ref[0, hd, 1])
        hid = _gelu(a + pltpu.roll(b, n - 1, 0))
        out = out + _dot(hid.astype(BF16), w2_ref[0, hd])
    o_ref[0, 0] = out.astype(BF16)


def _compress(xc, pe, w1, w2):
    nb, nch, width = xc.shape
    hid = w1.shape[-1]
    return pl.pallas_call(
        _compress_kernel,
        grid=(2, nb),
        in_specs=[pl.BlockSpec((1, nch, width), lambda w, i: (i, 0, 0)),
                  pl.BlockSpec((1, NSA_KV_HEADS, 2, width), lambda w, i: (w, 0, 0, 0)),
                  pl.BlockSpec((1, NSA_KV_HEADS, 2, width, hid), lambda w, i: (w, 0, 0, 0, 0)),
                  pl.BlockSpec((1, NSA_KV_HEADS, hid, LANES), lambda w, i: (w, 0, 0, 0))],
        out_specs=pl.BlockSpec((1, 1, nch, LANES), lambda w, i: (w, i, 0, 0)),
        out_shape=jax.ShapeDtypeStruct((2, nb, nch, LANES), BF16),
        compiler_params=_params("parallel", "parallel"),
    )(xc, pe, w1, w2)


def _compress_weights(pe, w1, w2):
    hid = w1.shape[-1]
    eye_t = jnp.eye(2, dtype=F32)[:, None, None, None, :, None, None]
    eye_h = jnp.eye(NSA_KV_HEADS, dtype=F32)[None, :, None, None, None, :, None]
    per = (2, 1, 2, CMP_STRIDE, 1, 1, HEAD_DIM)
    width = CMP_STRIDE * 2 * NSA_KV_HEADS * HEAD_DIM
    pe_big = (pe.reshape(per) * eye_t * eye_h).reshape(2, NSA_KV_HEADS, 2, width)
    w1_big = (w1.reshape(per + (hid,)) * eye_t[..., None] * eye_h[..., None]).reshape(2, NSA_KV_HEADS, 2, width, hid)
    eye_o = jnp.eye(NSA_KV_HEADS, dtype=F32)[None, :, None, :, None]
    w2_big = (w2[:, None, :, None, :] * eye_o).reshape(2, NSA_KV_HEADS, hid, NSA_KV_HEADS * HEAD_DIM)
    return pe_big, w1_big.astype(BF16), w2_big.astype(BF16)


def _nsa_kernel(q_ref, kcmp_ref, vcmp_ref, ks_ref, kw_ref, vs0_ref, vs1_ref, vw0_ref, vw1_ref, gate_ref, ov_ref,
                o_ref, s_sc, mrun_sc, acc_sc, *, tq, kc_sel, kc_win):
    t0 = pl.program_id(1) * tq
    rows = NSA_GROUP * tq
    lane = lax.broadcasted_iota(jnp.int32, (tq, LANES), 1)
    lane_r = lax.broadcasted_iota(jnp.int32, (rows, LANES), 1)
    tpos = t0 + lax.broadcasted_iota(jnp.int32, (tq, 1), 0)
    gates = jax.nn.sigmoid(gate_ref[...])
    n_cmp = kcmp_ref.shape[1]
    n_sel = ks_ref.shape[1] // SEL_BLOCK
    vs_refs, vw_refs = (vs0_ref, vs1_ref), (vw0_ref, vw1_ref)

    def normalise(acc, own):
        return acc / jnp.where(own, pltpu.roll(acc, HEAD_DIM, 1), 1.0)

    def own_lanes(lanes, k):
        return (lanes >= HEAD_DIM) if k == 1 else (lanes < HEAD_DIM)

    def queries(k):
        parts = []
        for g in range(NSA_GROUP):
            hh = k * NSA_GROUP + g
            c = q_ref[:, (hh // 2) * LANES:(hh // 2 + 1) * LANES].astype(F32)
            if hh % 2 != k:
                c = pltpu.roll(c, HEAD_DIM, 1)
            parts.append(jnp.where(own_lanes(lane, k), c, 0.0))
        return jnp.concatenate(parts, axis=0).astype(BF16)

    def compressed(qs):
        s = _dot_nt(qs, kcmp_ref[0]).reshape(NSA_GROUP, tq, n_cmp)
        cmp_end = lax.broadcasted_iota(jnp.int32, (tq, n_cmp), 1) * CMP_STRIDE + (CMP_BLOCK - 1)
        valid = cmp_end <= tpos
        s = s + jnp.where(valid, 0.0, NEG)[None]
        e = jnp.exp(s - jnp.max(s, axis=-1, keepdims=True)) * valid.astype(F32)[None]
        l = jnp.sum(e, axis=-1, keepdims=True)
        p3 = e * (1.0 / jnp.where(l > 0.0, l, 1.0))
        o_cmp = _dot(p3.reshape(rows, n_cmp).astype(BF16), vcmp_ref[0])

        psum = p3[0] + p3[1] + p3[2] + p3[3]
        ov = ov_ref[...]
        p_hi = psum.astype(BF16)
        r1 = psum - p_hi.astype(F32)
        p_mid = r1.astype(BF16)
        p_lo = (r1 - p_mid.astype(F32)).astype(BF16)
        imp = _dot(p_hi, ov) + _dot(p_mid, ov) + _dot(p_lo, ov)
        imp = imp.T[:n_sel]
        blk = lax.broadcasted_iota(jnp.int32, (n_sel, tq), 0)
        tpos_t = t0 + lax.broadcasted_iota(jnp.int32, (1, tq), 1)
        cur = tpos_t // SEL_BLOCK
        forced = (blk == 0) | (blk == cur) | (blk == cur - 1)
        imp = jnp.where(blk * SEL_BLOCK <= tpos_t, jnp.where(forced, FORCE, imp), NEG)
        sub8 = lax.broadcasted_iota(jnp.int32, (SUBLANES, tq), 0)
        groups = [imp[r:r + SUBLANES] for r in range(0, n_sel, SUBLANES)]
        ranks = [jnp.zeros((SUBLANES, tq), jnp.int32) for _ in groups]
        for jp in range(n_sel):
            one = imp[jp:jp + 1, :]
            for gi, grp in enumerate(groups):
                lo = gi * SUBLANES
                if lo > jp:
                    beats = one >= grp
                elif lo + SUBLANES - 1 < jp:
                    beats = one > grp
                else:
                    beats = (one > grp) | ((one == grp) & (sub8 > jp - lo))
                ranks[gi] = ranks[gi] + beats.astype(jnp.int32)
        rank = jnp.concatenate(ranks, axis=0)
        sel_t = jnp.concatenate([(rank < SEL_TOPN).astype(F32), jnp.zeros((LANES - n_sel, tq), F32)], axis=0)
        return o_cmp, sel_t.T.astype(BF16)

    def window(k, qs):
        w0 = pl.multiple_of(jnp.maximum(t0 + tq - kc_win, 0), tq)
        sw = _dot_nt(qs, kw_ref[0, pl.ds(w0, kc_win), :]).reshape(NSA_GROUP, tq, kc_win)
        wpos = w0 + lax.broadcasted_iota(jnp.int32, (tq, kc_win), 1)
        in_win = (wpos <= tpos) & (wpos > tpos - WINDOW)
        sw = (sw + jnp.where(in_win, 0.0, NEG)[None]).reshape(rows, kc_win)
        pw = jnp.exp(sw - jnp.max(sw, axis=1, keepdims=True))
        return normalise(_dot(pw.astype(BF16), vw_refs[k][0, pl.ds(w0, kc_win), :]), own_lanes(lane_r, k))

    heads = range(NSA_KV_HEADS)
    qs = [queries(k) for k in heads]
    o_cmp, sel = zip(*[compressed(qs[k]) for k in heads])
    o_win = [window(k, qs[k]) for k in heads]

    n_chunks = (t0 + tq + kc_sel - 1) // kc_sel
    mrun_sc[...] = jnp.full(mrun_sc.shape, NEG, F32)

    def scores(c, carry):
        k0 = pl.multiple_of(c * kc_sel, kc_sel)
        jrow = lax.broadcasted_iota(jnp.int32, (LANES, kc_sel), 0)
        kcol = lax.broadcasted_iota(jnp.int32, (LANES, kc_sel), 1)
        expand = (jrow == k0 // SEL_BLOCK + kcol // SEL_BLOCK).astype(BF16)
        causal = k0 + lax.broadcasted_iota(jnp.int32, (tq, kc_sel), 1) <= tpos
        kb = ks_ref[0, pl.ds(k0, kc_sel), :]
        for k in heads:
            seen = (_dot(sel[k], expand) > 0.5) & causal
            s = _dot_nt(qs[k], kb).reshape(NSA_GROUP, tq, kc_sel)
            s = (s + jnp.where(seen, 0.0, NEG)[None]).reshape(rows, kc_sel)
            s_sc[k, c] = s
            m = mrun_sc[k]
            for j in range(kc_sel // LANES):
                m = jnp.maximum(m, s[:, j * LANES:(j + 1) * LANES])
            mrun_sc[k] = m
        return carry

    lax.fori_loop(0, n_chunks, scores, 0)
    m_sel = [jnp.broadcast_to(jnp.max(mrun_sc[k], axis=1, keepdims=True), (rows, LANES)) for k in heads]
    acc_sc[...] = jnp.zeros(acc_sc.shape, F32)

    def weigh(c, carry):
        k0 = pl.multiple_of(c * kc_sel, kc_sel)
        for k in heads:
            s = s_sc[k, c]
            p = jnp.concatenate([jnp.exp(s[:, j * LANES:(j + 1) * LANES] - m_sel[k])
                                 for j in range(kc_sel // LANES)], axis=1)
            acc_sc[k] += _dot(p.astype(BF16), vs_refs[k][0, pl.ds(k0, kc_sel), :])
        return carry

    lax.fori_loop(0, n_chunks, weigh, 0)

    for k in heads:
        o_sel = normalise(acc_sc[k], own_lanes(lane_r, k))
        outs = []
        for g in range(NSA_GROUP):
            r = slice(g * tq, (g + 1) * tq)
            gi = (k * NSA_GROUP + g) * 3
            og = (gates[:, gi:gi + 1] * o_cmp[k][r] + gates[:, gi + 1:gi + 2] * o_sel[r]
                  + gates[:, gi + 2:gi + 3] * o_win[k][r])
            if g % 2 != k:
                og = pltpu.roll(og, HEAD_DIM, 1)
            outs.append(og)
        for j in range(NSA_GROUP // 2):
            chunk = jnp.where(lane < HEAD_DIM, outs[2 * j], outs[2 * j + 1])
            cj = k * (NSA_GROUP // 2) + j
            o_ref[:, cj * LANES:(cj + 1) * LANES] = chunk.astype(BF16)


def _nsa(q, kcmp, vcmp, kv, gates_misc, overlap, B, S, tq):
    kc_sel, kc_win = KC_SEL, WINDOW + tq
    nq = S // tq
    n_cmp = kcmp.shape[1]
    rows = NSA_GROUP * tq
    kvspec = lambda j: pl.BlockSpec((1, S, LANES), lambda b, i: (b, 0, j))
    return pl.pallas_call(
        functools.partial(_nsa_kernel, tq=tq, kc_sel=kc_sel, kc_win=kc_win),
        grid=(B, nq),
        in_specs=[pl.BlockSpec((tq, D_NSA), lambda b, i: (b * nq + i, 0)),
                  pl.BlockSpec((1, n_cmp, LANES), lambda b, i: (b, 0, 0)),
                  pl.BlockSpec((1, n_cmp, LANES), lambda b, i: (b, 0, 0)),
                  kvspec(0), kvspec(1), kvspec(2), kvspec(3), kvspec(4), kvspec(5),
                  pl.BlockSpec((tq, LANES), lambda b, i: (b * nq + i, 0)),
                  _full(overlap.shape)],
        out_specs=pl.BlockSpec((tq, D_NSA), lambda b, i: (b * nq + i, 0)),
        out_shape=jax.ShapeDtypeStruct((B * S, D_NSA), BF16),
        scratch_shapes=[pltpu.VMEM((NSA_KV_HEADS, S // kc_sel, rows, kc_sel), F32),
                        pltpu.VMEM((NSA_KV_HEADS, rows, LANES), F32), pltpu.VMEM((NSA_KV_HEADS, rows, LANES), F32)],
        compiler_params=_params("parallel", "parallel"),
    )(q, kcmp, vcmp, kv, kv, kv, kv, kv, kv, gates_misc, overlap)


HALO = 32


def _mixout_kernel(mc_ref, mp_ref, on_ref, x_ref, cw_ref, cb_ref, lg_ref, lb_ref, pw_ref, pb_ref, plw_ref,
                   pls_ref, wo_ref, o_ref, hbuf, pbuf, hsh, *, ts, dc):
    i = pl.program_id(1)
    first = i == 0
    off_a, off_b, off_p = LANES, LANES + dc, LANES + 2 * dc

    def glu(ref, r):
        return ref[r, off_a:off_a + dc] * jax.nn.sigmoid(ref[r, off_b:off_b + dc])

    tail = slice(ts - HALO, ts)
    hbuf[0:HALO, :] = jnp.where(first, 0.0, glu(mp_ref, tail))
    hbuf[HALO:, :] = glu(mc_ref, slice(None))
    pbuf[0:HALO, :] = jnp.where(first, 0.0, mp_ref[tail, off_p:off_p + dc])
    pcur = mc_ref[:, off_p:off_p + dc]
    pbuf[HALO:, :] = pcur

    shifted = ts + HALO - SUBLANES
    for r in range(1, SUBLANES):
        hsh[r, 0:shifted, :] = hbuf[pl.ds(r, shifted), :]
    acc = jnp.zeros((ts, dc), F32) + cb_ref[...]
    for w in range(CONV_WIDTH):
        q, r = divmod(HALO - (CONV_WIDTH - 1) + w, SUBLANES)
        tap = hbuf[pl.ds(q * SUBLANES, ts), :] if r == 0 else hsh[r, pl.ds(q * SUBLANES, ts), :]
        acc = acc + tap * cw_ref[w:w + 1, :]
    mu = jnp.mean(acc, axis=-1, keepdims=True)
    xc = acc - mu
    y = xc * lax.rsqrt(jnp.mean(xc * xc, axis=-1, keepdims=True) + EPS) * lg_ref[...] + lb_ref[...]
    y = y * jax.nn.sigmoid(y)
    o_conv = _dot(y.astype(BF16), pw_ref[...]) + pb_ref[...]

    tglob = i * ts + lax.broadcasted_iota(jnp.int32, (ts, dc), 0)
    lane = lax.broadcasted_iota(jnp.int32, (ts, dc), 1)
    pg = dc // len(POOL_WINDOWS)
    run = pcur
    d = 1
    mean = jnp.zeros((ts, dc), F32)
    for gi, w in enumerate(POOL_WINDOWS):
        while d < w:
            run = run + pbuf[pl.ds(HALO - d, ts), :]
            d += 1
        cnt = jnp.minimum(tglob + 1, w).astype(F32)
        mean = jnp.where(lane // pg == gi, run / cnt, mean)
    o_pool = _dot((mean - pcur).astype(BF16), plw_ref[...]) * pls_ref[...]

    dn = on_ref.shape[1]
    o_ref[...] = (x_ref[...] + _dot(on_ref[...], wo_ref[0:dn, :])
                  + _dot(o_conv.astype(BF16), wo_ref[dn:dn + dc, :])
                  + _dot(o_pool.astype(BF16), wo_ref[dn + dc:, :]))


def _mixout(misc, o_nsa, x2, cw, cb, lg, lb, pw, pb, plw, pls, wo, B, S, ts):
    T, D = x2.shape
    dc = cw.shape[1]
    ns = S // ts
    nm = misc.shape[1]
    cur = lambda w: pl.BlockSpec((ts, w), lambda b, i: (b * ns + i, 0))
    prev = pl.BlockSpec((ts, nm), lambda b, i: (b * ns + jnp.maximum(i - 1, 0), 0))
    return pl.pallas_call(
        functools.partial(_mixout_kernel, ts=ts, dc=dc),
        grid=(B, ns),
        in_specs=[cur(nm), prev, cur(o_nsa.shape[1]), cur(D), _full(cw.shape), _full(cb.shape), _full(lg.shape),
                  _full(lb.shape), _full(pw.shape), _full(pb.shape), _full(plw.shape), _full(pls.shape),
                  _full(wo.shape)],
        out_specs=cur(D),
        out_shape=jax.ShapeDtypeStruct((T, D), F32),
        scratch_shapes=[pltpu.VMEM((ts + HALO, dc), F32), pltpu.VMEM((ts + HALO, dc), F32),
                        pltpu.VMEM((SUBLANES, ts + HALO, dc), F32)],
        compiler_params=_params("parallel", "parallel"),
    )(misc, misc, o_nsa, x2, cw, cb, lg, lb, pw, pb, plw, pls, wo)


def _norm_matmul_kernel(x_ref, g_ref, w_ref, o_ref):
    o_ref[...] = _dot(_rms(x_ref[...], g_ref[...]).astype(BF16), w_ref[...]).astype(o_ref.dtype)


def _norm_matmul(x2, g, w, tm, out_dtype):
    T, D = x2.shape
    n = w.shape[1]
    return pl.pallas_call(
        _norm_matmul_kernel,
        grid=(T // tm,),
        in_specs=[pl.BlockSpec((tm, D), lambda i: (i, 0)), _full((1, D)), _full((D, n))],
        out_specs=pl.BlockSpec((tm, n), lambda i: (i, 0)),
        out_shape=jax.ShapeDtypeStruct((T, n), out_dtype),
        compiler_params=_params("parallel"),
    )(x2, g, w)


def _xattn_kernel(x_ref, g_ref, wq_ref, k_ref, v_ref, wo_ref, o_ref):
    x = x_ref[...]
    D = x.shape[1]
    dh = D // XA_HEADS
    q = _dot(_rms(x, g_ref[...]).astype(BF16), wq_ref[...]) * (dh ** -0.5)
    outs = []
    for h in range(XA_HEADS):
        c = slice(h * dh, (h + 1) * dh)
        s = _dot_nt(q[:, c].astype(BF16), k_ref[0, :, c])
        e = jnp.exp(s - jnp.max(s, axis=-1, keepdims=True))
        p = e * (1.0 / jnp.sum(e, axis=-1, keepdims=True))
        outs.append(_dot(p.astype(BF16), v_ref[0, :, c]))
    o = jnp.concatenate(outs, axis=1)
    o_ref[...] = x + _dot(o.astype(BF16), wo_ref[...])


def _xattn(x2, g, wq, memkv, wo, B, S, tm):
    T, D = x2.shape
    ns = S // tm
    M = memkv.shape[1]
    return pl.pallas_call(
        _xattn_kernel,
        grid=(B, ns),
        in_specs=[pl.BlockSpec((tm, D), lambda b, i: (b * ns + i, 0)), _full((1, D)), _full((D, D)),
                  pl.BlockSpec((1, M, D), lambda b, i: (b, 0, 0)),
                  pl.BlockSpec((1, M, D), lambda b, i: (b, 0, 1)),
                  _full((D, D))],
        out_specs=pl.BlockSpec((tm, D), lambda b, i: (b * ns + i, 0)),
        out_shape=jax.ShapeDtypeStruct((T, D), F32),
        compiler_params=_params("parallel", "parallel"),
    )(x2, g, wq, memkv, memkv, wo)


PAIR_LIST = tuple((a, b) for a in range(PEER_TOPK) for b in range(PEER_TOPK) if (a + 1) * (b + 1) <= PEER_TOPK)
N_PAIR_ROWS = -(-len(PAIR_LIST) // 8) * 8
PAIR_COUNT = tuple(PEER_TOPK // (a + 1) for a in range(PEER_TOPK))
PAIR_START = tuple(sum(PAIR_COUNT[:a]) for a in range(PEER_TOPK))


def _top_rows_exact(v, n):
    R = v.shape[0]
    ridx = lax.broadcasted_iota(jnp.int32, v.shape, 0).astype(F32)
    rank = jnp.full(v.shape, float(n), F32)
    vals = []
    for r in range(n):
        m = jnp.max(v, axis=0, keepdims=True)
        first = jnp.min(jnp.where(v == m, ridx, float(R)), axis=0, keepdims=True)
        taken = ridx == first
        v = jnp.where(taken, LOWEST, v)
        rank = jnp.where(taken, float(r), rank)
        vals.append(m)
    return vals, rank


def _peer_route_kernel(x_ref, g_ref, wqt_ref, sk_ref, hn_ref, c1_ref, g1_ref, r2_ref, g2_ref, qt_sc, s_sc, top_sc,
                       rank_sc, cand_sc, cw_sc):
    hn = _rms(x_ref[...], g_ref[...]).astype(BF16)
    hn_ref[...] = hn
    qt_sc[...] = _dot_nt(wqt_ref[...], hn).astype(BF16)
    nk, n = PEER_KEYS, PEER_TOPK
    n_lists = 2 * PEER_HEADS

    def one_list(li, carry):
        r0 = pl.multiple_of(li * nk, nk)
        s = _dot(sk_ref[li % 2], qt_sc[pl.ds(r0, nk), :])
        s_sc[pl.ds(r0, nk), :] = s
        vals, rank = _top_rows_exact(s, n)
        for r in range(n):
            top_sc[pl.ds(li * n + r, 1), :] = vals[r]
        rank_sc[pl.ds(r0, nk), :] = rank
        return carry

    lax.fori_loop(0, n_lists, one_list, 0)

    cand_sc[...] = jnp.full(cand_sc.shape, LOWEST, F32)
    cw_sc[...] = jnp.zeros(cw_sc.shape, F32)
    for h in range(PEER_HEADS):
        l1, l2 = 2 * h, 2 * h + 1
        top1 = [top_sc[l1 * n + a:l1 * n + a + 1, :] for a in range(n)]
        top2 = [top_sc[l2 * n + a:l2 * n + a + 1, :] for a in range(n)]
        e1 = [jnp.exp(t - top1[0]) for t in top1]
        e2 = [jnp.exp(t - top2[0]) for t in top2]
        for r, (a, b) in enumerate(PAIR_LIST):
            cand_sc[h, r:r + 1, :] = top1[a] + top2[b]
            cw_sc[h, r:r + 1, :] = e1[a] * e2[b]
        _, crank = _top_rows_exact(cand_sc[h], n)
        chosen = (crank < float(n)).astype(F32)
        z = jnp.sum(chosen * cw_sc[h], axis=0, keepdims=True)
        rank1 = rank_sc[l1 * nk:(l1 + 1) * nk, :]
        count1 = jnp.zeros(rank1.shape, F32)
        for a in range(n):
            n_a = jnp.sum(chosen[PAIR_START[a]:PAIR_START[a] + PAIR_COUNT[a]], axis=0, keepdims=True)
            count1 = jnp.where(rank1 == float(a), n_a, count1)
        rows = slice(h * nk, (h + 1) * nk)
        c1_ref[rows, :] = count1
        g1_ref[rows, :] = jnp.exp(s_sc[l1 * nk:(l1 + 1) * nk, :] - top1[0]) * (1.0 / z)
        r2_ref[rows, :] = rank_sc[l2 * nk:(l2 + 1) * nk, :].astype(BF16)
        g2_ref[rows, :] = jnp.exp(s_sc[l2 * nk:(l2 + 1) * nk, :] - top2[0]).astype(BF16)


def _peer_route(x2, g, wqt, sk, tm):
    T, D = x2.shape
    nr = PEER_HEADS * PEER_KEYS
    col = pl.BlockSpec((nr, tm), lambda i: (0, i))
    return pl.pallas_call(
        _peer_route_kernel,
        grid=(T // tm,),
        in_specs=[pl.BlockSpec((tm, D), lambda i: (i, 0)), _full((1, D)), _full(wqt.shape), _full(sk.shape)],
        out_specs=[pl.BlockSpec((tm, D), lambda i: (i, 0)), col, col, col, col],
        out_shape=[jax.ShapeDtypeStruct((T, D), BF16), jax.ShapeDtypeStruct((nr, T), F32),
                   jax.ShapeDtypeStruct((nr, T), F32), jax.ShapeDtypeStruct((nr, T), BF16),
                   jax.ShapeDtypeStruct((nr, T), BF16)],
        scratch_shapes=[pltpu.VMEM((2 * nr, tm), BF16), pltpu.VMEM((2 * nr, tm), F32),
                        pltpu.VMEM((2 * PEER_HEADS * PEER_TOPK, tm), F32),
                        pltpu.VMEM((2 * nr, tm), F32), pltpu.VMEM((PEER_HEADS, N_PAIR_ROWS, tm), F32),
                        pltpu.VMEM((PEER_HEADS, N_PAIR_ROWS, tm), F32)],
        compiler_params=_params("parallel"),
    )(x2, g, wqt, sk)


def _peer_expert_kernel(*refs, subs, final_norm):
    hn_ref, c1_ref, g1_ref, r2_ref, g2_ref, u_ref, vt_ref, x_ref = refs[:8]
    fg_ref = refs[8] if final_norm else None
    o_ref, acc_sc, wa_sc, act_sc = refs[-4:]
    c = pl.program_id(1)
    nk = PEER_KEYS
    te = sum(subs)
    offs = [sum(subs[:j]) for j in range(len(subs))]
    tm = hn_ref.shape[0]

    @pl.when(c == 0)
    def _():
        acc_sc[...] = jnp.zeros(acc_sc.shape, F32)

    def pre_act(j):
        o, rows = offs[j], subs[j]
        act_sc[j % 2, 0:rows, :] = _dot_nt(u_ref[o:o + rows, :], hn_ref[...]).astype(BF16)

    pre_act(0)
    for j, (o, rows) in enumerate(zip(offs, subs)):
        if j + 1 < len(subs):
            pre_act(j + 1)
        for il in range(rows // nk):
            i = c * (te // nk) + o // nk + il
            w = jnp.zeros((nk, tm), BF16)
            for h in range(PEER_HEADS):
                count = c1_ref[pl.ds(h * nk + i, 1), :].astype(BF16)
                g1row = g1_ref[pl.ds(h * nk + i, 1), :].astype(BF16)
                hr = slice(h * nk, (h + 1) * nk)
                w = w + jnp.where(r2_ref[hr, :] < count, g2_ref[hr, :], 0.0) * g1row
            er = slice(il * nk, (il + 1) * nk)
            wa_sc[o + il * nk:o + (il + 1) * nk, :] = w * _gelu(act_sc[j % 2, er, :])
        acc_sc[...] += _dot(vt_ref[:, o:o + rows], wa_sc[o:o + rows, :])

    @pl.when(c == pl.num_programs(1) - 1)
    def _():
        out = x_ref[...] + acc_sc[...].T
        o_ref[...] = _rms(out, fg_ref[...]) if final_norm else out


def _peer_expert(hn, c1, g1, r2, g2, u, vt, layer, x2, final_g, tm, subs):
    T, D = x2.shape
    ne = u.shape[1]
    te = sum(subs)
    tok = pl.BlockSpec((c1.shape[0], tm), lambda i, c: (0, i))
    row = pl.BlockSpec((tm, D), lambda i, c: (i, 0))
    return pl.pallas_call(
        functools.partial(_peer_expert_kernel, subs=subs, final_norm=final_g is not None),
        grid=(T // tm, ne // te),
        in_specs=[row, tok, tok, tok, tok,
                  pl.BlockSpec((None, te, D), lambda i, c: (layer, c, 0)),
                  pl.BlockSpec((None, D, te), lambda i, c: (layer, 0, c)),
                  row] + ([] if final_g is None else [_full((1, D))]),
        out_specs=row,
        out_shape=jax.ShapeDtypeStruct((T, D), F32),
        scratch_shapes=[pltpu.VMEM((D, tm), F32), pltpu.VMEM((te, tm), BF16),
                        pltpu.VMEM((2, max(subs), tm), BF16)],
        compiler_params=_params("parallel", "arbitrary"),
    )(hn, c1, g1, r2, g2, u, vt, x2, *([] if final_g is None else [final_g]))


def _rope_tables(positions):
    B, S = positions.shape
    freqs = ROPE_THETA ** (-jnp.arange(ROT_HALF, dtype=F32) * 2.0 / ROT_DIM)
    ang = positions.astype(F32)[:, :, None] * freqs
    cos, sin = jnp.cos(ang), jnp.sin(ang)
    ones = jnp.ones((B, S, HEAD_DIM - ROT_DIM), F32)
    zeros8 = jnp.zeros((B, S, ROT_HALF), F32)
    zeros = jnp.zeros((B, S, HEAD_DIM - ROT_DIM), F32)
    cos_h = jnp.concatenate([cos, cos, ones], axis=-1)
    sa_h = jnp.concatenate([zeros8, sin, zeros], axis=-1)
    sb_h = jnp.concatenate([-sin, zeros8, zeros], axis=-1)
    rep = LANES // HEAD_DIM
    tile = lambda t: jnp.tile(t, (1, 1, rep)).reshape(B * S, LANES)
    return tile(cos_h), tile(sa_h), tile(sb_h)


def _overlap_matrix(n_chunk, n_sel):
    ci = np.arange(n_chunk)[:, None] * CMP_STRIDE
    sj = np.arange(LANES)[None, :] * SEL_BLOCK
    ov = (ci < sj + SEL_BLOCK) & (ci + CMP_BLOCK > sj) & (np.arange(LANES)[None, :] < n_sel)
    ov = ov & (np.arange(n_chunk)[:, None] < n_chunk - 1)
    return jnp.asarray(ov, BF16)


def kernel(x, mem, positions, norm_mix_g, w_in, cmp_pe, cmp_w1, cmp_w2, conv_w, conv_b, conv_ln_g, conv_ln_b, conv_pw_w, conv_pw_b, pool_w, pool_scale, w_out, norm_xa_g, norm_mem_g, xa_wq, xa_wkv, xa_wo, norm_ffn_g, peer_wq, peer_subkeys, peer_u, peer_v, final_g):
    B, S, D = x.shape
    T = B * S
    depth = w_in.shape[0]
    M = mem.shape[1]
    dc = conv_w.shape[-1]
    n_chunk = S // CMP_STRIDE
    n_sel = S // SEL_BLOCK
    assert S % max(TM_PROJ, TS_MIX, KC_SEL, TM_PEER) == 0 and SEL_TOPN <= n_sel <= LANES and D % LANES == 0
    assert S >= WINDOW + TQ_NSA and peer_u.shape[1] % sum(EXPERT_SUBS) == 0

    cos_t, sa_t, sb_t = _rope_tables(positions)
    overlap = _overlap_matrix(n_chunk, n_sel)
    row = lambda v: v.reshape(1, -1)
    x2 = x.reshape(T, D)
    mem2 = mem.reshape(B * M, D)
    n_gate = 3 * NSA_HEADS

    w_pad = jnp.concatenate([w_in[:, :, :D_QKV + n_gate], jnp.zeros((depth, D, LANES - n_gate), F32),
                             w_in[:, :, D_QKV + n_gate:]], axis=2).astype(BF16)
    pg = dc // len(POOL_WINDOWS)
    plw = jnp.zeros((depth, dc, dc), F32)
    for gi in range(len(POOL_WINDOWS)):
        plw = plw.at[:, gi * pg:(gi + 1) * pg, gi * pg:(gi + 1) * pg].set(pool_w[:, gi])
    plw, conv_pw, wo = plw.astype(BF16), conv_pw_w.astype(BF16), w_out.astype(BF16)
    wkv, wq, wxo = xa_wkv.astype(BF16), xa_wq.astype(BF16), xa_wo.astype(BF16)
    pwq_t = jnp.swapaxes(peer_wq, 1, 2).astype(BF16)
    psk, pu = peer_subkeys.astype(BF16), peer_u.astype(BF16)
    pv_t = jnp.swapaxes(peer_v, 1, 2).astype(BF16)

    for l in range(depth):
        q, kvc, kv, misc = _inproj(x2, row(norm_mix_g[l]), w_pad[l], cos_t, sa_t, sb_t, tm=TM_PROJ)
        xc = kvc.reshape(B, n_chunk, CMP_STRIDE * 2 * LANES)
        cmp = _compress(xc, *_compress_weights(cmp_pe[l], cmp_w1[l], cmp_w2[l]))
        o_nsa = _nsa(q, cmp[0], cmp[1], kv.reshape(B, S, -1), misc, overlap, B, S, tq=TQ_NSA)
        x2 = _mixout(misc, o_nsa, x2, conv_w[l], row(conv_b[l]), row(conv_ln_g[l]), row(conv_ln_b[l]),
                     conv_pw[l], row(conv_pw_b[l]), plw[l], row(pool_scale[l]), wo[l], B, S, ts=TS_MIX)
        memkv = _norm_matmul(mem2, row(norm_mem_g[l]), wkv[l], tm=TM_MEM, out_dtype=BF16)
        x2 = _xattn(x2, row(norm_xa_g[l]), wq[l], memkv.reshape(B, M, 2 * D), wxo[l], B, S, tm=TM_PROJ)
        hn, c1, g1, r2, g2 = _peer_route(x2, row(norm_ffn_g[l]), pwq_t[l], psk[l], tm=TM_PEER)
        x2 = _peer_expert(hn, c1, g1, r2, g2, pu, pv_t, l, x2,
                          row(final_g) if l == depth - 1 else None, tm=TM_PEER, subs=EXPERT_SUBS)
    return x2.reshape(B, S, D)
```

```python
import functools
import math

import jax
import jax.numpy as jnp
import numpy as np
from jax import lax
from jax.experimental import pallas as pl
from jax.experimental.pallas import tpu as pltpu

F32 = jnp.float32
BF16 = jnp.bfloat16

NSA_HEADS = 8
NSA_KV_HEADS = 2
NSA_GROUP = NSA_HEADS // NSA_KV_HEADS
HEAD_DIM = 64
D_NSA = NSA_HEADS * HEAD_DIM
D_KV = NSA_KV_HEADS * HEAD_DIM
ROT_DIM = HEAD_DIM // 4
ROT_HALF = ROT_DIM // 2
ROPE_THETA = 500000.0
CMP_BLOCK = 32
CMP_STRIDE = 16
SEL_BLOCK = 64
SEL_TOPN = 16
WINDOW = 512
CONV_WIDTH = 31
POOL_WINDOWS = (2, 4, 8, 16)
XA_HEADS = 4
PEER_HEADS = 8
PEER_KEYS = 128
PEER_TOPK = 16
EPS = 1e-6
NEG = -1e30
FORCE = 1e4
LOWEST = -3.0e38

LANES = 128
SUBLANES = 8
VMEM_LIMIT = 56 * 1024 * 1024

NT_DIMS = (((1,), (1,)), ((), ()))

TM_PROJ = 512
TM_MEM = 256
TQ_NSA = 128
KC_SEL = 512
TS_MIX = 512
TM_PEER = 512
EXPERT_SUBS = (256, 768, 768, 256)


def _params(*sem):
    return pltpu.CompilerParams(dimension_semantics=sem, vmem_limit_bytes=VMEM_LIMIT)


def _full(shape):
    nd = len(shape)
    return pl.BlockSpec(shape, lambda *_: (0,) * nd)


def _rms(x, g):
    return x * lax.rsqrt(jnp.mean(x * x, axis=-1, keepdims=True) + EPS) * g


def _gelu(x):
    c = math.sqrt(2.0 / math.pi)
    return 0.5 * x * (1.0 + jnp.tanh(c * (x + 0.044715 * (x * x * x))))


def _dot(a, b):
    return jnp.dot(a, b, preferred_element_type=F32)


def _dot_nt(a, b):
    return lax.dot_general(a, b, NT_DIMS, preferred_element_type=F32)


N_ROPE_Q = D_NSA // LANES
N_KV_CHUNKS = 6
D_QKV = D_NSA + N_KV_CHUNKS * D_KV
N_KV_OUT = 2 + 2 * NSA_KV_HEADS


def _inproj_kernel(x_ref, g_ref, w_ref, cos_ref, sa_ref, sb_ref, q_ref, kvc_ref, kv_ref, misc_ref):
    y = _rms(x_ref[...], g_ref[...])
    z = _dot(y.astype(BF16), w_ref[...])
    cos, sa, sb = cos_ref[...], sa_ref[...], sb_ref[...]

    def rope(c):
        return c * cos + pltpu.roll(c, ROT_HALF, 1) * sa + pltpu.roll(c, LANES - ROT_HALF, 1) * sb

    scale = HEAD_DIM ** -0.5
    for j in range(N_ROPE_Q):
        q_ref[:, j * LANES:(j + 1) * LANES] = (rope(z[:, j * LANES:(j + 1) * LANES]) * scale).astype(BF16)
    lane = lax.broadcasted_iota(jnp.int32, (z.shape[0], LANES), 1)
    out = 0
    for j in range(N_KV_CHUNKS):
        c = z[:, D_NSA + j * LANES:D_NSA + (j + 1) * LANES]
        if j % 2 == 0:
            c = rope(c)
        if j < 2:
            kvc_ref[:, j * LANES:(j + 1) * LANES] = c.astype(BF16)
        elif j % 2 == 0:
            kv_ref[:, out * LANES:(out + 1) * LANES] = c.astype(BF16)
            out += 1
    for j in (3, 5):
        c = z[:, D_NSA + j * LANES:D_NSA + (j + 1) * LANES]
        for k in range(NSA_KV_HEADS):
            own = (lane >= HEAD_DIM) if k == 1 else (lane < HEAD_DIM)
            kv_ref[:, out * LANES:(out + 1) * LANES] = jnp.where(own, c, 1.0).astype(BF16)
            out += 1
    misc_ref[...] = z[:, D_QKV:]


def _inproj(x2, g, w_pad, cos_t, sa_t, sb_t, tm):
    T, D = x2.shape
    n = w_pad.shape[1]
    n_misc = n - D_QKV
    row = lambda w: pl.BlockSpec((tm, w), lambda i: (i, 0))
    return pl.pallas_call(
        _inproj_kernel,
        grid=(T // tm,),
        in_specs=[row(D), _full((1, D)), _full((D, n)), row(LANES), row(LANES), row(LANES)],
        out_specs=[row(D_NSA), row(2 * LANES), row(N_KV_OUT * LANES), row(n_misc)],
        out_shape=[jax.ShapeDtypeStruct((T, D_NSA), BF16),
                   jax.ShapeDtypeStruct((T, 2 * LANES), BF16),
                   jax.ShapeDtypeStruct((T, N_KV_OUT * LANES), BF16),
                   jax.ShapeDtypeStruct((T, n_misc), F32)],
        compiler_params=_params("parallel"),
    )(x2, g, w_pad, cos_t, sa_t, sb_t)


def _compress_kernel(x_ref, pe_ref, w1_ref, w2_ref, o_ref):
    x = x_ref[0].astype(F32)
    n = x.shape[0]
    out = jnp.zeros((n, LANES), F32)
    for hd in range(NSA_KV_HEADS):
        a = _dot((x + pe_ref[0, hd, 0:1]).astype(BF16), w1_ref[0, hd, 0])
        b = _dot((x + pe_ref[0, hd, 1:2]).astype(BF16), w1_ref[0, hd, 1])
        hid = _gelu(a + pltpu.roll(b, n - 1, 0))
        out = out + _dot(hid.astype(BF16), w2_ref[0, hd])
    o_ref[0, 0] = out.astype(BF16)


def _compress(xc, pe, w1, w2):
    nb, nch, width = xc.shape
    hid = w1.shape[-1]
    return pl.pallas_call(
        _compress_kernel,
        grid=(2, nb),
        in_specs=[pl.BlockSpec((1, nch, width), lambda w, i: (i, 0, 0)),
                  pl.BlockSpec((1, NSA_KV_HEADS, 2, width), lambda w, i: (w, 0, 0, 0)),
                  pl.BlockSpec((1, NSA_KV_HEADS, 2, width, hid), lambda w, i: (w, 0, 0, 0, 0)),
                  pl.BlockSpec((1, NSA_KV_HEADS, hid, LANES), lambda w, i: (w, 0, 0, 0))],
        out_specs=pl.BlockSpec((1, 1, nch, LANES), lambda w, i: (w, i, 0, 0)),
        out_shape=jax.ShapeDtypeStruct((2, nb, nch, LANES), BF16),
        compiler_params=_params("parallel", "parallel"),
    )(xc, pe, w1, w2)


def _compress_weights(pe, w1, w2):
    hid = w1.shape[-1]
    eye_t = jnp.eye(2, dtype=F32)[:, None, None, None, :, None, None]
    eye_h = jnp.eye(NSA_KV_HEADS, dtype=F32)[None, :, None, None, None, :, None]
    per = (2, 1, 2, CMP_STRIDE, 1, 1, HEAD_DIM)
    width = CMP_STRIDE * 2 * NSA_KV_HEADS * HEAD_DIM
    pe_big = (pe.reshape(per) * eye_t * eye_h).reshape(2, NSA_KV_HEADS, 2, width)
    w1_big = (w1.reshape(per + (hid,)) * eye_t[..., None] * eye_h[..., None]).reshape(2, NSA_KV_HEADS, 2, width, hid)
    eye_o = jnp.eye(NSA_KV_HEADS, dtype=F32)[None, :, None, :, None]
    w2_big = (w2[:, None, :, None, :] * eye_o).reshape(2, NSA_KV_HEADS, hid, NSA_KV_HEADS * HEAD_DIM)
    return pe_big, w1_big.astype(BF16), w2_big.astype(BF16)


def _nsa_kernel(q_ref, kcmp_ref, vcmp_ref, ks_ref, kw_ref, vs0_ref, vs1_ref, vw0_ref, vw1_ref, gate_ref, ov_ref,
                o_ref, s_sc, mrun_sc, acc_sc, *, tq, kc_sel, kc_win):
    t0 = pl.program_id(1) * tq
    rows = NSA_GROUP * tq
    lane = lax.broadcasted_iota(jnp.int32, (tq, LANES), 1)
    lane_r = lax.broadcasted_iota(jnp.int32, (rows, LANES), 1)
    tpos = t0 + lax.broadcasted_iota(jnp.int32, (tq, 1), 0)
    gates = jax.nn.sigmoid(gate_ref[...])
    n_cmp = kcmp_ref.shape[1]
    n_sel = ks_ref.shape[1] // SEL_BLOCK
    vs_refs, vw_refs = (vs0_ref, vs1_ref), (vw0_ref, vw1_ref)

    def normalise(acc, own):
        return acc / jnp.where(own, pltpu.roll(acc, HEAD_DIM, 1), 1.0)

    def own_lanes(lanes, k):
        return (lanes >= HEAD_DIM) if k == 1 else (lanes < HEAD_DIM)

    def queries(k):
        parts = []
        for g in range(NSA_GROUP):
            hh = k * NSA_GROUP + g
            c = q_ref[:, (hh // 2) * LANES:(hh // 2 + 1) * LANES].astype(F32)
            if hh % 2 != k:
                c = pltpu.roll(c, HEAD_DIM, 1)
            parts.append(jnp.where(own_lanes(lane, k), c, 0.0))
        return jnp.concatenate(parts, axis=0).astype(BF16)

    def compressed(qs):
        s = _dot_nt(qs, kcmp_ref[0]).reshape(NSA_GROUP, tq, n_cmp)
        cmp_end = lax.broadcasted_iota(jnp.int32, (tq, n_cmp), 1) * CMP_STRIDE + (CMP_BLOCK - 1)
        valid = cmp_end <= tpos
        s = s + jnp.where(valid, 0.0, NEG)[None]
        e = jnp.exp(s - jnp.max(s, axis=-1, keepdims=True)) * valid.astype(F32)[None]
        l = jnp.sum(e, axis=-1, keepdims=True)
        p3 = e * (1.0 / jnp.where(l > 0.0, l, 1.0))
        o_cmp = _dot(p3.reshape(rows, n_cmp).astype(BF16), vcmp_ref[0])

        psum = p3[0] + p3[1] + p3[2] + p3[3]
        ov = ov_ref[...]
        p_hi = psum.astype(BF16)
        r1 = psum - p_hi.astype(F32)
        p_mid = r1.astype(BF16)
        p_lo = (r1 - p_mid.astype(F32)).astype(BF16)
        imp = _dot(p_hi, ov) + _dot(p_mid, ov) + _dot(p_lo, ov)
        imp = imp.T[:n_sel]
        blk = lax.broadcasted_iota(jnp.int32, (n_sel, tq), 0)
        tpos_t = t0 + lax.broadcasted_iota(jnp.int32, (1, tq), 1)
        cur = tpos_t // SEL_BLOCK
        forced = (blk == 0) | (blk == cur) | (blk == cur - 1)
        imp = jnp.where(blk * SEL_BLOCK <= tpos_t, jnp.where(forced, FORCE, imp), NEG)
        sub8 = lax.broadcasted_iota(jnp.int32, (SUBLANES, tq), 0)
        groups = [imp[r:r + SUBLANES] for r in range(0, n_sel, SUBLANES)]
        ranks = [jnp.zeros((SUBLANES, tq), jnp.int32) for _ in groups]
        for jp in range(n_sel):
            one = imp[jp:jp + 1, :]
            for gi, grp in enumerate(groups):
                lo = gi * SUBLANES
                if lo > jp:
                    beats = one >= grp
                elif lo + SUBLANES - 1 < jp:
                    beats = one > grp
                else:
                    beats = (one > grp) | ((one == grp) & (sub8 > jp - lo))
                ranks[gi] = ranks[gi] + beats.astype(jnp.int32)
        rank = jnp.concatenate(ranks, axis=0)
        sel_t = jnp.concatenate([(rank < SEL_TOPN).astype(F32), jnp.zeros((LANES - n_sel, tq), F32)], axis=0)
        return o_cmp, sel_t.T.astype(BF16)

    def window(k, qs):
        w0 = pl.multiple_of(jnp.maximum(t0 + tq - kc_win, 0), tq)
        sw = _dot_nt(qs, kw_ref[0, pl.ds(w0, kc_win), :]).reshape(NSA_GROUP, tq, kc_win)
        wpos = w0 + lax.broadcasted_iota(jnp.int32, (tq, kc_win), 1)
        in_win = (wpos <= tpos) & (wpos > tpos - WINDOW)
        sw = (sw + jnp.where(in_win, 0.0, NEG)[None]).reshape(rows, kc_win)
        pw = jnp.exp(sw - jnp.max(sw, axis=1, keepdims=True))
        return normalise(_dot(pw.astype(BF16), vw_refs[k][0, pl.ds(w0, kc_win), :]), own_lanes(lane_r, k))

    heads = range(NSA_KV_HEADS)
    qs = [queries(k) for k in heads]
    o_cmp, sel = zip(*[compressed(qs[k]) for k in heads])
    o_win = [window(k, qs[k]) for k in heads]

    n_chunks = (t0 + tq + kc_sel - 1) // kc_sel
    mrun_sc[...] = jnp.full(mrun_sc.shape, NEG, F32)

    def scores(c, carry):
        k0 = pl.multiple_of(c * kc_sel, kc_sel)
        jrow = lax.broadcasted_iota(jnp.int32, (LANES, kc_sel), 0)
        kcol = lax.broadcasted_iota(jnp.int32, (LANES, kc_sel), 1)
        expand = (jrow == k0 // SEL_BLOCK + kcol // SEL_BLOCK).astype(BF16)
        causal = k0 + lax.broadcasted_iota(jnp.int32, (tq, kc_sel), 1) <= tpos
        kb = ks_ref[0, pl.ds(k0, kc_sel), :]
        for k in heads:
            seen = (_dot(sel[k], expand) > 0.5) & causal
            s = _dot_nt(qs[k], kb).reshape(NSA_GROUP, tq, kc_sel)
            s = (s + jnp.where(seen, 0.0, NEG)[None]).reshape(rows, kc_sel)
            s_sc[k, c] = s
            m = mrun_sc[k]
            for j in range(kc_sel // LANES):
                m = jnp.maximum(m, s[:, j * LANES:(j + 1) * LANES])
            mrun_sc[k] = m
        return carry

    lax.fori_loop(0, n_chunks, scores, 0)
    m_sel = [jnp.broadcast_to(jnp.max(mrun_sc[k], axis=1, keepdims=True), (rows, LANES)) for k in heads]
    acc_sc[...] = jnp.zeros(acc_sc.shape, F32)

    def weigh(c, carry):
        k0 = pl.multiple_of(c * kc_sel, kc_sel)
        for k in heads:
            s = s_sc[k, c]
            p = jnp.concatenate([jnp.exp(s[:, j * LANES:(j + 1) * LANES] - m_sel[k])
                                 for j in range(kc_sel // LANES)], axis=1)
            acc_sc[k] += _dot(p.astype(BF16), vs_refs[k][0, pl.ds(k0, kc_sel), :])
        return carry

    lax.fori_loop(0, n_chunks, weigh, 0)

    for k in heads:
        o_sel = normalise(acc_sc[k], own_lanes(lane_r, k))
        outs = []
        for g in range(NSA_GROUP):
            r = slice(g * tq, (g + 1) * tq)
            gi = (k * NSA_GROUP + g) * 3
            og = (gates[:, gi:gi + 1] * o_cmp[k][r] + gates[:, gi + 1:gi + 2] * o_sel[r]
                  + gates[:, gi + 2:gi + 3] * o_win[k][r])
            if g % 2 != k:
                og = pltpu.roll(og, HEAD_DIM, 1)
            outs.append(og)
        for j in range(NSA_GROUP // 2):
            chunk = jnp.where(lane < HEAD_DIM, outs[2 * j], outs[2 * j + 1])
            cj = k * (NSA_GROUP // 2) + j
            o_ref[:, cj * LANES:(cj + 1) * LANES] = chunk.astype(BF16)


def _nsa(q, kcmp, vcmp, kv, gates_misc, overlap, B, S, tq):
    kc_sel, kc_win = KC_SEL, WINDOW + tq
    nq = S // tq
    n_cmp = kcmp.shape[1]
    rows = NSA_GROUP * tq
    kvspec = lambda j: pl.BlockSpec((1, S, LANES), lambda b, i: (b, 0, j))
    return pl.pallas_call(
        functools.partial(_nsa_kernel, tq=tq, kc_sel=kc_sel, kc_win=kc_win),
        grid=(B, nq),
        in_specs=[pl.BlockSpec((tq, D_NSA), lambda b, i: (b * nq + i, 0)),
                  pl.BlockSpec((1, n_cmp, LANES), lambda b, i: (b, 0, 0)),
                  pl.BlockSpec((1, n_cmp, LANES), lambda b, i: (b, 0, 0)),
                  kvspec(0), kvspec(1), kvspec(2), kvspec(3), kvspec(4), kvspec(5),
                  pl.BlockSpec((tq, LANES), lambda b, i: (b * nq + i, 0)),
                  _full(overlap.shape)],
        out_specs=pl.BlockSpec((tq, D_NSA), lambda b, i: (b * nq + i, 0)),
        out_shape=jax.ShapeDtypeStruct((B * S, D_NSA), BF16),
        scratch_shapes=[pltpu.VMEM((NSA_KV_HEADS, S // kc_sel, rows, kc_sel), F32),
                        pltpu.VMEM((NSA_KV_HEADS, rows, LANES), F32), pltpu.VMEM((NSA_KV_HEADS, rows, LANES), F32)],
        compiler_params=_params("parallel", "parallel"),
    )(q, kcmp, vcmp, kv, kv, kv, kv, kv, kv, gates_misc, overlap)


HALO = 32


def _mixout_kernel(mc_ref, mp_ref, on_ref, x_ref, cw_ref, cb_ref, lg_ref, lb_ref, pw_ref, pb_ref, plw_ref,
                   pls_ref, wo_ref, o_ref, hbuf, pbuf, hsh, *, ts, dc):
    i = pl.program_id(1)
    first = i == 0
    off_a, off_b, off_p = LANES, LANES + dc, LANES + 2 * dc

    def glu(ref, r):
        return ref[r, off_a:off_a + dc] * jax.nn.sigmoid(ref[r, off_b:off_b + dc])

    tail = slice(ts - HALO, ts)
    hbuf[0:HALO, :] = jnp.where(first, 0.0, glu(mp_ref, tail))
    hbuf[HALO:, :] = glu(mc_ref, slice(None))
    pbuf[0:HALO, :] = jnp.where(first, 0.0, mp_ref[tail, off_p:off_p + dc])
    pcur = mc_ref[:, off_p:off_p + dc]
    pbuf[HALO:, :] = pcur

    shifted = ts + HALO - SUBLANES
    for r in range(1, SUBLANES):
        hsh[r, 0:shifted, :] = hbuf[pl.ds(r, shifted), :]
    acc = jnp.zeros((ts, dc), F32) + cb_ref[...]
    for w in range(CONV_WIDTH):
        q, r = divmod(HALO - (CONV_WIDTH - 1) + w, SUBLANES)
        tap = hbuf[pl.ds(q * SUBLANES, ts), :] if r == 0 else hsh[r, pl.ds(q * SUBLANES, ts), :]
        acc = acc + tap * cw_ref[w:w + 1, :]
    mu = jnp.mean(acc, axis=-1, keepdims=True)
    xc = acc - mu
    y = xc * lax.rsqrt(jnp.mean(xc * xc, axis=-1, keepdims=True) + EPS) * lg_ref[...] + lb_ref[...]
    y = y * jax.nn.sigmoid(y)
    o_conv = _dot(y.astype(BF16), pw_ref[...]) + pb_ref[...]

    tglob = i * ts + lax.broadcasted_iota(jnp.int32, (ts, dc), 0)
    lane = lax.broadcasted_iota(jnp.int32, (ts, dc), 1)
    pg = dc // len(POOL_WINDOWS)
    run = pcur
    d = 1
    mean = jnp.zeros((ts, dc), F32)
    for gi, w in enumerate(POOL_WINDOWS):
        while d < w:
            run = run + pbuf[pl.ds(HALO - d, ts), :]
            d += 1
        cnt = jnp.minimum(tglob + 1, w).astype(F32)
        mean = jnp.where(lane // pg == gi, run / cnt, mean)
    o_pool = _dot((mean - pcur).astype(BF16), plw_ref[...]) * pls_ref[...]

    dn = on_ref.shape[1]
    o_ref[...] = (x_ref[...] + _dot(on_ref[...], wo_ref[0:dn, :])
                  + _dot(o_conv.astype(BF16), wo_ref[dn:dn + dc, :])
                  + _dot(o_pool.astype(BF16), wo_ref[dn + dc:, :]))


def _mixout(misc, o_nsa, x2, cw, cb, lg, lb, pw, pb, plw, pls, wo, B, S, ts):
    T, D = x2.shape
    dc = cw.shape[1]
    ns = S // ts
    nm = misc.shape[1]
    cur = lambda w: pl.BlockSpec((ts, w), lambda b, i: (b * ns + i, 0))
    prev = pl.BlockSpec((ts, nm), lambda b, i: (b * ns + jnp.maximum(i - 1, 0), 0))
    return pl.pallas_call(
        functools.partial(_mixout_kernel, ts=ts, dc=dc),
        grid=(B, ns),
        in_specs=[cur(nm), prev, cur(o_nsa.shape[1]), cur(D), _full(cw.shape), _full(cb.shape), _full(lg.shape),
                  _full(lb.shape), _full(pw.shape), _full(pb.shape), _full(plw.shape), _full(pls.shape),
                  _full(wo.shape)],
        out_specs=cur(D),
        out_shape=jax.ShapeDtypeStruct((T, D), F32),
        scratch_shapes=[pltpu.VMEM((ts + HALO, dc), F32), pltpu.VMEM((ts + HALO, dc), F32),
                        pltpu.VMEM((SUBLANES, ts + HALO, dc), F32)],
        compiler_params=_params("parallel", "parallel"),
    )(misc, misc, o_nsa, x2, cw, cb, lg, lb, pw, pb, plw, pls, wo)


def _norm_matmul_kernel(x_ref, g_ref, w_ref, o_ref):
    o_ref[...] = _dot(_rms(x_ref[...], g_ref[...]).astype(BF16), w_ref[...]).astype(o_ref.dtype)


def _norm_matmul(x2, g, w, tm, out_dtype):
    T, D = x2.shape
    n = w.shape[1]
    return pl.pallas_call(
        _norm_matmul_kernel,
        grid=(T // tm,),
        in_specs=[pl.BlockSpec((tm, D), lambda i: (i, 0)), _full((1, D)), _full((D, n))],
        out_specs=pl.BlockSpec((tm, n), lambda i: (i, 0)),
        out_shape=jax.ShapeDtypeStruct((T, n), out_dtype),
        compiler_params=_params("parallel"),
    )(x2, g, w)


def _xattn_kernel(x_ref, g_ref, wq_ref, k_ref, v_ref, wo_ref, o_ref):
    x = x_ref[...]
    D = x.shape[1]
    dh = D // XA_HEADS
    q = _dot(_rms(x, g_ref[...]).astype(BF16), wq_ref[...]) * (dh ** -0.5)
    outs = []
    for h in range(XA_HEADS):
        c = slice(h * dh, (h + 1) * dh)
        s = _dot_nt(q[:, c].astype(BF16), k_ref[0, :, c])
        e = jnp.exp(s - jnp.max(s, axis=-1, keepdims=True))
        p = e * (1.0 / jnp.sum(e, axis=-1, keepdims=True))
        outs.append(_dot(p.astype(BF16), v_ref[0, :, c]))
    o = jnp.concatenate(outs, axis=1)
    o_ref[...] = x + _dot(o.astype(BF16), wo_ref[...])


def _xattn(x2, g, wq, memkv, wo, B, S, tm):
    T, D = x2.shape
    ns = S // tm
    M = memkv.shape[1]
    return pl.pallas_call(
        _xattn_kernel,
        grid=(B, ns),
        in_specs=[pl.BlockSpec((tm, D), lambda b, i: (b * ns + i, 0)), _full((1, D)), _full((D, D)),
                  pl.BlockSpec((1, M, D), lambda b, i: (b, 0, 0)),
                  pl.BlockSpec((1, M, D), lambda b, i: (b, 0, 1)),
                  _full((D, D))],
        out_specs=pl.BlockSpec((tm, D), lambda b, i: (b * ns + i, 0)),
        out_shape=jax.ShapeDtypeStruct((T, D), F32),
        compiler_params=_params("parallel", "parallel"),
    )(x2, g, wq, memkv, memkv, wo)


PAIR_LIST = tuple((a, b) for a in range(PEER_TOPK) for b in range(PEER_TOPK) if (a + 1) * (b + 1) <= PEER_TOPK)
N_PAIR_ROWS = -(-len(PAIR_LIST) // 8) * 8
PAIR_COUNT = tuple(PEER_TOPK // (a + 1) for a in range(PEER_TOPK))
PAIR_START = tuple(sum(PAIR_COUNT[:a]) for a in range(PEER_TOPK))


def _top_rows_exact(v, n):
    R = v.shape[0]
    ridx = lax.broadcasted_iota(jnp.int32, v.shape, 0).astype(F32)
    rank = jnp.full(v.shape, float(n), F32)
    vals = []
    for r in range(n):
        m = jnp.max(v, axis=0, keepdims=True)
        first = jnp.min(jnp.where(v == m, ridx, float(R)), axis=0, keepdims=True)
        taken = ridx == first
        v = jnp.where(taken, LOWEST, v)
        rank = jnp.where(taken, float(r), rank)
        vals.append(m)
    return vals, rank


def _top_rows_distinct(v, n):
    rank = jnp.full(v.shape, float(n), F32)
    vals = []
    for r in range(n):
        m = jnp.max(v, axis=0, keepdims=True)
        taken = v == m
        v = jnp.where(taken, LOWEST, v)
        rank = jnp.where(taken, float(r), rank)
        vals.append(m)
    return vals, rank


def _peer_route_kernel(x_ref, g_ref, wqt_ref, sk_ref, hn_ref, c1_ref, g1_ref, r2_ref, g2_ref, qt_sc, s_sc, top_sc,
                       rank_sc, cand_sc, cw_sc):
    hn = _rms(x_ref[...], g_ref[...]).astype(BF16)
    hn_ref[...] = hn
    qt_sc[...] = _dot_nt(wqt_ref[...], hn).astype(BF16)
    nk, n = PEER_KEYS, PEER_TOPK
    n_lists = 2 * PEER_HEADS

    def one_list(li, carry):
        r0 = pl.multiple_of(li * nk, nk)
        s = _dot(sk_ref[li % 2], qt_sc[pl.ds(r0, nk), :])
        s_sc[pl.ds(r0, nk), :] = s

        def put(vals, rank):
            for r in range(n):
                top_sc[pl.ds(li * n + r, 1), :] = vals[r]
            rank_sc[pl.ds(r0, nk), :] = rank

        vals, rank = _top_rows_distinct(s, n)
        put(vals, rank)
        ranked = jnp.sum((rank < float(n)).astype(F32), axis=0, keepdims=True)

        @pl.when(jnp.max(ranked) > float(n))
        def _():
            put(*_top_rows_exact(s_sc[pl.ds(r0, nk), :], n))

        return carry

    lax.fori_loop(0, n_lists, one_list, 0)

    cand_sc[...] = jnp.full(cand_sc.shape, LOWEST, F32)
    cw_sc[...] = jnp.zeros(cw_sc.shape, F32)
    for h in range(PEER_HEADS):
        l1, l2 = 2 * h, 2 * h + 1
        top1 = [top_sc[l1 * n + a:l1 * n + a + 1, :] for a in range(n)]
        top2 = [top_sc[l2 * n + a:l2 * n + a + 1, :] for a in range(n)]
        e1 = [jnp.exp(t - top1[0]) for t in top1]
        e2 = [jnp.exp(t - top2[0]) for t in top2]
        for r, (a, b) in enumerate(PAIR_LIST):
            cand_sc[h, r:r + 1, :] = top1[a] + top2[b]
            cw_sc[h, r:r + 1, :] = e1[a] * e2[b]
        _, crank = _top_rows_exact(cand_sc[h], n)
        chosen = (crank < float(n)).astype(F32)
        z = jnp.sum(chosen * cw_sc[h], axis=0, keepdims=True)
        rank1 = rank_sc[l1 * nk:(l1 + 1) * nk, :]
        count1 = jnp.zeros(rank1.shape, F32)
        for a in range(n):
            n_a = jnp.sum(chosen[PAIR_START[a]:PAIR_START[a] + PAIR_COUNT[a]], axis=0, keepdims=True)
            count1 = jnp.where(rank1 == float(a), n_a, count1)
        rows = slice(h * nk, (h + 1) * nk)
        c1_ref[rows, :] = count1
        g1_ref[rows, :] = jnp.exp(s_sc[l1 * nk:(l1 + 1) * nk, :] - top1[0]) * (1.0 / z)
        r2_ref[rows, :] = rank_sc[l2 * nk:(l2 + 1) * nk, :].astype(BF16)
        g2_ref[rows, :] = jnp.exp(s_sc[l2 * nk:(l2 + 1) * nk, :] - top2[0]).astype(BF16)


def _peer_route(x2, g, wqt, sk, tm):
    T, D = x2.shape
    nr = PEER_HEADS * PEER_KEYS
    col = pl.BlockSpec((nr, tm), lambda i: (0, i))
    return pl.pallas_call(
        _peer_route_kernel,
        grid=(T // tm,),
        in_specs=[pl.BlockSpec((tm, D), lambda i: (i, 0)), _full((1, D)), _full(wqt.shape), _full(sk.shape)],
        out_specs=[pl.BlockSpec((tm, D), lambda i: (i, 0)), col, col, col, col],
        out_shape=[jax.ShapeDtypeStruct((T, D), BF16), jax.ShapeDtypeStruct((nr, T), F32),
                   jax.ShapeDtypeStruct((nr, T), F32), jax.ShapeDtypeStruct((nr, T), BF16),
                   jax.ShapeDtypeStruct((nr, T), BF16)],
        scratch_shapes=[pltpu.VMEM((2 * nr, tm), BF16), pltpu.VMEM((2 * nr, tm), F32),
                        pltpu.VMEM((2 * PEER_HEADS * PEER_TOPK, tm), F32),
                        pltpu.VMEM((2 * nr, tm), F32), pltpu.VMEM((PEER_HEADS, N_PAIR_ROWS, tm), F32),
                        pltpu.VMEM((PEER_HEADS, N_PAIR_ROWS, tm), F32)],
        compiler_params=_params("parallel"),
    )(x2, g, wqt, sk)


def _peer_expert_kernel(*refs, subs, final_norm):
    hn_ref, c1_ref, g1_ref, r2_ref, g2_ref, u_ref, vt_ref, x_ref = refs[:8]
    fg_ref = refs[8] if final_norm else None
    o_ref, acc_sc, wa_sc, act_sc = refs[-4:]
    c = pl.program_id(1)
    nk = PEER_KEYS
    te = sum(subs)
    offs = [sum(subs[:j]) for j in range(len(subs))]
    tm = hn_ref.shape[0]

    @pl.when(c == 0)
    def _():
        acc_sc[...] = jnp.zeros(acc_sc.shape, F32)

    def pre_act(j):
        o, rows = offs[j], subs[j]
        act_sc[j % 2, 0:rows, :] = _dot_nt(u_ref[o:o + rows, :], hn_ref[...]).astype(BF16)

    pre_act(0)
    for j, (o, rows) in enumerate(zip(offs, subs)):
        if j + 1 < len(subs):
            pre_act(j + 1)
        for il in range(rows // nk):
            i = c * (te // nk) + o // nk + il
            w = jnp.zeros((nk, tm), BF16)
            for h in range(PEER_HEADS):
                count = c1_ref[pl.ds(h * nk + i, 1), :].astype(BF16)
                g1row = g1_ref[pl.ds(h * nk + i, 1), :].astype(BF16)
                hr = slice(h * nk, (h + 1) * nk)
                w = w + jnp.where(r2_ref[hr, :] < count, g2_ref[hr, :], 0.0) * g1row
            er = slice(il * nk, (il + 1) * nk)
            wa_sc[o + il * nk:o + (il + 1) * nk, :] = w * _gelu(act_sc[j % 2, er, :])
        acc_sc[...] += _dot(vt_ref[:, o:o + rows], wa_sc[o:o + rows, :])

    @pl.when(c == pl.num_programs(1) - 1)
    def _():
        out = x_ref[...] + acc_sc[...].T
        o_ref[...] = _rms(out, fg_ref[...]) if final_norm else out


def _peer_expert(hn, c1, g1, r2, g2, u, vt, layer, x2, final_g, tm, subs):
    T, D = x2.shape
    ne = u.shape[1]
    te = sum(subs)
    tok = pl.BlockSpec((c1.shape[0], tm), lambda i, c: (0, i))
    row = pl.BlockSpec((tm, D), lambda i, c: (i, 0))
    return pl.pallas_call(
        functools.partial(_peer_expert_kernel, subs=subs, final_norm=final_g is not None),
        grid=(T // tm, ne // te),
        in_specs=[row, tok, tok, tok, tok,
                  pl.BlockSpec((None, te, D), lambda i, c: (layer, c, 0)),
                  pl.BlockSpec((None, D, te), lambda i, c: (layer, 0, c)),
                  row] + ([] if final_g is None else [_full((1, D))]),
        out_specs=row,
        out_shape=jax.ShapeDtypeStruct((T, D), F32),
        scratch_shapes=[pltpu.VMEM((D, tm), F32), pltpu.VMEM((te, tm), BF16),
                        pltpu.VMEM((2, max(subs), tm), BF16)],
        compiler_params=_params("parallel", "arbitrary"),
    )(hn, c1, g1, r2, g2, u, vt, x2, *([] if final_g is None else [final_g]))


def _rope_tables(positions):
    B, S = positions.shape
    freqs = ROPE_THETA ** (-jnp.arange(ROT_HALF, dtype=F32) * 2.0 / ROT_DIM)
    ang = positions.astype(F32)[:, :, None] * freqs
    cos, sin = jnp.cos(ang), jnp.sin(ang)
    ones = jnp.ones((B, S, HEAD_DIM - ROT_DIM), F32)
    zeros8 = jnp.zeros((B, S, ROT_HALF), F32)
    zeros = jnp.zeros((B, S, HEAD_DIM - ROT_DIM), F32)
    cos_h = jnp.concatenate([cos, cos, ones], axis=-1)
    sa_h = jnp.concatenate([zeros8, sin, zeros], axis=-1)
    sb_h = jnp.concatenate([-sin, zeros8, zeros], axis=-1)
    rep = LANES // HEAD_DIM
    tile = lambda t: jnp.tile(t, (1, 1, rep)).reshape(B * S, LANES)
    return tile(cos_h), tile(sa_h), tile(sb_h)


def _overlap_matrix(n_chunk, n_sel):
    ci = np.arange(n_chunk)[:, None] * CMP_STRIDE
    sj = np.arange(LANES)[None, :] * SEL_BLOCK
    ov = (ci < sj + SEL_BLOCK) & (ci + CMP_BLOCK > sj) & (np.arange(LANES)[None, :] < n_sel)
    ov = ov & (np.arange(n_chunk)[:, None] < n_chunk - 1)
    return jnp.asarray(ov, BF16)


def kernel(x, mem, positions, norm_mix_g, w_in, cmp_pe, cmp_w1, cmp_w2, conv_w, conv_b, conv_ln_g, conv_ln_b, conv_pw_w, conv_pw_b, pool_w, pool_scale, w_out, norm_xa_g, norm_mem_g, xa_wq, xa_wkv, xa_wo, norm_ffn_g, peer_wq, peer_subkeys, peer_u, peer_v, final_g):
    B, S, D = x.shape
    T = B * S
    depth = w_in.shape[0]
    M = mem.shape[1]
    dc = conv_w.shape[-1]
    n_chunk = S // CMP_STRIDE
    n_sel = S // SEL_BLOCK
    assert S % max(TM_PROJ, TS_MIX, KC_SEL, TM_PEER) == 0 and SEL_TOPN <= n_sel <= LANES and D % LANES == 0
    assert S >= WINDOW + TQ_NSA and peer_u.shape[1] % sum(EXPERT_SUBS) == 0

    cos_t, sa_t, sb_t = _rope_tables(positions)
    overlap = _overlap_matrix(n_chunk, n_sel)
    row = lambda v: v.reshape(1, -1)
    x2 = x.reshape(T, D)
    mem2 = mem.reshape(B * M, D)
    n_gate = 3 * NSA_HEADS

    w_pad = jnp.concatenate([w_in[:, :, :D_QKV + n_gate], jnp.zeros((depth, D, LANES - n_gate), F32),
                             w_in[:, :, D_QKV + n_gate:]], axis=2).astype(BF16)
    pg = dc // len(POOL_WINDOWS)
    plw = jnp.zeros((depth, dc, dc), F32)
    for gi in range(len(POOL_WINDOWS)):
        plw = plw.at[:, gi * pg:(gi + 1) * pg, gi * pg:(gi + 1) * pg].set(pool_w[:, gi])
    plw, conv_pw, wo = plw.astype(BF16), conv_pw_w.astype(BF16), w_out.astype(BF16)
    wkv, wq, wxo = xa_wkv.astype(BF16), xa_wq.astype(BF16), xa_wo.astype(BF16)
    pwq_t = jnp.swapaxes(peer_wq, 1, 2).astype(BF16)
    psk, pu = peer_subkeys.astype(BF16), peer_u.astype(BF16)
    pv_t = jnp.swapaxes(peer_v, 1, 2).astype(BF16)

    for l in range(depth):
        q, kvc, kv, misc = _inproj(x2, row(norm_mix_g[l]), w_pad[l], cos_t, sa_t, sb_t, tm=TM_PROJ)
        xc = kvc.reshape(B, n_chunk, CMP_STRIDE * 2 * LANES)
        cmp = _compress(xc, *_compress_weights(cmp_pe[l], cmp_w1[l], cmp_w2[l]))
        o_nsa = _nsa(q, cmp[0], cmp[1], kv.reshape(B, S, -1), misc, overlap, B, S, tq=TQ_NSA)
        x2 = _mixout(misc, o_nsa, x2, conv_w[l], row(conv_b[l]), row(conv_ln_g[l]), row(conv_ln_b[l]),
                     conv_pw[l], row(conv_pw_b[l]), plw[l], row(pool_scale[l]), wo[l], B, S, ts=TS_MIX)
        memkv = _norm_matmul(mem2, row(norm_mem_g[l]), wkv[l], tm=TM_MEM, out_dtype=BF16)
        x2 = _xattn(x2, row(norm_xa_g[l]), wq[l], memkv.reshape(B, M, 2 * D), wxo[l], B, S, tm=TM_PROJ)
        hn, c1, g1, r2, g2 = _peer_route(x2, row(norm_ffn_g[l]), pwq_t[l], psk[l], tm=TM_PEER)
        x2 = _peer_expert(hn, c1, g1, r2, g2, pu, pv_t, l, x2,
                          row(final_g) if l == depth - 1 else None, tm=TM_PEER, subs=EXPERT_SUBS)
    return x2.reshape(B, S, D)
```

```python
import functools
import math

import jax
import jax.numpy as jnp
import numpy as np
from jax import lax
from jax.experimental import pallas as pl
from jax.experimental.pallas import tpu as pltpu

F32 = jnp.float32
BF16 = jnp.bfloat16

NSA_HEADS = 8
NSA_KV_HEADS = 2
NSA_GROUP = NSA_HEADS // NSA_KV_HEADS
HEAD_DIM = 64
D_NSA = NSA_HEADS * HEAD_DIM
D_KV = NSA_KV_HEADS * HEAD_DIM
ROT_DIM = HEAD_DIM // 4
ROT_HALF = ROT_DIM // 2
ROPE_THETA = 500000.0
CMP_BLOCK = 32
CMP_STRIDE = 16
SEL_BLOCK = 64
SEL_TOPN = 16
WINDOW = 512
CONV_WIDTH = 31
POOL_WINDOWS = (2, 4, 8, 16)
XA_HEADS = 4
PEER_HEADS = 8
PEER_KEYS = 128
PEER_TOPK = 16
EPS = 1e-6
NEG = -1e30
FORCE = 1e4
LOWEST = -3.0e38

LANES = 128
SUBLANES = 8
VMEM_LIMIT = 56 * 1024 * 1024

NT_DIMS = (((1,), (1,)), ((), ()))

TM_PROJ = 512
TM_MEM = 256
TQ_NSA = 256
KC_SEL = 512
TS_MIX = 512
TM_PEER = 512
EXPERT_SUBS = (256, 768, 768, 256)


def _params(*sem):
    return pltpu.CompilerParams(dimension_semantics=sem, vmem_limit_bytes=VMEM_LIMIT)


def _full(shape):
    nd = len(shape)
    return pl.BlockSpec(shape, lambda *_: (0,) * nd)


def _rms(x, g):
    return x * lax.rsqrt(jnp.mean(x * x, axis=-1, keepdims=True) + EPS) * g


def _gelu(x):
    c = math.sqrt(2.0 / math.pi)
    return 0.5 * x * (1.0 + jnp.tanh(c * (x + 0.044715 * (x * x * x))))


def _dot(a, b):
    return jnp.dot(a, b, preferred_element_type=F32)


def _dot_nt(a, b):
    return lax.dot_general(a, b, NT_DIMS, preferred_element_type=F32)


N_ROPE_Q = D_NSA // LANES
N_KV_CHUNKS = 6
D_QKV = D_NSA + N_KV_CHUNKS * D_KV
N_KV_OUT = 2 + 2 * NSA_KV_HEADS


def _inproj_kernel(x_ref, g_ref, w_ref, cos_ref, sa_ref, sb_ref, q_ref, kvc_ref, kv_ref, misc_ref):
    y = _rms(x_ref[...], g_ref[...])
    z = _dot(y.astype(BF16), w_ref[...])
    cos, sa, sb = cos_ref[...], sa_ref[...], sb_ref[...]

    def rope(c):
        return c * cos + pltpu.roll(c, ROT_HALF, 1) * sa + pltpu.roll(c, LANES - ROT_HALF, 1) * sb

    scale = HEAD_DIM ** -0.5
    for j in range(N_ROPE_Q):
        q_ref[:, j * LANES:(j + 1) * LANES] = (rope(z[:, j * LANES:(j + 1) * LANES]) * scale).astype(BF16)
    lane = lax.broadcasted_iota(jnp.int32, (z.shape[0], LANES), 1)
    out = 0
    for j in range(N_KV_CHUNKS):
        c = z[:, D_NSA + j * LANES:D_NSA + (j + 1) * LANES]
        if j % 2 == 0:
            c = rope(c)
        if j < 2:
            kvc_ref[:, j * LANES:(j + 1) * LANES] = c.astype(BF16)
        elif j % 2 == 0:
            kv_ref[:, out * LANES:(out + 1) * LANES] = c.astype(BF16)
            out += 1
    for j in (3, 5):
        c = z[:, D_NSA + j * LANES:D_NSA + (j + 1) * LANES]
        for k in range(NSA_KV_HEADS):
            own = (lane >= HEAD_DIM) if k == 1 else (lane < HEAD_DIM)
            kv_ref[:, out * LANES:(out + 1) * LANES] = jnp.where(own, c, 1.0).astype(BF16)
            out += 1
    misc_ref[...] = z[:, D_QKV:]


def _inproj(x2, g, w_pad, cos_t, sa_t, sb_t, tm):
    T, D = x2.shape
    n = w_pad.shape[1]
    n_misc = n - D_QKV
    row = lambda w: pl.BlockSpec((tm, w), lambda i: (i, 0))
    return pl.pallas_call(
        _inproj_kernel,
        grid=(T // tm,),
        in_specs=[row(D), _full((1, D)), _full((D, n)), row(LANES), row(LANES), row(LANES)],
        out_specs=[row(D_NSA), row(2 * LANES), row(N_KV_OUT * LANES), row(n_misc)],
        out_shape=[jax.ShapeDtypeStruct((T, D_NSA), BF16),
                   jax.ShapeDtypeStruct((T, 2 * LANES), BF16),
                   jax.ShapeDtypeStruct((T, N_KV_OUT * LANES), BF16),
                   jax.ShapeDtypeStruct((T, n_misc), F32)],
        compiler_params=_params("parallel"),
    )(x2, g, w_pad, cos_t, sa_t, sb_t)


def _compress_kernel(x_ref, pe_ref, w1_ref, w2_ref, o_ref):
    x = x_ref[0].astype(F32)
    n = x.shape[0]
    out = jnp.zeros((n, LANES), F32)
    for hd in range(NSA_KV_HEADS):
        a = _dot((x + pe_ref[0, hd, 0:1]).astype(BF16), w1_ref[0, hd, 0])
        b = _dot((x + pe_ref[0, hd, 1:2]).astype(BF16), w1_ref[0, hd, 1])
        hid = _gelu(a + pltpu.roll(b, n - 1, 0))
        out = out + _dot(hid.astype(BF16), w2_ref[0, hd])
    o_ref[0, 0] = out.astype(BF16)


def _compress(xc, pe, w1, w2):
    nb, nch, width = xc.shape
    hid = w1.shape[-1]
    return pl.pallas_call(
        _compress_kernel,
        grid=(2, nb),
        in_specs=[pl.BlockSpec((1, nch, width), lambda w, i: (i, 0, 0)),
                  pl.BlockSpec((1, NSA_KV_HEADS, 2, width), lambda w, i: (w, 0, 0, 0)),
                  pl.BlockSpec((1, NSA_KV_HEADS, 2, width, hid), lambda w, i: (w, 0, 0, 0, 0)),
                  pl.BlockSpec((1, NSA_KV_HEADS, hid, LANES), lambda w, i: (w, 0, 0, 0))],
        out_specs=pl.BlockSpec((1, 1, nch, LANES), lambda w, i: (w, i, 0, 0)),
        out_shape=jax.ShapeDtypeStruct((2, nb, nch, LANES), BF16),
        compiler_params=_params("parallel", "parallel"),
    )(xc, pe, w1, w2)


def _compress_weights(pe, w1, w2):
    hid = w1.shape[-1]
    eye_t = jnp.eye(2, dtype=F32)[:, None, None, None, :, None, None]
    eye_h = jnp.eye(NSA_KV_HEADS, dtype=F32)[None, :, None, None, None, :, None]
    per = (2, 1, 2, CMP_STRIDE, 1, 1, HEAD_DIM)
    width = CMP_STRIDE * 2 * NSA_KV_HEADS * HEAD_DIM
    pe_big = (pe.reshape(per) * eye_t * eye_h).reshape(2, NSA_KV_HEADS, 2, width)
    w1_big = (w1.reshape(per + (hid,)) * eye_t[..., None] * eye_h[..., None]).reshape(2, NSA_KV_HEADS, 2, width, hid)
    eye_o = jnp.eye(NSA_KV_HEADS, dtype=F32)[None, :, None, :, None]
    w2_big = (w2[:, None, :, None, :] * eye_o).reshape(2, NSA_KV_HEADS, hid, NSA_KV_HEADS * HEAD_DIM)
    return pe_big, w1_big.astype(BF16), w2_big.astype(BF16)


def _nsa_kernel(q_ref, kcmp_ref, vcmp_ref, ks_ref, kw_ref, vs0_ref, vs1_ref, vw0_ref, vw1_ref, gate_ref, ov_ref,
                o_ref, s_sc, mrun_sc, acc_sc, *, tq, kc_sel, kc_win):
    t0 = pl.program_id(1) * tq
    rows = NSA_GROUP * tq
    lane = lax.broadcasted_iota(jnp.int32, (tq, LANES), 1)
    lane_r = lax.broadcasted_iota(jnp.int32, (rows, LANES), 1)
    tpos = t0 + lax.broadcasted_iota(jnp.int32, (tq, 1), 0)
    gates = jax.nn.sigmoid(gate_ref[...])
    n_cmp = kcmp_ref.shape[1]
    n_sel = ks_ref.shape[1] // SEL_BLOCK
    vs_refs, vw_refs = (vs0_ref, vs1_ref), (vw0_ref, vw1_ref)

    def normalise(acc, own):
        return acc / jnp.where(own, pltpu.roll(acc, HEAD_DIM, 1), 1.0)

    def own_lanes(lanes, k):
        return (lanes >= HEAD_DIM) if k == 1 else (lanes < HEAD_DIM)

    def queries(k):
        parts = []
        for g in range(NSA_GROUP):
            hh = k * NSA_GROUP + g
            c = q_ref[:, (hh // 2) * LANES:(hh // 2 + 1) * LANES].astype(F32)
            if hh % 2 != k:
                c = pltpu.roll(c, HEAD_DIM, 1)
            parts.append(jnp.where(own_lanes(lane, k), c, 0.0))
        return jnp.concatenate(parts, axis=0).astype(BF16)

    def compressed(qs):
        s = _dot_nt(qs, kcmp_ref[0]).reshape(NSA_GROUP, tq, n_cmp)
        cmp_end = lax.broadcasted_iota(jnp.int32, (tq, n_cmp), 1) * CMP_STRIDE + (CMP_BLOCK - 1)
        valid = cmp_end <= tpos
        s = s + jnp.where(valid, 0.0, NEG)[None]
        e = jnp.exp(s - jnp.max(s, axis=-1, keepdims=True)) * valid.astype(F32)[None]
        l = jnp.sum(e, axis=-1, keepdims=True)
        p3 = e * (1.0 / jnp.where(l > 0.0, l, 1.0))
        o_cmp = _dot(p3.reshape(rows, n_cmp).astype(BF16), vcmp_ref[0])

        psum = p3[0] + p3[1] + p3[2] + p3[3]
        ov = ov_ref[...]
        p_hi = psum.astype(BF16)
        r1 = psum - p_hi.astype(F32)
        p_mid = r1.astype(BF16)
        p_lo = (r1 - p_mid.astype(F32)).astype(BF16)
        imp = _dot(p_hi, ov) + _dot(p_mid, ov) + _dot(p_lo, ov)
        imp = imp.T[:n_sel]
        blk = lax.broadcasted_iota(jnp.int32, (n_sel, tq), 0)
        tpos_t = t0 + lax.broadcasted_iota(jnp.int32, (1, tq), 1)
        cur = tpos_t // SEL_BLOCK
        forced = (blk == 0) | (blk == cur) | (blk == cur - 1)
        imp = jnp.where(blk * SEL_BLOCK <= tpos_t, jnp.where(forced, FORCE, imp), NEG)
        sub8 = lax.broadcasted_iota(jnp.int32, (SUBLANES, tq), 0)
        groups = [imp[r:r + SUBLANES] for r in range(0, n_sel, SUBLANES)]
        ranks = [jnp.zeros((SUBLANES, tq), jnp.int32) for _ in groups]
        for jp in range(n_sel):
            one = imp[jp:jp + 1, :]
            for gi, grp in enumerate(groups):
                lo = gi * SUBLANES
                if lo > jp:
                    beats = one >= grp
                elif lo + SUBLANES - 1 < jp:
                    beats = one > grp
                else:
                    beats = (one > grp) | ((one == grp) & (sub8 > jp - lo))
                ranks[gi] = ranks[gi] + beats.astype(jnp.int32)
        rank = jnp.concatenate(ranks, axis=0)
        sel_t = jnp.concatenate([(rank < SEL_TOPN).astype(F32), jnp.zeros((LANES - n_sel, tq), F32)], axis=0)
        return o_cmp, sel_t.T.astype(BF16)

    def window(k, qs):
        w0 = pl.multiple_of(jnp.maximum(t0 + tq - kc_win, 0), tq)
        sw = _dot_nt(qs, kw_ref[0, pl.ds(w0, kc_win), :]).reshape(NSA_GROUP, tq, kc_win)
        wpos = w0 + lax.broadcasted_iota(jnp.int32, (tq, kc_win), 1)
        in_win = (wpos <= tpos) & (wpos > tpos - WINDOW)
        sw = (sw + jnp.where(in_win, 0.0, NEG)[None]).reshape(rows, kc_win)
        pw = jnp.exp(sw - jnp.max(sw, axis=1, keepdims=True))
        return normalise(_dot(pw.astype(BF16), vw_refs[k][0, pl.ds(w0, kc_win), :]), own_lanes(lane_r, k))

    heads = range(NSA_KV_HEADS)
    qs = [queries(k) for k in heads]
    o_cmp, sel = zip(*[compressed(qs[k]) for k in heads])
    o_win = [window(k, qs[k]) for k in heads]

    n_chunks = (t0 + tq + kc_sel - 1) // kc_sel
    mrun_sc[...] = jnp.full(mrun_sc.shape, NEG, F32)

    def scores(c, carry):
        k0 = pl.multiple_of(c * kc_sel, kc_sel)
        jrow = lax.broadcasted_iota(jnp.int32, (LANES, kc_sel), 0)
        kcol = lax.broadcasted_iota(jnp.int32, (LANES, kc_sel), 1)
        expand = (jrow == k0 // SEL_BLOCK + kcol // SEL_BLOCK).astype(BF16)
        causal = k0 + lax.broadcasted_iota(jnp.int32, (tq, kc_sel), 1) <= tpos
        kb = ks_ref[0, pl.ds(k0, kc_sel), :]
        for k in heads:
            seen = (_dot(sel[k], expand) > 0.5) & causal
            s = _dot_nt(qs[k], kb).reshape(NSA_GROUP, tq, kc_sel)
            s = (s + jnp.where(seen, 0.0, NEG)[None]).reshape(rows, kc_sel)
            s_sc[k, c] = s
            m = mrun_sc[k]
            for j in range(kc_sel // LANES):
                m = jnp.maximum(m, s[:, j * LANES:(j + 1) * LANES])
            mrun_sc[k] = m
        return carry

    lax.fori_loop(0, n_chunks, scores, 0)
    m_sel = [jnp.broadcast_to(jnp.max(mrun_sc[k], axis=1, keepdims=True), (rows, LANES)) for k in heads]
    acc_sc[...] = jnp.zeros(acc_sc.shape, F32)

    def weigh(c, carry):
        k0 = pl.multiple_of(c * kc_sel, kc_sel)
        for k in heads:
            s = s_sc[k, c]
            p = jnp.concatenate([jnp.exp(s[:, j * LANES:(j + 1) * LANES] - m_sel[k])
                                 for j in range(kc_sel // LANES)], axis=1)
            acc_sc[k] += _dot(p.astype(BF16), vs_refs[k][0, pl.ds(k0, kc_sel), :])
        return carry

    lax.fori_loop(0, n_chunks, weigh, 0)

    for k in heads:
        o_sel = normalise(acc_sc[k], own_lanes(lane_r, k))
        outs = []
        for g in range(NSA_GROUP):
            r = slice(g * tq, (g + 1) * tq)
            gi = (k * NSA_GROUP + g) * 3
            og = (gates[:, gi:gi + 1] * o_cmp[k][r] + gates[:, gi + 1:gi + 2] * o_sel[r]
                  + gates[:, gi + 2:gi + 3] * o_win[k][r])
            if g % 2 != k:
                og = pltpu.roll(og, HEAD_DIM, 1)
            outs.append(og)
        for j in range(NSA_GROUP // 2):
            chunk = jnp.where(lane < HEAD_DIM, outs[2 * j], outs[2 * j + 1])
            cj = k * (NSA_GROUP // 2) + j
            o_ref[:, cj * LANES:(cj + 1) * LANES] = chunk.astype(BF16)


def _nsa(q, kcmp, vcmp, kv, gates_misc, overlap, B, S, tq):
    kc_sel, kc_win = KC_SEL, WINDOW + tq
    nq = S // tq
    n_cmp = kcmp.shape[1]
    rows = NSA_GROUP * tq
    kvspec = lambda j: pl.BlockSpec((1, S, LANES), lambda b, i: (b, 0, j), pipeline_mode=pl.Buffered(1))
    return pl.pallas_call(
        functools.partial(_nsa_kernel, tq=tq, kc_sel=kc_sel, kc_win=kc_win),
        grid=(B, nq),
        in_specs=[pl.BlockSpec((tq, D_NSA), lambda b, i: (b * nq + i, 0)),
                  pl.BlockSpec((1, n_cmp, LANES), lambda b, i: (b, 0, 0)),
                  pl.BlockSpec((1, n_cmp, LANES), lambda b, i: (b, 0, 0)),
                  kvspec(0), kvspec(1), kvspec(2), kvspec(3), kvspec(4), kvspec(5),
                  pl.BlockSpec((tq, LANES), lambda b, i: (b * nq + i, 0)),
                  _full(overlap.shape)],
        out_specs=pl.BlockSpec((tq, D_NSA), lambda b, i: (b * nq + i, 0)),
        out_shape=jax.ShapeDtypeStruct((B * S, D_NSA), BF16),
        scratch_shapes=[pltpu.VMEM((NSA_KV_HEADS, S // kc_sel, rows, kc_sel), F32),
                        pltpu.VMEM((NSA_KV_HEADS, rows, LANES), F32), pltpu.VMEM((NSA_KV_HEADS, rows, LANES), F32)],
        compiler_params=_params("parallel", "parallel"),
    )(q, kcmp, vcmp, kv, kv, kv, kv, kv, kv, gates_misc, overlap)


HALO = 32


def _mixout_kernel(mc_ref, mp_ref, on_ref, x_ref, cw_ref, cb_ref, lg_ref, lb_ref, pw_ref, pb_ref, plw_ref,
                   pls_ref, wo_ref, o_ref, hbuf, pbuf, hsh, *, ts, dc):
    i = pl.program_id(1)
    first = i == 0
    off_a, off_b, off_p = LANES, LANES + dc, LANES + 2 * dc

    def glu(ref, r):
        return ref[r, off_a:off_a + dc] * jax.nn.sigmoid(ref[r, off_b:off_b + dc])

    tail = slice(ts - HALO, ts)
    hbuf[0:HALO, :] = jnp.where(first, 0.0, glu(mp_ref, tail))
    hbuf[HALO:, :] = glu(mc_ref, slice(None))
    pbuf[0:HALO, :] = jnp.where(first, 0.0, mp_ref[tail, off_p:off_p + dc])
    pcur = mc_ref[:, off_p:off_p + dc]
    pbuf[HALO:, :] = pcur

    shifted = ts + HALO - SUBLANES
    for r in range(1, SUBLANES):
        hsh[r, 0:shifted, :] = hbuf[pl.ds(r, shifted), :]
    acc = jnp.zeros((ts, dc), F32) + cb_ref[...]
    for w in range(CONV_WIDTH):
        q, r = divmod(HALO - (CONV_WIDTH - 1) + w, SUBLANES)
        tap = hbuf[pl.ds(q * SUBLANES, ts), :] if r == 0 else hsh[r, pl.ds(q * SUBLANES, ts), :]
        acc = acc + tap * cw_ref[w:w + 1, :]
    mu = jnp.mean(acc, axis=-1, keepdims=True)
    xc = acc - mu
    y = xc * lax.rsqrt(jnp.mean(xc * xc, axis=-1, keepdims=True) + EPS) * lg_ref[...] + lb_ref[...]
    y = y * jax.nn.sigmoid(y)
    o_conv = _dot(y.astype(BF16), pw_ref[...]) + pb_ref[...]

    tglob = i * ts + lax.broadcasted_iota(jnp.int32, (ts, dc), 0)
    lane = lax.broadcasted_iota(jnp.int32, (ts, dc), 1)
    pg = dc // len(POOL_WINDOWS)
    run = pcur
    d = 1
    mean = jnp.zeros((ts, dc), F32)
    for gi, w in enumerate(POOL_WINDOWS):
        while d < w:
            run = run + pbuf[pl.ds(HALO - d, ts), :]
            d += 1
        cnt = jnp.minimum(tglob + 1, w).astype(F32)
        mean = jnp.where(lane // pg == gi, run / cnt, mean)
    o_pool = _dot((mean - pcur).astype(BF16), plw_ref[...]) * pls_ref[...]

    dn = on_ref.shape[1]
    o_ref[...] = (x_ref[...] + _dot(on_ref[...], wo_ref[0:dn, :])
                  + _dot(o_conv.astype(BF16), wo_ref[dn:dn + dc, :])
                  + _dot(o_pool.astype(BF16), wo_ref[dn + dc:, :]))


def _mixout(misc, o_nsa, x2, cw, cb, lg, lb, pw, pb, plw, pls, wo, B, S, ts):
    T, D = x2.shape
    dc = cw.shape[1]
    ns = S // ts
    nm = misc.shape[1]
    cur = lambda w: pl.BlockSpec((ts, w), lambda b, i: (b * ns + i, 0))
    prev = pl.BlockSpec((ts, nm), lambda b, i: (b * ns + jnp.maximum(i - 1, 0), 0))
    return pl.pallas_call(
        functools.partial(_mixout_kernel, ts=ts, dc=dc),
        grid=(B, ns),
        in_specs=[cur(nm), prev, cur(o_nsa.shape[1]), cur(D), _full(cw.shape), _full(cb.shape), _full(lg.shape),
                  _full(lb.shape), _full(pw.shape), _full(pb.shape), _full(plw.shape), _full(pls.shape),
                  _full(wo.shape)],
        out_specs=cur(D),
        out_shape=jax.ShapeDtypeStruct((T, D), F32),
        scratch_shapes=[pltpu.VMEM((ts + HALO, dc), F32), pltpu.VMEM((ts + HALO, dc), F32),
                        pltpu.VMEM((SUBLANES, ts + HALO, dc), F32)],
        compiler_params=_params("parallel", "parallel"),
    )(misc, misc, o_nsa, x2, cw, cb, lg, lb, pw, pb, plw, pls, wo)


def _norm_matmul_kernel(x_ref, g_ref, w_ref, o_ref):
    o_ref[...] = _dot(_rms(x_ref[...], g_ref[...]).astype(BF16), w_ref[...]).astype(o_ref.dtype)


def _norm_matmul(x2, g, w, tm, out_dtype):
    T, D = x2.shape
    n = w.shape[1]
    return pl.pallas_call(
        _norm_matmul_kernel,
        grid=(T // tm,),
        in_specs=[pl.BlockSpec((tm, D), lambda i: (i, 0)), _full((1, D)), _full((D, n))],
        out_specs=pl.BlockSpec((tm, n), lambda i: (i, 0)),
        out_shape=jax.ShapeDtypeStruct((T, n), out_dtype),
        compiler_params=_params("parallel"),
    )(x2, g, w)


def _xattn_kernel(x_ref, g_ref, wq_ref, k_ref, v_ref, wo_ref, o_ref):
    x = x_ref[...]
    D = x.shape[1]
    dh = D // XA_HEADS
    q = _dot(_rms(x, g_ref[...]).astype(BF16), wq_ref[...]) * (dh ** -0.5)
    outs = []
    for h in range(XA_HEADS):
        c = slice(h * dh, (h + 1) * dh)
        s = _dot_nt(q[:, c].astype(BF16), k_ref[0, :, c])
        e = jnp.exp(s - jnp.max(s, axis=-1, keepdims=True))
        p = e * (1.0 / jnp.sum(e, axis=-1, keepdims=True))
        outs.append(_dot(p.astype(BF16), v_ref[0, :, c]))
    o = jnp.concatenate(outs, axis=1)
    o_ref[...] = x + _dot(o.astype(BF16), wo_ref[...])


def _xattn(x2, g, wq, memkv, wo, B, S, tm):
    T, D = x2.shape
    ns = S // tm
    M = memkv.shape[1]
    return pl.pallas_call(
        _xattn_kernel,
        grid=(B, ns),
        in_specs=[pl.BlockSpec((tm, D), lambda b, i: (b * ns + i, 0)), _full((1, D)), _full((D, D)),
                  pl.BlockSpec((1, M, D), lambda b, i: (b, 0, 0)),
                  pl.BlockSpec((1, M, D), lambda b, i: (b, 0, 1)),
                  _full((D, D))],
        out_specs=pl.BlockSpec((tm, D), lambda b, i: (b * ns + i, 0)),
        out_shape=jax.ShapeDtypeStruct((T, D), F32),
        compiler_params=_params("parallel", "parallel"),
    )(x2, g, wq, memkv, memkv, wo)


PAIR_LIST = tuple((a, b) for a in range(PEER_TOPK) for b in range(PEER_TOPK) if (a + 1) * (b + 1) <= PEER_TOPK)
N_PAIR_ROWS = -(-len(PAIR_LIST) // 8) * 8
PAIR_COUNT = tuple(PEER_TOPK // (a + 1) for a in range(PEER_TOPK))
PAIR_START = tuple(sum(PAIR_COUNT[:a]) for a in range(PEER_TOPK))


def _top_rows_exact(v, n):
    R = v.shape[0]
    ridx = lax.broadcasted_iota(jnp.int32, v.shape, 0).astype(F32)
    rank = jnp.full(v.shape, float(n), F32)
    vals = []
    for r in range(n):
        m = jnp.max(v, axis=0, keepdims=True)
        first = jnp.min(jnp.where(v == m, ridx, float(R)), axis=0, keepdims=True)
        taken = ridx == first
        v = jnp.where(taken, LOWEST, v)
        rank = jnp.where(taken, float(r), rank)
        vals.append(m)
    return vals, rank


def _top_rows_distinct(v, n):
    rank = jnp.full(v.shape, float(n), F32)
    vals = []
    for r in range(n):
        m = jnp.max(v, axis=0, keepdims=True)
        taken = v == m
        v = jnp.where(taken, LOWEST, v)
        rank = jnp.where(taken, float(r), rank)
        vals.append(m)
    return vals, rank


def _peer_route_kernel(x_ref, g_ref, wqt_ref, sk_ref, hn_ref, c1_ref, g1_ref, r2_ref, g2_ref, qt_sc, s_sc, top_sc,
                       rank_sc, cand_sc, cw_sc):
    hn = _rms(x_ref[...], g_ref[...]).astype(BF16)
    hn_ref[...] = hn
    qt_sc[...] = _dot_nt(wqt_ref[...], hn).astype(BF16)
    nk, n = PEER_KEYS, PEER_TOPK
    n_lists = 2 * PEER_HEADS

    def one_list(li, carry):
        r0 = pl.multiple_of(li * nk, nk)
        s = _dot(sk_ref[li % 2], qt_sc[pl.ds(r0, nk), :])
        s_sc[pl.ds(r0, nk), :] = s

        def put(vals, rank):
            for r in range(n):
                top_sc[pl.ds(li * n + r, 1), :] = vals[r]
            rank_sc[pl.ds(r0, nk), :] = rank

        vals, rank = _top_rows_distinct(s, n)
        put(vals, rank)
        ranked = jnp.sum((rank < float(n)).astype(F32), axis=0, keepdims=True)

        @pl.when(jnp.max(ranked) > float(n))
        def _():
            put(*_top_rows_exact(s_sc[pl.ds(r0, nk), :], n))

        return carry

    lax.fori_loop(0, n_lists, one_list, 0)

    cand_sc[...] = jnp.full(cand_sc.shape, LOWEST, F32)
    cw_sc[...] = jnp.zeros(cw_sc.shape, F32)
    for h in range(PEER_HEADS):
        l1, l2 = 2 * h, 2 * h + 1
        top1 = [top_sc[l1 * n + a:l1 * n + a + 1, :] for a in range(n)]
        top2 = [top_sc[l2 * n + a:l2 * n + a + 1, :] for a in range(n)]
        e1 = [jnp.exp(t - top1[0]) for t in top1]
        e2 = [jnp.exp(t - top2[0]) for t in top2]
        for r, (a, b) in enumerate(PAIR_LIST):
            cand_sc[h, r:r + 1, :] = top1[a] + top2[b]
            cw_sc[h, r:r + 1, :] = e1[a] * e2[b]
        _, crank = _top_rows_exact(cand_sc[h], n)
        chosen = (crank < float(n)).astype(F32)
        z = jnp.sum(chosen * cw_sc[h], axis=0, keepdims=True)
        rank1 = rank_sc[l1 * nk:(l1 + 1) * nk, :]
        count1 = jnp.zeros(rank1.shape, F32)
        for a in range(n):
            n_a = jnp.sum(chosen[PAIR_START[a]:PAIR_START[a] + PAIR_COUNT[a]], axis=0, keepdims=True)
            count1 = jnp.where(rank1 == float(a), n_a, count1)
        rows = slice(h * nk, (h + 1) * nk)
        c1_ref[rows, :] = count1
        g1_ref[rows, :] = jnp.exp(s_sc[l1 * nk:(l1 + 1) * nk, :] - top1[0]) * (1.0 / z)
        r2_ref[rows, :] = rank_sc[l2 * nk:(l2 + 1) * nk, :].astype(BF16)
        g2_ref[rows, :] = jnp.exp(s_sc[l2 * nk:(l2 + 1) * nk, :] - top2[0]).astype(BF16)


def _peer_route(x2, g, wqt, sk, tm):
    T, D = x2.shape
    nr = PEER_HEADS * PEER_KEYS
    col = pl.BlockSpec((nr, tm), lambda i: (0, i))
    return pl.pallas_call(
        _peer_route_kernel,
        grid=(T // tm,),
        in_specs=[pl.BlockSpec((tm, D), lambda i: (i, 0)), _full((1, D)), _full(wqt.shape), _full(sk.shape)],
        out_specs=[pl.BlockSpec((tm, D), lambda i: (i, 0)), col, col, col, col],
        out_shape=[jax.ShapeDtypeStruct((T, D), BF16), jax.ShapeDtypeStruct((nr, T), F32),
                   jax.ShapeDtypeStruct((nr, T), F32), jax.ShapeDtypeStruct((nr, T), BF16),
                   jax.ShapeDtypeStruct((nr, T), BF16)],
        scratch_shapes=[pltpu.VMEM((2 * nr, tm), BF16), pltpu.VMEM((2 * nr, tm), F32),
                        pltpu.VMEM((2 * PEER_HEADS * PEER_TOPK, tm), F32),
                        pltpu.VMEM((2 * nr, tm), F32), pltpu.VMEM((PEER_HEADS, N_PAIR_ROWS, tm), F32),
                        pltpu.VMEM((PEER_HEADS, N_PAIR_ROWS, tm), F32)],
        compiler_params=_params("parallel"),
    )(x2, g, wqt, sk)


def _peer_expert_kernel(*refs, subs, final_norm):
    hn_ref, c1_ref, g1_ref, r2_ref, g2_ref, u_ref, vt_ref, x_ref = refs[:8]
    fg_ref = refs[8] if final_norm else None
    o_ref, acc_sc, wa_sc, act_sc = refs[-4:]
    c = pl.program_id(1)
    nk = PEER_KEYS
    te = sum(subs)
    offs = [sum(subs[:j]) for j in range(len(subs))]
    tm = hn_ref.shape[0]

    @pl.when(c == 0)
    def _():
        acc_sc[...] = jnp.zeros(acc_sc.shape, F32)

    def pre_act(j):
        o, rows = offs[j], subs[j]
        act_sc[j % 2, 0:rows, :] = _dot_nt(u_ref[o:o + rows, :], hn_ref[...]).astype(BF16)

    pre_act(0)
    for j, (o, rows) in enumerate(zip(offs, subs)):
        if j + 1 < len(subs):
            pre_act(j + 1)
        for il in range(rows // nk):
            i = c * (te // nk) + o // nk + il
            w = jnp.zeros((nk, tm), BF16)
            for h in range(PEER_HEADS):
                count = c1_ref[pl.ds(h * nk + i, 1), :].astype(BF16)
                g1row = g1_ref[pl.ds(h * nk + i, 1), :].astype(BF16)
                hr = slice(h * nk, (h + 1) * nk)
                w = w + jnp.where(r2_ref[hr, :] < count, g2_ref[hr, :], 0.0) * g1row
            er = slice(il * nk, (il + 1) * nk)
            wa_sc[o + il * nk:o + (il + 1) * nk, :] = w * _gelu(act_sc[j % 2, er, :])
        acc_sc[...] += _dot(vt_ref[:, o:o + rows], wa_sc[o:o + rows, :])

    @pl.when(c == pl.num_programs(1) - 1)
    def _():
        out = x_ref[...] + acc_sc[...].T
        o_ref[...] = _rms(out, fg_ref[...]) if final_norm else out


def _peer_expert(hn, c1, g1, r2, g2, u, vt, layer, x2, final_g, tm, subs):
    T, D = x2.shape
    ne = u.shape[1]
    te = sum(subs)
    tok = pl.BlockSpec((c1.shape[0], tm), lambda i, c: (0, i))
    row = pl.BlockSpec((tm, D), lambda i, c: (i, 0))
    return pl.pallas_call(
        functools.partial(_peer_expert_kernel, subs=subs, final_norm=final_g is not None),
        grid=(T // tm, ne // te),
        in_specs=[row, tok, tok, tok, tok,
                  pl.BlockSpec((None, te, D), lambda i, c: (layer, c, 0)),
                  pl.BlockSpec((None, D, te), lambda i, c: (layer, 0, c)),
                  row] + ([] if final_g is None else [_full((1, D))]),
        out_specs=row,
        out_shape=jax.ShapeDtypeStruct((T, D), F32),
        scratch_shapes=[pltpu.VMEM((D, tm), F32), pltpu.VMEM((te, tm), BF16),
                        pltpu.VMEM((2, max(subs), tm), BF16)],
        compiler_params=_params("parallel", "arbitrary"),
    )(hn, c1, g1, r2, g2, u, vt, x2, *([] if final_g is None else [final_g]))


def _rope_tables(positions):
    B, S = positions.shape
    freqs = ROPE_THETA ** (-jnp.arange(ROT_HALF, dtype=F32) * 2.0 / ROT_DIM)
    ang = positions.astype(F32)[:, :, None] * freqs
    cos, sin = jnp.cos(ang), jnp.sin(ang)
    ones = jnp.ones((B, S, HEAD_DIM - ROT_DIM), F32)
    zeros8 = jnp.zeros((B, S, ROT_HALF), F32)
    zeros = jnp.zeros((B, S, HEAD_DIM - ROT_DIM), F32)
    cos_h = jnp.concatenate([cos, cos, ones], axis=-1)
    sa_h = jnp.concatenate([zeros8, sin, zeros], axis=-1)
    sb_h = jnp.concatenate([-sin, zeros8, zeros], axis=-1)
    rep = LANES // HEAD_DIM
    tile = lambda t: jnp.tile(t, (1, 1, rep)).reshape(B * S, LANES)
    return tile(cos_h), tile(sa_h), tile(sb_h)


def _overlap_matrix(n_chunk, n_sel):
    ci = np.arange(n_chunk)[:, None] * CMP_STRIDE
    sj = np.arange(LANES)[None, :] * SEL_BLOCK
    ov = (ci < sj + SEL_BLOCK) & (ci + CMP_BLOCK > sj) & (np.arange(LANES)[None, :] < n_sel)
    ov = ov & (np.arange(n_chunk)[:, None] < n_chunk - 1)
    return jnp.asarray(ov, BF16)


def kernel(x, mem, positions, norm_mix_g, w_in, cmp_pe, cmp_w1, cmp_w2, conv_w, conv_b, conv_ln_g, conv_ln_b, conv_pw_w, conv_pw_b, pool_w, pool_scale, w_out, norm_xa_g, norm_mem_g, xa_wq, xa_wkv, xa_wo, norm_ffn_g, peer_wq, peer_subkeys, peer_u, peer_v, final_g):
    B, S, D = x.shape
    T = B * S
    depth = w_in.shape[0]
    M = mem.shape[1]
    dc = conv_w.shape[-1]
    n_chunk = S // CMP_STRIDE
    n_sel = S // SEL_BLOCK
    assert S % max(TM_PROJ, TS_MIX, KC_SEL, TM_PEER) == 0 and SEL_TOPN <= n_sel <= LANES and D % LANES == 0
    assert S >= WINDOW + TQ_NSA and peer_u.shape[1] % sum(EXPERT_SUBS) == 0

    cos_t, sa_t, sb_t = _rope_tables(positions)
    overlap = _overlap_matrix(n_chunk, n_sel)
    row = lambda v: v.reshape(1, -1)
    x2 = x.reshape(T, D)
    mem2 = mem.reshape(B * M, D)
    n_gate = 3 * NSA_HEADS

    w_pad = jnp.concatenate([w_in[:, :, :D_QKV + n_gate], jnp.zeros((depth, D, LANES - n_gate), F32),
                             w_in[:, :, D_QKV + n_gate:]], axis=2).astype(BF16)
    pg = dc // len(POOL_WINDOWS)
    plw = jnp.zeros((depth, dc, dc), F32)
    for gi in range(len(POOL_WINDOWS)):
        plw = plw.at[:, gi * pg:(gi + 1) * pg, gi * pg:(gi + 1) * pg].set(pool_w[:, gi])
    plw, conv_pw, wo = plw.astype(BF16), conv_pw_w.astype(BF16), w_out.astype(BF16)
    wkv, wq, wxo = xa_wkv.astype(BF16), xa_wq.astype(BF16), xa_wo.astype(BF16)
    pwq_t = jnp.swapaxes(peer_wq, 1, 2).astype(BF16)
    psk, pu = peer_subkeys.astype(BF16), peer_u.astype(BF16)
    pv_t = jnp.swapaxes(peer_v, 1, 2).astype(BF16)

    for l in range(depth):
        q, kvc, kv, misc = _inproj(x2, row(norm_mix_g[l]), w_pad[l], cos_t, sa_t, sb_t, tm=TM_PROJ)
        xc = kvc.reshape(B, n_chunk, CMP_STRIDE * 2 * LANES)
        cmp = _compress(xc, *_compress_weights(cmp_pe[l], cmp_w1[l], cmp_w2[l]))
        o_nsa = _nsa(q, cmp[0], cmp[1], kv.reshape(B, S, -1), misc, overlap, B, S, tq=TQ_NSA)
        x2 = _mixout(misc, o_nsa, x2, conv_w[l], row(conv_b[l]), row(conv_ln_g[l]), row(conv_ln_b[l]),
                     conv_pw[l], row(conv_pw_b[l]), plw[l], row(pool_scale[l]), wo[l], B, S, ts=TS_MIX)
        memkv = _norm_matmul(mem2, row(norm_mem_g[l]), wkv[l], tm=TM_MEM, out_dtype=BF16)
        x2 = _xattn(x2, row(norm_xa_g[l]), wq[l], memkv.reshape(B, M, 2 * D), wxo[l], B, S, tm=TM_PROJ)
        hn, c1, g1, r2, g2 = _peer_route(x2, row(norm_ffn_g[l]), pwq_t[l], psk[l], tm=TM_PEER)
        x2 = _peer_expert(hn, c1, g1, r2, g2, pu, pv_t, l, x2,
                          row(final_g) if l == depth - 1 else None, tm=TM_PEER, subs=EXPERT_SUBS)
    return x2.reshape(B, S, D)
```

```python
import functools
import math

import jax
import jax.numpy as jnp
import numpy as np
from jax import lax
from jax.experimental import pallas as pl
from jax.experimental.pallas import tpu as pltpu

F32 = jnp.float32
BF16 = jnp.bfloat16

NSA_HEADS = 8
NSA_KV_HEADS = 2
NSA_GROUP = NSA_HEADS // NSA_KV_HEADS
HEAD_DIM = 64
D_NSA = NSA_HEADS * HEAD_DIM
D_KV = NSA_KV_HEADS * HEAD_DIM
ROT_DIM = HEAD_DIM // 4
ROT_HALF = ROT_DIM // 2
ROPE_THETA = 500000.0
CMP_BLOCK = 32
CMP_STRIDE = 16
SEL_BLOCK = 64
SEL_TOPN = 16
WINDOW = 512
CONV_WIDTH = 31
POOL_WINDOWS = (2, 4, 8, 16)
XA_HEADS = 4
PEER_HEADS = 8
PEER_KEYS = 128
PEER_TOPK = 16
EPS = 1e-6
NEG = -1e30
FORCE = 1e4
LOWEST = -3.0e38

LANES = 128
SUBLANES = 8
VMEM_LIMIT = 56 * 1024 * 1024

NT_DIMS = (((1,), (1,)), ((), ()))

TM_PROJ = 512
TM_MEM = 256
TQ_NSA = 256
KC_SEL = 512
TS_MIX = 512
TM_PEER = 512
EXPERT_SUBS = (256, 768, 768, 256)


def _params(*sem):
    return pltpu.CompilerParams(dimension_semantics=sem, vmem_limit_bytes=VMEM_LIMIT)


def _full(shape):
    nd = len(shape)
    return pl.BlockSpec(shape, lambda *_: (0,) * nd)


def _rms(x, g):
    return x * lax.rsqrt(jnp.mean(x * x, axis=-1, keepdims=True) + EPS) * g


def _gelu(x):
    c = math.sqrt(2.0 / math.pi)
    return 0.5 * x * (1.0 + jnp.tanh(c * (x + 0.044715 * (x * x * x))))


def _dot(a, b):
    return jnp.dot(a, b, preferred_element_type=F32)


def _dot_nt(a, b):
    return lax.dot_general(a, b, NT_DIMS, preferred_element_type=F32)


N_ROPE_Q = D_NSA // LANES
N_KV_CHUNKS = 6
D_QKV = D_NSA + N_KV_CHUNKS * D_KV
N_KV_OUT = 2 + 2 * NSA_KV_HEADS


def _inproj_kernel(x_ref, g_ref, w_ref, cos_ref, sa_ref, sb_ref, q_ref, kvc_ref, kv_ref, misc_ref):
    y = _rms(x_ref[...], g_ref[...])
    z = _dot(y.astype(BF16), w_ref[...])
    cos, sa, sb = cos_ref[...], sa_ref[...], sb_ref[...]

    def rope(c):
        return c * cos + pltpu.roll(c, ROT_HALF, 1) * sa + pltpu.roll(c, LANES - ROT_HALF, 1) * sb

    scale = HEAD_DIM ** -0.5
    for j in range(N_ROPE_Q):
        q_ref[:, j * LANES:(j + 1) * LANES] = (rope(z[:, j * LANES:(j + 1) * LANES]) * scale).astype(BF16)
    lane = lax.broadcasted_iota(jnp.int32, (z.shape[0], LANES), 1)
    out = 0
    for j in range(N_KV_CHUNKS):
        c = z[:, D_NSA + j * LANES:D_NSA + (j + 1) * LANES]
        if j % 2 == 0:
            c = rope(c)
        if j < 2:
            kvc_ref[:, j * LANES:(j + 1) * LANES] = c.astype(BF16)
        elif j % 2 == 0:
            kv_ref[:, out * LANES:(out + 1) * LANES] = c.astype(BF16)
            out += 1
    for j in (3, 5):
        c = z[:, D_NSA + j * LANES:D_NSA + (j + 1) * LANES]
        for k in range(NSA_KV_HEADS):
            own = (lane >= HEAD_DIM) if k == 1 else (lane < HEAD_DIM)
            kv_ref[:, out * LANES:(out + 1) * LANES] = jnp.where(own, c, 1.0).astype(BF16)
            out += 1
    misc_ref[...] = z[:, D_QKV:]


def _inproj(x2, g, w_pad, cos_t, sa_t, sb_t, tm):
    T, D = x2.shape
    n = w_pad.shape[1]
    n_misc = n - D_QKV
    row = lambda w: pl.BlockSpec((tm, w), lambda i: (i, 0))
    return pl.pallas_call(
        _inproj_kernel,
        grid=(T // tm,),
        in_specs=[row(D), _full((1, D)), _full((D, n)), row(LANES), row(LANES), row(LANES)],
        out_specs=[row(D_NSA), row(2 * LANES), row(N_KV_OUT * LANES), row(n_misc)],
        out_shape=[jax.ShapeDtypeStruct((T, D_NSA), BF16),
                   jax.ShapeDtypeStruct((T, 2 * LANES), BF16),
                   jax.ShapeDtypeStruct((T, N_KV_OUT * LANES), BF16),
                   jax.ShapeDtypeStruct((T, n_misc), F32)],
        compiler_params=_params("parallel"),
    )(x2, g, w_pad, cos_t, sa_t, sb_t)


def _compress_kernel(x_ref, pe_ref, w1_ref, w2_ref, o_ref):
    x = x_ref[0].astype(F32)
    n = x.shape[0]
    out = jnp.zeros((n, LANES), F32)
    for hd in range(NSA_KV_HEADS):
        a = _dot((x + pe_ref[0, hd, 0:1]).astype(BF16), w1_ref[0, hd, 0])
        b = _dot((x + pe_ref[0, hd, 1:2]).astype(BF16), w1_ref[0, hd, 1])
        hid = _gelu(a + pltpu.roll(b, n - 1, 0))
        out = out + _dot(hid.astype(BF16), w2_ref[0, hd])
    o_ref[0, 0] = out.astype(BF16)


def _compress(xc, pe, w1, w2):
    nb, nch, width = xc.shape
    hid = w1.shape[-1]
    return pl.pallas_call(
        _compress_kernel,
        grid=(2, nb),
        in_specs=[pl.BlockSpec((1, nch, width), lambda w, i: (i, 0, 0)),
                  pl.BlockSpec((1, NSA_KV_HEADS, 2, width), lambda w, i: (w, 0, 0, 0)),
                  pl.BlockSpec((1, NSA_KV_HEADS, 2, width, hid), lambda w, i: (w, 0, 0, 0, 0)),
                  pl.BlockSpec((1, NSA_KV_HEADS, hid, LANES), lambda w, i: (w, 0, 0, 0))],
        out_specs=pl.BlockSpec((1, 1, nch, LANES), lambda w, i: (w, i, 0, 0)),
        out_shape=jax.ShapeDtypeStruct((2, nb, nch, LANES), BF16),
        compiler_params=_params("parallel", "parallel"),
    )(xc, pe, w1, w2)


def _compress_weights(pe, w1, w2):
    hid = w1.shape[-1]
    eye_t = jnp.eye(2, dtype=F32)[:, None, None, None, :, None, None]
    eye_h = jnp.eye(NSA_KV_HEADS, dtype=F32)[None, :, None, None, None, :, None]
    per = (2, 1, 2, CMP_STRIDE, 1, 1, HEAD_DIM)
    width = CMP_STRIDE * 2 * NSA_KV_HEADS * HEAD_DIM
    pe_big = (pe.reshape(per) * eye_t * eye_h).reshape(2, NSA_KV_HEADS, 2, width)
    w1_big = (w1.reshape(per + (hid,)) * eye_t[..., None] * eye_h[..., None]).reshape(2, NSA_KV_HEADS, 2, width, hid)
    eye_o = jnp.eye(NSA_KV_HEADS, dtype=F32)[None, :, None, :, None]
    w2_big = (w2[:, None, :, None, :] * eye_o).reshape(2, NSA_KV_HEADS, hid, NSA_KV_HEADS * HEAD_DIM)
    return pe_big, w1_big.astype(BF16), w2_big.astype(BF16)


def _nsa_kernel(q_ref, kcmp_ref, vcmp_ref, ks_ref, kw_ref, vs0_ref, vs1_ref, vw0_ref, vw1_ref, gate_ref, ov_ref,
                o_ref, s_sc, mrun_sc, acc_sc, *, tq, kc_sel, kc_win):
    t0 = pl.program_id(1) * tq
    rows = NSA_GROUP * tq
    lane = lax.broadcasted_iota(jnp.int32, (tq, LANES), 1)
    lane_r = lax.broadcasted_iota(jnp.int32, (rows, LANES), 1)
    tpos = t0 + lax.broadcasted_iota(jnp.int32, (tq, 1), 0)
    gates = jax.nn.sigmoid(gate_ref[...])
    n_cmp = kcmp_ref.shape[1]
    n_sel = ks_ref.shape[1] // SEL_BLOCK
    vs_refs, vw_refs = (vs0_ref, vs1_ref), (vw0_ref, vw1_ref)

    def normalise(acc, own):
        return acc / jnp.where(own, pltpu.roll(acc, HEAD_DIM, 1), 1.0)

    def own_lanes(lanes, k):
        return (lanes >= HEAD_DIM) if k == 1 else (lanes < HEAD_DIM)

    def queries(k):
        parts = []
        for g in range(NSA_GROUP):
            hh = k * NSA_GROUP + g
            c = q_ref[:, (hh // 2) * LANES:(hh // 2 + 1) * LANES].astype(F32)
            if hh % 2 != k:
                c = pltpu.roll(c, HEAD_DIM, 1)
            parts.append(jnp.where(own_lanes(lane, k), c, 0.0))
        return jnp.concatenate(parts, axis=0).astype(BF16)

    def compressed(qs):
        s = _dot_nt(qs, kcmp_ref[0]).reshape(NSA_GROUP, tq, n_cmp)
        cmp_end = lax.broadcasted_iota(jnp.int32, (tq, n_cmp), 1) * CMP_STRIDE + (CMP_BLOCK - 1)
        valid = cmp_end <= tpos
        s = s + jnp.where(valid, 0.0, NEG)[None]
        e = jnp.exp(s - jnp.max(s, axis=-1, keepdims=True)) * valid.astype(F32)[None]
        l = jnp.sum(e, axis=-1, keepdims=True)
        p3 = e * (1.0 / jnp.where(l > 0.0, l, 1.0))
        o_cmp = _dot(p3.reshape(rows, n_cmp).astype(BF16), vcmp_ref[0])

        psum = p3[0] + p3[1] + p3[2] + p3[3]
        ov = ov_ref[...]
        p_hi = psum.astype(BF16)
        r1 = psum - p_hi.astype(F32)
        p_mid = r1.astype(BF16)
        p_lo = (r1 - p_mid.astype(F32)).astype(BF16)
        imp = _dot(p_hi, ov) + _dot(p_mid, ov) + _dot(p_lo, ov)
        imp = imp.T[:n_sel]
        blk = lax.broadcasted_iota(jnp.int32, (n_sel, tq), 0)
        tpos_t = t0 + lax.broadcasted_iota(jnp.int32, (1, tq), 1)
        cur = tpos_t // SEL_BLOCK
        forced = (blk == 0) | (blk == cur) | (blk == cur - 1)
        imp = jnp.where(blk * SEL_BLOCK <= tpos_t, jnp.where(forced, FORCE, imp), NEG)
        sub8 = lax.broadcasted_iota(jnp.int32, (SUBLANES, tq), 0)
        groups = [imp[r:r + SUBLANES] for r in range(0, n_sel, SUBLANES)]
        ranks = [jnp.zeros((SUBLANES, tq), jnp.int32) for _ in groups]
        for jp in range(n_sel):
            one = imp[jp:jp + 1, :]
            for gi, grp in enumerate(groups):
                lo = gi * SUBLANES
                if lo > jp:
                    beats = one >= grp
                elif lo + SUBLANES - 1 < jp:
                    beats = one > grp
                else:
                    beats = (one > grp) | ((one == grp) & (sub8 > jp - lo))
                ranks[gi] = ranks[gi] + beats.astype(jnp.int32)
        rank = jnp.concatenate(ranks, axis=0)
        sel_t = jnp.concatenate([(rank < SEL_TOPN).astype(F32), jnp.zeros((LANES - n_sel, tq), F32)], axis=0)
        return o_cmp, sel_t.T.astype(BF16)

    def window(k, qs):
        w0 = pl.multiple_of(jnp.maximum(t0 + tq - kc_win, 0), tq)
        sw = _dot_nt(qs, kw_ref[0, pl.ds(w0, kc_win), :]).reshape(NSA_GROUP, tq, kc_win)
        wpos = w0 + lax.broadcasted_iota(jnp.int32, (tq, kc_win), 1)
        in_win = (wpos <= tpos) & (wpos > tpos - WINDOW)
        sw = (sw + jnp.where(in_win, 0.0, NEG)[None]).reshape(rows, kc_win)
        pw = jnp.exp(sw - jnp.max(sw, axis=1, keepdims=True))
        return normalise(_dot(pw.astype(BF16), vw_refs[k][0, pl.ds(w0, kc_win), :]), own_lanes(lane_r, k))

    heads = range(NSA_KV_HEADS)
    qs = [queries(k) for k in heads]
    o_cmp, sel = zip(*[compressed(qs[k]) for k in heads])
    o_win = [window(k, qs[k]) for k in heads]

    n_chunks = (t0 + tq + kc_sel - 1) // kc_sel
    mrun_sc[...] = jnp.full(mrun_sc.shape, NEG, F32)

    def scores(c, carry):
        k0 = pl.multiple_of(c * kc_sel, kc_sel)
        jrow = lax.broadcasted_iota(jnp.int32, (LANES, kc_sel), 0)
        kcol = lax.broadcasted_iota(jnp.int32, (LANES, kc_sel), 1)
        expand = (jrow == k0 // SEL_BLOCK + kcol // SEL_BLOCK).astype(BF16)
        causal = k0 + lax.broadcasted_iota(jnp.int32, (tq, kc_sel), 1) <= tpos
        kb = ks_ref[0, pl.ds(k0, kc_sel), :]
        for k in heads:
            seen = (_dot(sel[k], expand) > 0.5) & causal
            s = _dot_nt(qs[k], kb).reshape(NSA_GROUP, tq, kc_sel)
            s = (s + jnp.where(seen, 0.0, NEG)[None]).reshape(rows, kc_sel)
            s_sc[k, c] = s
            m = mrun_sc[k]
            for j in range(kc_sel // LANES):
                m = jnp.maximum(m, s[:, j * LANES:(j + 1) * LANES])
            mrun_sc[k] = m
        return carry

    lax.fori_loop(0, n_chunks, scores, 0)
    m_sel = [jnp.broadcast_to(jnp.max(mrun_sc[k], axis=1, keepdims=True), (rows, LANES)) for k in heads]
    acc_sc[...] = jnp.zeros(acc_sc.shape, F32)

    def weigh(c, carry):
        k0 = pl.multiple_of(c * kc_sel, kc_sel)
        for k in heads:
            s = s_sc[k, c]
            p = jnp.concatenate([jnp.exp(s[:, j * LANES:(j + 1) * LANES] - m_sel[k])
                                 for j in range(kc_sel // LANES)], axis=1)
            acc_sc[k] += _dot(p.astype(BF16), vs_refs[k][0, pl.ds(k0, kc_sel), :])
        return carry

    lax.fori_loop(0, n_chunks, weigh, 0)

    for k in heads:
        o_sel = normalise(acc_sc[k], own_lanes(lane_r, k))
        outs = []
        for g in range(NSA_GROUP):
            r = slice(g * tq, (g + 1) * tq)
            gi = (k * NSA_GROUP + g) * 3
            og = (gates[:, gi:gi + 1] * o_cmp[k][r] + gates[:, gi + 1:gi + 2] * o_sel[r]
                  + gates[:, gi + 2:gi + 3] * o_win[k][r])
            if g % 2 != k:
                og = pltpu.roll(og, HEAD_DIM, 1)
            outs.append(og)
        for j in range(NSA_GROUP // 2):
            chunk = jnp.where(lane < HEAD_DIM, outs[2 * j], outs[2 * j + 1])
            cj = k * (NSA_GROUP // 2) + j
            o_ref[:, cj * LANES:(cj + 1) * LANES] = chunk.astype(BF16)


def _nsa(q, kcmp, vcmp, kv, gates_misc, overlap, B, S, tq):
    kc_sel, kc_win = KC_SEL, WINDOW + tq
    nq = S // tq
    n_cmp = kcmp.shape[1]
    rows = NSA_GROUP * tq
    kvspec = lambda j: pl.BlockSpec((1, S, LANES), lambda b, i: (b, 0, j), pipeline_mode=pl.Buffered(1))
    return pl.pallas_call(
        functools.partial(_nsa_kernel, tq=tq, kc_sel=kc_sel, kc_win=kc_win),
        grid=(B, nq),
        in_specs=[pl.BlockSpec((tq, D_NSA), lambda b, i: (b * nq + i, 0)),
                  pl.BlockSpec((1, n_cmp, LANES), lambda b, i: (b, 0, 0)),
                  pl.BlockSpec((1, n_cmp, LANES), lambda b, i: (b, 0, 0)),
                  kvspec(0), kvspec(1), kvspec(2), kvspec(3), kvspec(4), kvspec(5),
                  pl.BlockSpec((tq, LANES), lambda b, i: (b * nq + i, 0)),
                  _full(overlap.shape)],
        out_specs=pl.BlockSpec((tq, D_NSA), lambda b, i: (b * nq + i, 0)),
        out_shape=jax.ShapeDtypeStruct((B * S, D_NSA), BF16),
        scratch_shapes=[pltpu.VMEM((NSA_KV_HEADS, S // kc_sel, rows, kc_sel), F32),
                        pltpu.VMEM((NSA_KV_HEADS, rows, LANES), F32), pltpu.VMEM((NSA_KV_HEADS, rows, LANES), F32)],
        compiler_params=_params("parallel", "parallel"),
    )(q, kcmp, vcmp, kv, kv, kv, kv, kv, kv, gates_misc, overlap)


HALO = 32


def _mixout_kernel(mc_ref, mp_ref, on_ref, x_ref, cw_ref, cb_ref, lg_ref, lb_ref, pw_ref, pb_ref, plw_ref,
                   pls_ref, wo_ref, o_ref, hbuf, pbuf, hsh, *, ts, dc):
    i = pl.program_id(1)
    first = i == 0
    off_a, off_b, off_p = LANES, LANES + dc, LANES + 2 * dc

    def glu(ref, r):
        return ref[r, off_a:off_a + dc] * jax.nn.sigmoid(ref[r, off_b:off_b + dc])

    tail = slice(ts - HALO, ts)
    hbuf[0:HALO, :] = jnp.where(first, 0.0, glu(mp_ref, tail))
    hbuf[HALO:, :] = glu(mc_ref, slice(None))
    pbuf[0:HALO, :] = jnp.where(first, 0.0, mp_ref[tail, off_p:off_p + dc])
    pcur = mc_ref[:, off_p:off_p + dc]
    pbuf[HALO:, :] = pcur

    shifted = ts + HALO - SUBLANES
    for r in range(1, SUBLANES):
        hsh[r, 0:shifted, :] = hbuf[pl.ds(r, shifted), :]
    acc = jnp.zeros((ts, dc), F32) + cb_ref[...]
    for w in range(CONV_WIDTH):
        q, r = divmod(HALO - (CONV_WIDTH - 1) + w, SUBLANES)
        tap = hbuf[pl.ds(q * SUBLANES, ts), :] if r == 0 else hsh[r, pl.ds(q * SUBLANES, ts), :]
        acc = acc + tap * cw_ref[w:w + 1, :]
    mu = jnp.mean(acc, axis=-1, keepdims=True)
    xc = acc - mu
    y = xc * lax.rsqrt(jnp.mean(xc * xc, axis=-1, keepdims=True) + EPS) * lg_ref[...] + lb_ref[...]
    y = y * jax.nn.sigmoid(y)
    o_conv = _dot(y.astype(BF16), pw_ref[...]) + pb_ref[...]

    tglob = i * ts + lax.broadcasted_iota(jnp.int32, (ts, dc), 0)
    lane = lax.broadcasted_iota(jnp.int32, (ts, dc), 1)
    pg = dc // len(POOL_WINDOWS)
    run = pcur
    d = 1
    mean = jnp.zeros((ts, dc), F32)
    for gi, w in enumerate(POOL_WINDOWS):
        while d < w:
            run = run + pbuf[pl.ds(HALO - d, ts), :]
            d += 1
        cnt = jnp.minimum(tglob + 1, w).astype(F32)
        mean = jnp.where(lane // pg == gi, run / cnt, mean)
    o_pool = _dot((mean - pcur).astype(BF16), plw_ref[...]) * pls_ref[...]

    dn = on_ref.shape[1]
    o_ref[...] = (x_ref[...] + _dot(on_ref[...], wo_ref[0:dn, :])
                  + _dot(o_conv.astype(BF16), wo_ref[dn:dn + dc, :])
                  + _dot(o_pool.astype(BF16), wo_ref[dn + dc:, :]))


def _mixout(misc, o_nsa, x2, cw, cb, lg, lb, pw, pb, plw, pls, wo, B, S, ts):
    T, D = x2.shape
    dc = cw.shape[1]
    ns = S // ts
    nm = misc.shape[1]
    cur = lambda w: pl.BlockSpec((ts, w), lambda b, i: (b * ns + i, 0))
    prev = pl.BlockSpec((ts, nm), lambda b, i: (b * ns + jnp.maximum(i - 1, 0), 0))
    return pl.pallas_call(
        functools.partial(_mixout_kernel, ts=ts, dc=dc),
        grid=(B, ns),
        in_specs=[cur(nm), prev, cur(o_nsa.shape[1]), cur(D), _full(cw.shape), _full(cb.shape), _full(lg.shape),
                  _full(lb.shape), _full(pw.shape), _full(pb.shape), _full(plw.shape), _full(pls.shape),
                  _full(wo.shape)],
        out_specs=cur(D),
        out_shape=jax.ShapeDtypeStruct((T, D), F32),
        scratch_shapes=[pltpu.VMEM((ts + HALO, dc), F32), pltpu.VMEM((ts + HALO, dc), F32),
                        pltpu.VMEM((SUBLANES, ts + HALO, dc), F32)],
        compiler_params=_params("parallel", "parallel"),
    )(misc, misc, o_nsa, x2, cw, cb, lg, lb, pw, pb, plw, pls, wo)


def _norm_matmul_kernel(x_ref, g_ref, w_ref, o_ref):
    o_ref[...] = _dot(_rms(x_ref[...], g_ref[...]).astype(BF16), w_ref[...]).astype(o_ref.dtype)


def _norm_matmul(x2, g, w, tm, out_dtype):
    T, D = x2.shape
    n = w.shape[1]
    return pl.pallas_call(
        _norm_matmul_kernel,
        grid=(T // tm,),
        in_specs=[pl.BlockSpec((tm, D), lambda i: (i, 0)), _full((1, D)), _full((D, n))],
        out_specs=pl.BlockSpec((tm, n), lambda i: (i, 0)),
        out_shape=jax.ShapeDtypeStruct((T, n), out_dtype),
        compiler_params=_params("parallel"),
    )(x2, g, w)


def _xattn_kernel(x_ref, g_ref, wq_ref, k_ref, v_ref, wo_ref, o_ref):
    x = x_ref[...]
    D = x.shape[1]
    dh = D // XA_HEADS
    q = _dot(_rms(x, g_ref[...]).astype(BF16), wq_ref[...]) * (dh ** -0.5)
    outs = []
    for h in range(XA_HEADS):
        c = slice(h * dh, (h + 1) * dh)
        s = _dot_nt(q[:, c].astype(BF16), k_ref[0, :, c])
        e = jnp.exp(s - jnp.max(s, axis=-1, keepdims=True))
        p = e * (1.0 / jnp.sum(e, axis=-1, keepdims=True))
        outs.append(_dot(p.astype(BF16), v_ref[0, :, c]))
    o = jnp.concatenate(outs, axis=1)
    o_ref[...] = x + _dot(o.astype(BF16), wo_ref[...])


def _xattn(x2, g, wq, memkv, wo, B, S, tm):
    T, D = x2.shape
    ns = S // tm
    M = memkv.shape[1]
    return pl.pallas_call(
        _xattn_kernel,
        grid=(B, ns),
        in_specs=[pl.BlockSpec((tm, D), lambda b, i: (b * ns + i, 0)), _full((1, D)), _full((D, D)),
                  pl.BlockSpec((1, M, D), lambda b, i: (b, 0, 0)),
                  pl.BlockSpec((1, M, D), lambda b, i: (b, 0, 1)),
                  _full((D, D))],
        out_specs=pl.BlockSpec((tm, D), lambda b, i: (b * ns + i, 0)),
        out_shape=jax.ShapeDtypeStruct((T, D), F32),
        compiler_params=_params("parallel", "parallel"),
    )(x2, g, wq, memkv, memkv, wo)


PAIR_LIST = tuple((a, b) for a in range(PEER_TOPK) for b in range(PEER_TOPK) if (a + 1) * (b + 1) <= PEER_TOPK)
N_PAIR_ROWS = -(-len(PAIR_LIST) // 8) * 8
PAIR_COUNT = tuple(PEER_TOPK // (a + 1) for a in range(PEER_TOPK))
PAIR_START = tuple(sum(PAIR_COUNT[:a]) for a in range(PEER_TOPK))


def _top_rows_exact(v, n):
    R = v.shape[0]
    ridx = lax.broadcasted_iota(jnp.int32, v.shape, 0).astype(F32)
    rank = jnp.full(v.shape, float(n), F32)
    vals = []
    for r in range(n):
        m = jnp.max(v, axis=0, keepdims=True)
        first = jnp.min(jnp.where(v == m, ridx, float(R)), axis=0, keepdims=True)
        taken = ridx == first
        v = jnp.where(taken, LOWEST, v)
        rank = jnp.where(taken, float(r), rank)
        vals.append(m)
    return vals, rank


def _top_rows_distinct(v, n):
    rank = jnp.full(v.shape, float(n), F32)
    vals = []
    for r in range(n):
        m = jnp.max(v, axis=0, keepdims=True)
        taken = v == m
        v = jnp.where(taken, LOWEST, v)
        rank = jnp.where(taken, float(r), rank)
        vals.append(m)
    return vals, rank


def _peer_route_kernel(x_ref, g_ref, wqt_ref, sk_ref, hn_ref, c1_ref, g1_ref, r2_ref, g2_ref, qt_sc, s_sc, top_sc,
                       rank_sc, cand_sc, cw_sc):
    hn = _rms(x_ref[...], g_ref[...]).astype(BF16)
    hn_ref[...] = hn
    qt_sc[...] = _dot_nt(wqt_ref[...], hn).astype(BF16)
    nk, n = PEER_KEYS, PEER_TOPK
    tm = x_ref.shape[0]

    def put(li, vals, rank):
        for r in range(n):
            top_sc[pl.ds(li * n + r, 1), :] = vals[r]
        rank_sc[pl.ds(pl.multiple_of(li * nk, nk), nk), :] = rank

    def one_head(h, carry):
        ranked = jnp.zeros((1, tm), F32)
        for half in range(2):
            li = 2 * h + half
            r0 = pl.multiple_of(li * nk, nk)
            s = _dot(sk_ref[half], qt_sc[pl.ds(r0, nk), :])
            s_sc[pl.ds(r0, nk), :] = s
            vals, rank = _top_rows_distinct(s, n)
            put(li, vals, rank)
            ranked = jnp.maximum(ranked, jnp.sum((rank < float(n)).astype(F32), axis=0, keepdims=True))

        @pl.when(jnp.max(ranked) > float(n))
        def _():
            for half in range(2):
                li = 2 * h + half
                put(li, *_top_rows_exact(s_sc[pl.ds(pl.multiple_of(li * nk, nk), nk), :], n))

        return carry

    lax.fori_loop(0, PEER_HEADS, one_head, 0)

    cand_sc[...] = jnp.full(cand_sc.shape, LOWEST, F32)
    cw_sc[...] = jnp.zeros(cw_sc.shape, F32)
    for h in range(PEER_HEADS):
        l1, l2 = 2 * h, 2 * h + 1
        top1 = [top_sc[l1 * n + a:l1 * n + a + 1, :] for a in range(n)]
        top2 = [top_sc[l2 * n + a:l2 * n + a + 1, :] for a in range(n)]
        e1 = [jnp.exp(t - top1[0]) for t in top1]
        e2 = [jnp.exp(t - top2[0]) for t in top2]
        for r, (a, b) in enumerate(PAIR_LIST):
            cand_sc[h, r:r + 1, :] = top1[a] + top2[b]
            cw_sc[h, r:r + 1, :] = e1[a] * e2[b]
        _, crank = _top_rows_exact(cand_sc[h], n)
        chosen = (crank < float(n)).astype(F32)
        z = jnp.sum(chosen * cw_sc[h], axis=0, keepdims=True)
        rank1 = rank_sc[l1 * nk:(l1 + 1) * nk, :].astype(BF16)
        count1 = jnp.zeros(rank1.shape, BF16)
        for a in range(n):
            n_a = jnp.sum(chosen[PAIR_START[a]:PAIR_START[a] + PAIR_COUNT[a]], axis=0, keepdims=True)
            count1 = jnp.where(rank1 == float(a), n_a.astype(BF16), count1)
        rows = slice(h * nk, (h + 1) * nk)
        c1_ref[rows, :] = count1.astype(F32)
        g1_ref[rows, :] = jnp.exp(s_sc[l1 * nk:(l1 + 1) * nk, :] - top1[0]) * (1.0 / z)
        r2_ref[rows, :] = rank_sc[l2 * nk:(l2 + 1) * nk, :].astype(BF16)
        g2_ref[rows, :] = jnp.exp(s_sc[l2 * nk:(l2 + 1) * nk, :] - top2[0]).astype(BF16)


def _peer_route(x2, g, wqt, sk, tm):
    T, D = x2.shape
    nr = PEER_HEADS * PEER_KEYS
    col = pl.BlockSpec((nr, tm), lambda i: (0, i))
    return pl.pallas_call(
        _peer_route_kernel,
        grid=(T // tm,),
        in_specs=[pl.BlockSpec((tm, D), lambda i: (i, 0)), _full((1, D)), _full(wqt.shape), _full(sk.shape)],
        out_specs=[pl.BlockSpec((tm, D), lambda i: (i, 0)), col, col, col, col],
        out_shape=[jax.ShapeDtypeStruct((T, D), BF16), jax.ShapeDtypeStruct((nr, T), F32),
                   jax.ShapeDtypeStruct((nr, T), F32), jax.ShapeDtypeStruct((nr, T), BF16),
                   jax.ShapeDtypeStruct((nr, T), BF16)],
        scratch_shapes=[pltpu.VMEM((2 * nr, tm), BF16), pltpu.VMEM((2 * nr, tm), F32),
                        pltpu.VMEM((2 * PEER_HEADS * PEER_TOPK, tm), F32),
                        pltpu.VMEM((2 * nr, tm), F32), pltpu.VMEM((PEER_HEADS, N_PAIR_ROWS, tm), F32),
                        pltpu.VMEM((PEER_HEADS, N_PAIR_ROWS, tm), F32)],
        compiler_params=_params("parallel"),
    )(x2, g, wqt, sk)


def _peer_expert_kernel(*refs, subs, final_norm):
    hn_ref, c1_ref, g1_ref, r2_ref, g2_ref, u_ref, vt_ref, x_ref = refs[:8]
    fg_ref = refs[8] if final_norm else None
    o_ref, acc_sc, wa_sc, act_sc = refs[-4:]
    c = pl.program_id(1)
    nk = PEER_KEYS
    te = sum(subs)
    offs = [sum(subs[:j]) for j in range(len(subs))]
    tm = hn_ref.shape[0]

    @pl.when(c == 0)
    def _():
        acc_sc[...] = jnp.zeros(acc_sc.shape, F32)

    def pre_act(j):
        o, rows = offs[j], subs[j]
        act_sc[j % 2, 0:rows, :] = _dot_nt(u_ref[o:o + rows, :], hn_ref[...]).astype(BF16)

    pre_act(0)
    for j, (o, rows) in enumerate(zip(offs, subs)):
        if j + 1 < len(subs):
            pre_act(j + 1)
        for il in range(rows // nk):
            i = c * (te // nk) + o // nk + il
            w = jnp.zeros((nk, tm), BF16)
            for h in range(PEER_HEADS):
                count = c1_ref[pl.ds(h * nk + i, 1), :].astype(BF16)
                g1row = g1_ref[pl.ds(h * nk + i, 1), :].astype(BF16)
                hr = slice(h * nk, (h + 1) * nk)
                w = w + jnp.where(r2_ref[hr, :] < count, g2_ref[hr, :], 0.0) * g1row
            er = slice(il * nk, (il + 1) * nk)
            wa_sc[o + il * nk:o + (il + 1) * nk, :] = w * _gelu(act_sc[j % 2, er, :])
        acc_sc[...] += _dot(vt_ref[:, o:o + rows], wa_sc[o:o + rows, :])

    @pl.when(c == pl.num_programs(1) - 1)
    def _():
        out = x_ref[...] + acc_sc[...].T
        o_ref[...] = _rms(out, fg_ref[...]) if final_norm else out


def _peer_expert(hn, c1, g1, r2, g2, u, vt, layer, x2, final_g, tm, subs):
    T, D = x2.shape
    ne = u.shape[1]
    te = sum(subs)
    tok = pl.BlockSpec((c1.shape[0], tm), lambda i, c: (0, i))
    row = pl.BlockSpec((tm, D), lambda i, c: (i, 0))
    return pl.pallas_call(
        functools.partial(_peer_expert_kernel, subs=subs, final_norm=final_g is not None),
        grid=(T // tm, ne // te),
        in_specs=[row, tok, tok, tok, tok,
                  pl.BlockSpec((None, te, D), lambda i, c: (layer, c, 0)),
                  pl.BlockSpec((None, D, te), lambda i, c: (layer, 0, c)),
                  row] + ([] if final_g is None else [_full((1, D))]),
        out_specs=row,
        out_shape=jax.ShapeDtypeStruct((T, D), F32),
        scratch_shapes=[pltpu.VMEM((D, tm), F32), pltpu.VMEM((te, tm), BF16),
                        pltpu.VMEM((2, max(subs), tm), BF16)],
        compiler_params=_params("parallel", "arbitrary"),
    )(hn, c1, g1, r2, g2, u, vt, x2, *([] if final_g is None else [final_g]))


def _rope_tables(positions):
    B, S = positions.shape
    freqs = ROPE_THETA ** (-jnp.arange(ROT_HALF, dtype=F32) * 2.0 / ROT_DIM)
    ang = positions.astype(F32)[:, :, None] * freqs
    cos, sin = jnp.cos(ang), jnp.sin(ang)
    ones = jnp.ones((B, S, HEAD_DIM - ROT_DIM), F32)
    zeros8 = jnp.zeros((B, S, ROT_HALF), F32)
    zeros = jnp.zeros((B, S, HEAD_DIM - ROT_DIM), F32)
    cos_h = jnp.concatenate([cos, cos, ones], axis=-1)
    sa_h = jnp.concatenate([zeros8, sin, zeros], axis=-1)
    sb_h = jnp.concatenate([-sin, zeros8, zeros], axis=-1)
    rep = LANES // HEAD_DIM
    tile = lambda t: jnp.tile(t, (1, 1, rep)).reshape(B * S, LANES)
    return tile(cos_h), tile(sa_h), tile(sb_h)


def _overlap_matrix(n_chunk, n_sel):
    ci = np.arange(n_chunk)[:, None] * CMP_STRIDE
    sj = np.arange(LANES)[None, :] * SEL_BLOCK
    ov = (ci < sj + SEL_BLOCK) & (ci + CMP_BLOCK > sj) & (np.arange(LANES)[None, :] < n_sel)
    ov = ov & (np.arange(n_chunk)[:, None] < n_chunk - 1)
    return jnp.asarray(ov, BF16)


def kernel(x, mem, positions, norm_mix_g, w_in, cmp_pe, cmp_w1, cmp_w2, conv_w, conv_b, conv_ln_g, conv_ln_b, conv_pw_w, conv_pw_b, pool_w, pool_scale, w_out, norm_xa_g, norm_mem_g, xa_wq, xa_wkv, xa_wo, norm_ffn_g, peer_wq, peer_subkeys, peer_u, peer_v, final_g):
    B, S, D = x.shape
    T = B * S
    depth = w_in.shape[0]
    M = mem.shape[1]
    dc = conv_w.shape[-1]
    n_chunk = S // CMP_STRIDE
    n_sel = S // SEL_BLOCK
    assert S % max(TM_PROJ, TS_MIX, KC_SEL, TM_PEER) == 0 and SEL_TOPN <= n_sel <= LANES and D % LANES == 0
    assert S >= WINDOW + TQ_NSA and peer_u.shape[1] % sum(EXPERT_SUBS) == 0

    cos_t, sa_t, sb_t = _rope_tables(positions)
    overlap = _overlap_matrix(n_chunk, n_sel)
    row = lambda v: v.reshape(1, -1)
    x2 = x.reshape(T, D)
    mem2 = mem.reshape(B * M, D)
    n_gate = 3 * NSA_HEADS

    w_pad = jnp.concatenate([w_in[:, :, :D_QKV + n_gate], jnp.zeros((depth, D, LANES - n_gate), F32),
                             w_in[:, :, D_QKV + n_gate:]], axis=2).astype(BF16)
    pg = dc // len(POOL_WINDOWS)
    plw = jnp.zeros((depth, dc, dc), F32)
    for gi in range(len(POOL_WINDOWS)):
        plw = plw.at[:, gi * pg:(gi + 1) * pg, gi * pg:(gi + 1) * pg].set(pool_w[:, gi])
    plw, conv_pw, wo = plw.astype(BF16), conv_pw_w.astype(BF16), w_out.astype(BF16)
    wkv, wq, wxo = xa_wkv.astype(BF16), xa_wq.astype(BF16), xa_wo.astype(BF16)
    pwq_t = jnp.swapaxes(peer_wq, 1, 2).astype(BF16)
    psk, pu = peer_subkeys.astype(BF16), peer_u.astype(BF16)
    pv_t = jnp.swapaxes(peer_v, 1, 2).astype(BF16)

    for l in range(depth):
        q, kvc, kv, misc = _inproj(x2, row(norm_mix_g[l]), w_pad[l], cos_t, sa_t, sb_t, tm=TM_PROJ)
        xc = kvc.reshape(B, n_chunk, CMP_STRIDE * 2 * LANES)
        cmp = _compress(xc, *_compress_weights(cmp_pe[l], cmp_w1[l], cmp_w2[l]))
        o_nsa = _nsa(q, cmp[0], cmp[1], kv.reshape(B, S, -1), misc, overlap, B, S, tq=TQ_NSA)
        x2 = _mixout(misc, o_nsa, x2, conv_w[l], row(conv_b[l]), row(conv_ln_g[l]), row(conv_ln_b[l]),
                     conv_pw[l], row(conv_pw_b[l]), plw[l], row(pool_scale[l]), wo[l], B, S, ts=TS_MIX)
        memkv = _norm_matmul(mem2, row(norm_mem_g[l]), wkv[l], tm=TM_MEM, out_dtype=BF16)
        x2 = _xattn(x2, row(norm_xa_g[l]), wq[l], memkv.reshape(B, M, 2 * D), wxo[l], B, S, tm=TM_PROJ)
        hn, c1, g1, r2, g2 = _peer_route(x2, row(norm_ffn_g[l]), pwq_t[l], psk[l], tm=TM_PEER)
        x2 = _peer_expert(hn, c1, g1, r2, g2, pu, pv_t, l, x2,
                          row(final_g) if l == depth - 1 else None, tm=TM_PEER, subs=EXPERT_SUBS)
    return x2.reshape(B, S, D)
```

```python
import functools
import math

import jax
import jax.numpy as jnp
import numpy as np
from jax import lax
from jax.experimental import pallas as pl
from jax.experimental.pallas import tpu as pltpu

F32 = jnp.float32
BF16 = jnp.bfloat16

NSA_HEADS = 8
NSA_KV_HEADS = 2
NSA_GROUP = NSA_HEADS // NSA_KV_HEADS
HEAD_DIM = 64
D_NSA = NSA_HEADS * HEAD_DIM
D_KV = NSA_KV_HEADS * HEAD_DIM
ROT_DIM = HEAD_DIM // 4
ROT_HALF = ROT_DIM // 2
ROPE_THETA = 500000.0
CMP_BLOCK = 32
CMP_STRIDE = 16
SEL_BLOCK = 64
SEL_TOPN = 16
WINDOW = 512
CONV_WIDTH = 31
POOL_WINDOWS = (2, 4, 8, 16)
XA_HEADS = 4
PEER_HEADS = 8
PEER_KEYS = 128
PEER_TOPK = 16
EPS = 1e-6
NEG = -1e30
FORCE = 1e4
LOWEST = -3.0e38

LANES = 128
SUBLANES = 8
VMEM_LIMIT = 56 * 1024 * 1024

NT_DIMS = (((1,), (1,)), ((), ()))

TM_PROJ = 512
TM_MEM = 256
TQ_NSA = 256
KC_SEL = 512
TS_MIX = 512
TM_PEER = 512
EXPERT_SUBS = (256, 768, 768, 256)


def _params(*sem):
    return pltpu.CompilerParams(dimension_semantics=sem, vmem_limit_bytes=VMEM_LIMIT)


def _full(shape):
    nd = len(shape)
    return pl.BlockSpec(shape, lambda *_: (0,) * nd)


def _rms(x, g):
    return x * lax.rsqrt(jnp.mean(x * x, axis=-1, keepdims=True) + EPS) * g


def _gelu(x):
    c = math.sqrt(2.0 / math.pi)
    return 0.5 * x * (1.0 + jnp.tanh(c * (x + 0.044715 * (x * x * x))))


def _dot(a, b):
    return jnp.dot(a, b, preferred_element_type=F32)


def _dot_nt(a, b):
    return lax.dot_general(a, b, NT_DIMS, preferred_element_type=F32)


N_ROPE_Q = D_NSA // LANES
N_KV_CHUNKS = 6
D_QKV = D_NSA + N_KV_CHUNKS * D_KV
N_KV_OUT = 2 + 2 * NSA_KV_HEADS


def _inproj_kernel(x_ref, g_ref, w_ref, cos_ref, sa_ref, sb_ref, q_ref, kvc_ref, kv_ref, misc_ref):
    y = _rms(x_ref[...], g_ref[...])
    z = _dot(y.astype(BF16), w_ref[...])
    cos, sa, sb = cos_ref[...], sa_ref[...], sb_ref[...]

    def rope(c):
        return c * cos + pltpu.roll(c, ROT_HALF, 1) * sa + pltpu.roll(c, LANES - ROT_HALF, 1) * sb

    scale = HEAD_DIM ** -0.5
    for j in range(N_ROPE_Q):
        q_ref[:, j * LANES:(j + 1) * LANES] = (rope(z[:, j * LANES:(j + 1) * LANES]) * scale).astype(BF16)
    lane = lax.broadcasted_iota(jnp.int32, (z.shape[0], LANES), 1)
    out = 0
    for j in range(N_KV_CHUNKS):
        c = z[:, D_NSA + j * LANES:D_NSA + (j + 1) * LANES]
        if j % 2 == 0:
            c = rope(c)
        if j < 2:
            kvc_ref[:, j * LANES:(j + 1) * LANES] = c.astype(BF16)
        elif j % 2 == 0:
            kv_ref[:, out * LANES:(out + 1) * LANES] = c.astype(BF16)
            out += 1
    for j in (3, 5):
        c = z[:, D_NSA + j * LANES:D_NSA + (j + 1) * LANES]
        for k in range(NSA_KV_HEADS):
            own = (lane >= HEAD_DIM) if k == 1 else (lane < HEAD_DIM)
            kv_ref[:, out * LANES:(out + 1) * LANES] = jnp.where(own, c, 1.0).astype(BF16)
            out += 1
    misc_ref[...] = z[:, D_QKV:]


def _inproj(x2, g, w_pad, cos_t, sa_t, sb_t, tm):
    T, D = x2.shape
    n = w_pad.shape[1]
    n_misc = n - D_QKV
    row = lambda w: pl.BlockSpec((tm, w), lambda i: (i, 0))
    return pl.pallas_call(
        _inproj_kernel,
        grid=(T // tm,),
        in_specs=[row(D), _full((1, D)), _full((D, n)), row(LANES), row(LANES), row(LANES)],
        out_specs=[row(D_NSA), row(2 * LANES), row(N_KV_OUT * LANES), row(n_misc)],
        out_shape=[jax.ShapeDtypeStruct((T, D_NSA), BF16),
                   jax.ShapeDtypeStruct((T, 2 * LANES), BF16),
                   jax.ShapeDtypeStruct((T, N_KV_OUT * LANES), BF16),
                   jax.ShapeDtypeStruct((T, n_misc), F32)],
        compiler_params=_params("parallel"),
    )(x2, g, w_pad, cos_t, sa_t, sb_t)


def _compress_kernel(x_ref, pe_ref, w1_ref, w2_ref, o_ref):
    x = x_ref[0].astype(F32)
    n = x.shape[0]
    out = jnp.zeros((n, LANES), F32)
    for hd in range(NSA_KV_HEADS):
        a = _dot((x + pe_ref[0, hd, 0:1]).astype(BF16), w1_ref[0, hd, 0])
        b = _dot((x + pe_ref[0, hd, 1:2]).astype(BF16), w1_ref[0, hd, 1])
        hid = _gelu(a + pltpu.roll(b, n - 1, 0))
        out = out + _dot(hid.astype(BF16), w2_ref[0, hd])
    o_ref[0, 0] = out.astype(BF16)


def _compress(xc, pe, w1, w2):
    nb, nch, width = xc.shape
    hid = w1.shape[-1]
    return pl.pallas_call(
        _compress_kernel,
        grid=(2, nb),
        in_specs=[pl.BlockSpec((1, nch, width), lambda w, i: (i, 0, 0)),
                  pl.BlockSpec((1, NSA_KV_HEADS, 2, width), lambda w, i: (w, 0, 0, 0)),
                  pl.BlockSpec((1, NSA_KV_HEADS, 2, width, hid), lambda w, i: (w, 0, 0, 0, 0)),
                  pl.BlockSpec((1, NSA_KV_HEADS, hid, LANES), lambda w, i: (w, 0, 0, 0))],
        out_specs=pl.BlockSpec((1, 1, nch, LANES), lambda w, i: (w, i, 0, 0)),
        out_shape=jax.ShapeDtypeStruct((2, nb, nch, LANES), BF16),
        compiler_params=_params("parallel", "parallel"),
    )(xc, pe, w1, w2)


def _compress_weights(pe, w1, w2):
    hid = w1.shape[-1]
    eye_t = jnp.eye(2, dtype=F32)[:, None, None, None, :, None, None]
    eye_h = jnp.eye(NSA_KV_HEADS, dtype=F32)[None, :, None, None, None, :, None]
    per = (2, 1, 2, CMP_STRIDE, 1, 1, HEAD_DIM)
    width = CMP_STRIDE * 2 * NSA_KV_HEADS * HEAD_DIM
    pe_big = (pe.reshape(per) * eye_t * eye_h).reshape(2, NSA_KV_HEADS, 2, width)
    w1_big = (w1.reshape(per + (hid,)) * eye_t[..., None] * eye_h[..., None]).reshape(2, NSA_KV_HEADS, 2, width, hid)
    eye_o = jnp.eye(NSA_KV_HEADS, dtype=F32)[None, :, None, :, None]
    w2_big = (w2[:, None, :, None, :] * eye_o).reshape(2, NSA_KV_HEADS, hid, NSA_KV_HEADS * HEAD_DIM)
    return pe_big, w1_big.astype(BF16), w2_big.astype(BF16)


def _nsa_kernel(q_ref, kcmp_ref, vcmp_ref, ks_ref, kw_ref, vs0_ref, vs1_ref, vw0_ref, vw1_ref, gate_ref, ov_ref,
                o_ref, s_sc, mrun_sc, acc_sc, *, tq, kc_sel, kc_win):
    t0 = pl.program_id(1) * tq
    rows = NSA_GROUP * tq
    lane = lax.broadcasted_iota(jnp.int32, (tq, LANES), 1)
    lane_r = lax.broadcasted_iota(jnp.int32, (rows, LANES), 1)
    tpos = t0 + lax.broadcasted_iota(jnp.int32, (tq, 1), 0)
    gates = jax.nn.sigmoid(gate_ref[...])
    n_cmp = kcmp_ref.shape[1]
    n_sel = ks_ref.shape[1] // SEL_BLOCK
    vs_refs, vw_refs = (vs0_ref, vs1_ref), (vw0_ref, vw1_ref)

    def normalise(acc, own):
        return acc / jnp.where(own, pltpu.roll(acc, HEAD_DIM, 1), 1.0)

    def own_lanes(lanes, k):
        return (lanes >= HEAD_DIM) if k == 1 else (lanes < HEAD_DIM)

    def queries(k):
        parts = []
        for g in range(NSA_GROUP):
            hh = k * NSA_GROUP + g
            c = q_ref[:, (hh // 2) * LANES:(hh // 2 + 1) * LANES].astype(F32)
            if hh % 2 != k:
                c = pltpu.roll(c, HEAD_DIM, 1)
            parts.append(jnp.where(own_lanes(lane, k), c, 0.0))
        return jnp.concatenate(parts, axis=0).astype(BF16)

    def compressed(qs):
        s = _dot_nt(qs, kcmp_ref[0]).reshape(NSA_GROUP, tq, n_cmp)
        cmp_end = lax.broadcasted_iota(jnp.int32, (tq, n_cmp), 1) * CMP_STRIDE + (CMP_BLOCK - 1)
        valid = cmp_end <= tpos
        s = s + jnp.where(valid, 0.0, NEG)[None]
        e = jnp.exp(s - jnp.max(s, axis=-1, keepdims=True)) * valid.astype(F32)[None]
        l = jnp.sum(e, axis=-1, keepdims=True)
        p3 = e * (1.0 / jnp.where(l > 0.0, l, 1.0))
        o_cmp = _dot(p3.reshape(rows, n_cmp).astype(BF16), vcmp_ref[0])

        psum = p3[0] + p3[1] + p3[2] + p3[3]
        ov = ov_ref[...]
        p_hi = psum.astype(BF16)
        r1 = psum - p_hi.astype(F32)
        p_mid = r1.astype(BF16)
        p_lo = (r1 - p_mid.astype(F32)).astype(BF16)
        imp = _dot(p_hi, ov) + _dot(p_mid, ov) + _dot(p_lo, ov)
        imp = imp.T[:n_sel]
        blk = lax.broadcasted_iota(jnp.int32, (n_sel, tq), 0)
        tpos_t = t0 + lax.broadcasted_iota(jnp.int32, (1, tq), 1)
        cur = tpos_t // SEL_BLOCK
        forced = (blk == 0) | (blk == cur) | (blk == cur - 1)
        imp = jnp.where(blk * SEL_BLOCK <= tpos_t, jnp.where(forced, FORCE, imp), NEG)
        blk_f = blk.astype(F32)
        for _ in range(SEL_TOPN):
            m = jnp.max(imp, axis=0, keepdims=True)
            first = jnp.min(jnp.where(imp == m, blk_f, float(n_sel)), axis=0, keepdims=True)
            imp = jnp.where(blk_f == first, LOWEST, imp)
        sel_t = jnp.concatenate([(imp == LOWEST).astype(F32), jnp.zeros((LANES - n_sel, tq), F32)], axis=0)
        return o_cmp, sel_t.T.astype(BF16)

    def window(k, qs):
        w0 = pl.multiple_of(jnp.maximum(t0 + tq - kc_win, 0), tq)
        sw = _dot_nt(qs, kw_ref[0, pl.ds(w0, kc_win), :]).reshape(NSA_GROUP, tq, kc_win)
        wpos = w0 + lax.broadcasted_iota(jnp.int32, (tq, kc_win), 1)
        in_win = (wpos <= tpos) & (wpos > tpos - WINDOW)
        sw = (sw + jnp.where(in_win, 0.0, NEG)[None]).reshape(rows, kc_win)
        pw = jnp.exp(sw - jnp.max(sw, axis=1, keepdims=True))
        return normalise(_dot(pw.astype(BF16), vw_refs[k][0, pl.ds(w0, kc_win), :]), own_lanes(lane_r, k))

    heads = range(NSA_KV_HEADS)
    qs = [queries(k) for k in heads]
    o_cmp, sel = zip(*[compressed(qs[k]) for k in heads])
    o_win = [window(k, qs[k]) for k in heads]

    n_chunks = (t0 + tq + kc_sel - 1) // kc_sel
    mrun_sc[...] = jnp.full(mrun_sc.shape, NEG, F32)

    def scores(c, carry):
        k0 = pl.multiple_of(c * kc_sel, kc_sel)
        jrow = lax.broadcasted_iota(jnp.int32, (LANES, kc_sel), 0)
        kcol = lax.broadcasted_iota(jnp.int32, (LANES, kc_sel), 1)
        expand = (jrow == k0 // SEL_BLOCK + kcol // SEL_BLOCK).astype(BF16)
        causal = k0 + lax.broadcasted_iota(jnp.int32, (tq, kc_sel), 1) <= tpos
        kb = ks_ref[0, pl.ds(k0, kc_sel), :]
        for k in heads:
            seen = (_dot(sel[k], expand) > 0.5) & causal
            s = _dot_nt(qs[k], kb).reshape(NSA_GROUP, tq, kc_sel)
            s = (s + jnp.where(seen, 0.0, NEG)[None]).reshape(rows, kc_sel)
            s_sc[k, c] = s
            m = mrun_sc[k]
            for j in range(kc_sel // LANES):
                m = jnp.maximum(m, s[:, j * LANES:(j + 1) * LANES])
            mrun_sc[k] = m
        return carry

    lax.fori_loop(0, n_chunks, scores, 0)
    m_sel = [jnp.broadcast_to(jnp.max(mrun_sc[k], axis=1, keepdims=True), (rows, LANES)) for k in heads]
    acc_sc[...] = jnp.zeros(acc_sc.shape, F32)

    def weigh(c, carry):
        k0 = pl.multiple_of(c * kc_sel, kc_sel)
        for k in heads:
            s = s_sc[k, c]
            p = jnp.concatenate([jnp.exp(s[:, j * LANES:(j + 1) * LANES] - m_sel[k])
                                 for j in range(kc_sel // LANES)], axis=1)
            acc_sc[k] += _dot(p.astype(BF16), vs_refs[k][0, pl.ds(k0, kc_sel), :])
        return carry

    lax.fori_loop(0, n_chunks, weigh, 0)

    for k in heads:
        o_sel = normalise(acc_sc[k], own_lanes(lane_r, k))
        outs = []
        for g in range(NSA_GROUP):
            r = slice(g * tq, (g + 1) * tq)
            gi = (k * NSA_GROUP + g) * 3
            og = (gates[:, gi:gi + 1] * o_cmp[k][r] + gates[:, gi + 1:gi + 2] * o_sel[r]
                  + gates[:, gi + 2:gi + 3] * o_win[k][r])
            if g % 2 != k:
                og = pltpu.roll(og, HEAD_DIM, 1)
            outs.append(og)
        for j in range(NSA_GROUP // 2):
            chunk = jnp.where(lane < HEAD_DIM, outs[2 * j], outs[2 * j + 1])
            cj = k * (NSA_GROUP // 2) + j
            o_ref[:, cj * LANES:(cj + 1) * LANES] = chunk.astype(BF16)


def _nsa(q, kcmp, vcmp, kv, gates_misc, overlap, B, S, tq):
    kc_sel, kc_win = KC_SEL, WINDOW + tq
    nq = S // tq
    n_cmp = kcmp.shape[1]
    rows = NSA_GROUP * tq
    kvspec = lambda j: pl.BlockSpec((1, S, LANES), lambda b, i: (b, 0, j), pipeline_mode=pl.Buffered(1))
    return pl.pallas_call(
        functools.partial(_nsa_kernel, tq=tq, kc_sel=kc_sel, kc_win=kc_win),
        grid=(B, nq),
        in_specs=[pl.BlockSpec((tq, D_NSA), lambda b, i: (b * nq + i, 0)),
                  pl.BlockSpec((1, n_cmp, LANES), lambda b, i: (b, 0, 0)),
                  pl.BlockSpec((1, n_cmp, LANES), lambda b, i: (b, 0, 0)),
                  kvspec(0), kvspec(1), kvspec(2), kvspec(3), kvspec(4), kvspec(5),
                  pl.BlockSpec((tq, LANES), lambda b, i: (b * nq + i, 0)),
                  _full(overlap.shape)],
        out_specs=pl.BlockSpec((tq, D_NSA), lambda b, i: (b * nq + i, 0)),
        out_shape=jax.ShapeDtypeStruct((B * S, D_NSA), BF16),
        scratch_shapes=[pltpu.VMEM((NSA_KV_HEADS, S // kc_sel, rows, kc_sel), F32),
                        pltpu.VMEM((NSA_KV_HEADS, rows, LANES), F32), pltpu.VMEM((NSA_KV_HEADS, rows, LANES), F32)],
        compiler_params=_params("parallel", "parallel"),
    )(q, kcmp, vcmp, kv, kv, kv, kv, kv, kv, gates_misc, overlap)


HALO = 32


def _mixout_kernel(mc_ref, mp_ref, on_ref, x_ref, cw_ref, cb_ref, lg_ref, lb_ref, pw_ref, pb_ref, plw_ref,
                   pls_ref, wo_ref, o_ref, hbuf, pbuf, hsh, *, ts, dc):
    i = pl.program_id(1)
    first = i == 0
    off_a, off_b, off_p = LANES, LANES + dc, LANES + 2 * dc

    def glu(ref, r):
        return ref[r, off_a:off_a + dc] * jax.nn.sigmoid(ref[r, off_b:off_b + dc])

    tail = slice(ts - HALO, ts)
    hbuf[0:HALO, :] = jnp.where(first, 0.0, glu(mp_ref, tail))
    hbuf[HALO:, :] = glu(mc_ref, slice(None))
    pbuf[0:HALO, :] = jnp.where(first, 0.0, mp_ref[tail, off_p:off_p + dc])
    pcur = mc_ref[:, off_p:off_p + dc]
    pbuf[HALO:, :] = pcur

    shifted = ts + HALO - SUBLANES
    for r in range(1, SUBLANES):
        hsh[r, 0:shifted, :] = hbuf[pl.ds(r, shifted), :]
    acc = jnp.zeros((ts, dc), F32) + cb_ref[...]
    for w in range(CONV_WIDTH):
        q, r = divmod(HALO - (CONV_WIDTH - 1) + w, SUBLANES)
        tap = hbuf[pl.ds(q * SUBLANES, ts), :] if r == 0 else hsh[r, pl.ds(q * SUBLANES, ts), :]
        acc = acc + tap * cw_ref[w:w + 1, :]
    mu = jnp.mean(acc, axis=-1, keepdims=True)
    xc = acc - mu
    y = xc * lax.rsqrt(jnp.mean(xc * xc, axis=-1, keepdims=True) + EPS) * lg_ref[...] + lb_ref[...]
    y = y * jax.nn.sigmoid(y)
    o_conv = _dot(y.astype(BF16), pw_ref[...]) + pb_ref[...]

    tglob = i * ts + lax.broadcasted_iota(jnp.int32, (ts, dc), 0)
    lane = lax.broadcasted_iota(jnp.int32, (ts, dc), 1)
    pg = dc // len(POOL_WINDOWS)
    run = pcur
    d = 1
    mean = jnp.zeros((ts, dc), F32)
    for gi, w in enumerate(POOL_WINDOWS):
        while d < w:
            run = run + pbuf[pl.ds(HALO - d, ts), :]
            d += 1
        cnt = jnp.minimum(tglob + 1, w).astype(F32)
        mean = jnp.where(lane // pg == gi, run / cnt, mean)
    o_pool = _dot((mean - pcur).astype(BF16), plw_ref[...]) * pls_ref[...]

    dn = on_ref.shape[1]
    o_ref[...] = (x_ref[...] + _dot(on_ref[...], wo_ref[0:dn, :])
                  + _dot(o_conv.astype(BF16), wo_ref[dn:dn + dc, :])
                  + _dot(o_pool.astype(BF16), wo_ref[dn + dc:, :]))


def _mixout(misc, o_nsa, x2, cw, cb, lg, lb, pw, pb, plw, pls, wo, B, S, ts):
    T, D = x2.shape
    dc = cw.shape[1]
    ns = S // ts
    nm = misc.shape[1]
    cur = lambda w: pl.BlockSpec((ts, w), lambda b, i: (b * ns + i, 0))
    prev = pl.BlockSpec((ts, nm), lambda b, i: (b * ns + jnp.maximum(i - 1, 0), 0))
    return pl.pallas_call(
        functools.partial(_mixout_kernel, ts=ts, dc=dc),
        grid=(B, ns),
        in_specs=[cur(nm), prev, cur(o_nsa.shape[1]), cur(D), _full(cw.shape), _full(cb.shape), _full(lg.shape),
                  _full(lb.shape), _full(pw.shape), _full(pb.shape), _full(plw.shape), _full(pls.shape),
                  _full(wo.shape)],
        out_specs=cur(D),
        out_shape=jax.ShapeDtypeStruct((T, D), F32),
        scratch_shapes=[pltpu.VMEM((ts + HALO, dc), F32), pltpu.VMEM((ts + HALO, dc), F32),
                        pltpu.VMEM((SUBLANES, ts + HALO, dc), F32)],
        compiler_params=_params("parallel", "parallel"),
    )(misc, misc, o_nsa, x2, cw, cb, lg, lb, pw, pb, plw, pls, wo)


def _norm_matmul_kernel(x_ref, g_ref, w_ref, o_ref):
    o_ref[...] = _dot(_rms(x_ref[...], g_ref[...]).astype(BF16), w_ref[...]).astype(o_ref.dtype)


def _norm_matmul(x2, g, w, tm, out_dtype):
    T, D = x2.shape
    n = w.shape[1]
    return pl.pallas_call(
        _norm_matmul_kernel,
        grid=(T // tm,),
        in_specs=[pl.BlockSpec((tm, D), lambda i: (i, 0)), _full((1, D)), _full((D, n))],
        out_specs=pl.BlockSpec((tm, n), lambda i: (i, 0)),
        out_shape=jax.ShapeDtypeStruct((T, n), out_dtype),
        compiler_params=_params("parallel"),
    )(x2, g, w)


def _xattn_kernel(x_ref, g_ref, wq_ref, k_ref, v_ref, wo_ref, o_ref):
    x = x_ref[...]
    D = x.shape[1]
    dh = D // XA_HEADS
    q = _dot(_rms(x, g_ref[...]).astype(BF16), wq_ref[...]) * (dh ** -0.5)
    outs = []
    for h in range(XA_HEADS):
        c = slice(h * dh, (h + 1) * dh)
        s = _dot_nt(q[:, c].astype(BF16), k_ref[0, :, c])
        e = jnp.exp(s - jnp.max(s, axis=-1, keepdims=True))
        p = e * (1.0 / jnp.sum(e, axis=-1, keepdims=True))
        outs.append(_dot(p.astype(BF16), v_ref[0, :, c]))
    o = jnp.concatenate(outs, axis=1)
    o_ref[...] = x + _dot(o.astype(BF16), wo_ref[...])


def _xattn(x2, g, wq, memkv, wo, B, S, tm):
    T, D = x2.shape
    ns = S // tm
    M = memkv.shape[1]
    return pl.pallas_call(
        _xattn_kernel,
        grid=(B, ns),
        in_specs=[pl.BlockSpec((tm, D), lambda b, i: (b * ns + i, 0)), _full((1, D)), _full((D, D)),
                  pl.BlockSpec((1, M, D), lambda b, i: (b, 0, 0)),
                  pl.BlockSpec((1, M, D), lambda b, i: (b, 0, 1)),
                  _full((D, D))],
        out_specs=pl.BlockSpec((tm, D), lambda b, i: (b * ns + i, 0)),
        out_shape=jax.ShapeDtypeStruct((T, D), F32),
        compiler_params=_params("parallel", "parallel"),
    )(x2, g, wq, memkv, memkv, wo)


PAIR_LIST = tuple((a, b) for a in range(PEER_TOPK) for b in range(PEER_TOPK) if (a + 1) * (b + 1) <= PEER_TOPK)
N_PAIR_ROWS = -(-len(PAIR_LIST) // 8) * 8
PAIR_COUNT = tuple(PEER_TOPK // (a + 1) for a in range(PEER_TOPK))
PAIR_START = tuple(sum(PAIR_COUNT[:a]) for a in range(PEER_TOPK))


def _top_rows_exact(v, n):
    R = v.shape[0]
    ridx = lax.broadcasted_iota(jnp.int32, v.shape, 0).astype(F32)
    rank = jnp.full(v.shape, float(n), F32)
    vals = []
    for r in range(n):
        m = jnp.max(v, axis=0, keepdims=True)
        first = jnp.min(jnp.where(v == m, ridx, float(R)), axis=0, keepdims=True)
        taken = ridx == first
        v = jnp.where(taken, LOWEST, v)
        rank = jnp.where(taken, float(r), rank)
        vals.append(m)
    return vals, rank


def _top_rows_distinct(v, n):
    rank = jnp.full(v.shape, float(n), F32)
    vals = []
    for r in range(n):
        m = jnp.max(v, axis=0, keepdims=True)
        taken = v == m
        v = jnp.where(taken, LOWEST, v)
        rank = jnp.where(taken, float(r), rank)
        vals.append(m)
    return vals, rank


def _peer_route_kernel(x_ref, g_ref, wqt_ref, sk_ref, hn_ref, c1_ref, g1_ref, r2_ref, g2_ref, qt_sc, s_sc, top_sc,
                       rank_sc, cand_sc, cw_sc):
    hn = _rms(x_ref[...], g_ref[...]).astype(BF16)
    hn_ref[...] = hn
    qt_sc[...] = _dot_nt(wqt_ref[...], hn).astype(BF16)
    nk, n = PEER_KEYS, PEER_TOPK
    tm = x_ref.shape[0]

    def put(li, vals, rank):
        for r in range(n):
            top_sc[pl.ds(li * n + r, 1), :] = vals[r]
        rank_sc[pl.ds(pl.multiple_of(li * nk, nk), nk), :] = rank

    def one_head(h, carry):
        ranked = jnp.zeros((1, tm), F32)
        for half in range(2):
            li = 2 * h + half
            r0 = pl.multiple_of(li * nk, nk)
            s = _dot(sk_ref[half], qt_sc[pl.ds(r0, nk), :])
            s_sc[pl.ds(r0, nk), :] = s
            vals, rank = _top_rows_distinct(s, n)
            put(li, vals, rank)
            ranked = jnp.maximum(ranked, jnp.sum((rank < float(n)).astype(F32), axis=0, keepdims=True))

        @pl.when(jnp.max(ranked) > float(n))
        def _():
            for half in range(2):
                li = 2 * h + half
                put(li, *_top_rows_exact(s_sc[pl.ds(pl.multiple_of(li * nk, nk), nk), :], n))

        return carry

    lax.fori_loop(0, PEER_HEADS, one_head, 0)

    cand_sc[...] = jnp.full(cand_sc.shape, LOWEST, F32)
    cw_sc[...] = jnp.zeros(cw_sc.shape, F32)
    for h in range(PEER_HEADS):
        l1, l2 = 2 * h, 2 * h + 1
        top1 = [top_sc[l1 * n + a:l1 * n + a + 1, :] for a in range(n)]
        top2 = [top_sc[l2 * n + a:l2 * n + a + 1, :] for a in range(n)]
        e1 = [jnp.exp(t - top1[0]) for t in top1]
        e2 = [jnp.exp(t - top2[0]) for t in top2]
        for r, (a, b) in enumerate(PAIR_LIST):
            cand_sc[h, r:r + 1, :] = top1[a] + top2[b]
            cw_sc[h, r:r + 1, :] = e1[a] * e2[b]
        _, crank = _top_rows_exact(cand_sc[h], n)
        chosen = (crank < float(n)).astype(F32)
        z = jnp.sum(chosen * cw_sc[h], axis=0, keepdims=True)
        rank1 = rank_sc[l1 * nk:(l1 + 1) * nk, :].astype(BF16)
        count1 = jnp.zeros(rank1.shape, BF16)
        for a in range(n):
            n_a = jnp.sum(chosen[PAIR_START[a]:PAIR_START[a] + PAIR_COUNT[a]], axis=0, keepdims=True)
            count1 = jnp.where(rank1 == float(a), n_a.astype(BF16), count1)
        rows = slice(h * nk, (h + 1) * nk)
        c1_ref[rows, :] = count1.astype(F32)
        g1_ref[rows, :] = jnp.exp(s_sc[l1 * nk:(l1 + 1) * nk, :] - top1[0]) * (1.0 / z)
        r2_ref[rows, :] = rank_sc[l2 * nk:(l2 + 1) * nk, :].astype(BF16)
        g2_ref[rows, :] = jnp.exp(s_sc[l2 * nk:(l2 + 1) * nk, :] - top2[0]).astype(BF16)


def _peer_route(x2, g, wqt, sk, tm):
    T, D = x2.shape
    nr = PEER_HEADS * PEER_KEYS
    col = pl.BlockSpec((nr, tm), lambda i: (0, i))
    return pl.pallas_call(
        _peer_route_kernel,
        grid=(T // tm,),
        in_specs=[pl.BlockSpec((tm, D), lambda i: (i, 0)), _full((1, D)), _full(wqt.shape), _full(sk.shape)],
        out_specs=[pl.BlockSpec((tm, D), lambda i: (i, 0)), col, col, col, col],
        out_shape=[jax.ShapeDtypeStruct((T, D), BF16), jax.ShapeDtypeStruct((nr, T), F32),
                   jax.ShapeDtypeStruct((nr, T), F32), jax.ShapeDtypeStruct((nr, T), BF16),
                   jax.ShapeDtypeStruct((nr, T), BF16)],
        scratch_shapes=[pltpu.VMEM((2 * nr, tm), BF16), pltpu.VMEM((2 * nr, tm), F32),
                        pltpu.VMEM((2 * PEER_HEADS * PEER_TOPK, tm), F32),
                        pltpu.VMEM((2 * nr, tm), F32), pltpu.VMEM((PEER_HEADS, N_PAIR_ROWS, tm), F32),
                        pltpu.VMEM((PEER_HEADS, N_PAIR_ROWS, tm), F32)],
        compiler_params=_params("parallel"),
    )(x2, g, wqt, sk)


def _peer_expert_kernel(*refs, subs, final_norm):
    hn_ref, c1_ref, g1_ref, r2_ref, g2_ref, u_ref, vt_ref, x_ref = refs[:8]
    fg_ref = refs[8] if final_norm else None
    o_ref, acc_sc, wa_sc, act_sc = refs[-4:]
    c = pl.program_id(1)
    nk = PEER_KEYS
    te = sum(subs)
    offs = [sum(subs[:j]) for j in range(len(subs))]
    tm = hn_ref.shape[0]

    @pl.when(c == 0)
    def _():
        acc_sc[...] = jnp.zeros(acc_sc.shape, F32)

    def pre_act(j):
        o, rows = offs[j], subs[j]
        act_sc[j % 2, 0:rows, :] = _dot_nt(u_ref[o:o + rows, :], hn_ref[...]).astype(BF16)

    pre_act(0)
    for j, (o, rows) in enumerate(zip(offs, subs)):
        if j + 1 < len(subs):
            pre_act(j + 1)
        for il in range(rows // nk):
            i = c * (te // nk) + o // nk + il
            w = jnp.zeros((nk, tm), BF16)
            for h in range(PEER_HEADS):
                count = c1_ref[pl.ds(h * nk + i, 1), :].astype(BF16)
                g1row = g1_ref[pl.ds(h * nk + i, 1), :].astype(BF16)
                hr = slice(h * nk, (h + 1) * nk)
                w = w + jnp.where(r2_ref[hr, :] < count, g2_ref[hr, :], 0.0) * g1row
            er = slice(il * nk, (il + 1) * nk)
            wa_sc[o + il * nk:o + (il + 1) * nk, :] = w * _gelu(act_sc[j % 2, er, :])
        acc_sc[...] += _dot(vt_ref[:, o:o + rows], wa_sc[o:o + rows, :])

    @pl.when(c == pl.num_programs(1) - 1)
    def _():
        out = x_ref[...] + acc_sc[...].T
        o_ref[...] = _rms(out, fg_ref[...]) if final_norm else out


def _peer_expert(hn, c1, g1, r2, g2, u, vt, layer, x2, final_g, tm, subs):
    T, D = x2.shape
    ne = u.shape[1]
    te = sum(subs)
    tok = pl.BlockSpec((c1.shape[0], tm), lambda i, c: (0, i))
    row = pl.BlockSpec((tm, D), lambda i, c: (i, 0))
    return pl.pallas_call(
        functools.partial(_peer_expert_kernel, subs=subs, final_norm=final_g is not None),
        grid=(T // tm, ne // te),
        in_specs=[row, tok, tok, tok, tok,
                  pl.BlockSpec((None, te, D), lambda i, c: (layer, c, 0)),
                  pl.BlockSpec((None, D, te), lambda i, c: (layer, 0, c)),
                  row] + ([] if final_g is None else [_full((1, D))]),
        out_specs=row,
        out_shape=jax.ShapeDtypeStruct((T, D), F32),
        scratch_shapes=[pltpu.VMEM((D, tm), F32), pltpu.VMEM((te, tm), BF16),
                        pltpu.VMEM((2, max(subs), tm), BF16)],
        compiler_params=_params("parallel", "arbitrary"),
    )(hn, c1, g1, r2, g2, u, vt, x2, *([] if final_g is None else [final_g]))


def _rope_tables(positions):
    B, S = positions.shape
    freqs = ROPE_THETA ** (-jnp.arange(ROT_HALF, dtype=F32) * 2.0 / ROT_DIM)
    ang = positions.astype(F32)[:, :, None] * freqs
    cos, sin = jnp.cos(ang), jnp.sin(ang)
    ones = jnp.ones((B, S, HEAD_DIM - ROT_DIM), F32)
    zeros8 = jnp.zeros((B, S, ROT_HALF), F32)
    zeros = jnp.zeros((B, S, HEAD_DIM - ROT_DIM), F32)
    cos_h = jnp.concatenate([cos, cos, ones], axis=-1)
    sa_h = jnp.concatenate([zeros8, sin, zeros], axis=-1)
    sb_h = jnp.concatenate([-sin, zeros8, zeros], axis=-1)
    rep = LANES // HEAD_DIM
    tile = lambda t: jnp.tile(t, (1, 1, rep)).reshape(B * S, LANES)
    return tile(cos_h), tile(sa_h), tile(sb_h)


def _overlap_matrix(n_chunk, n_sel):
    ci = np.arange(n_chunk)[:, None] * CMP_STRIDE
    sj = np.arange(LANES)[None, :] * SEL_BLOCK
    ov = (ci < sj + SEL_BLOCK) & (ci + CMP_BLOCK > sj) & (np.arange(LANES)[None, :] < n_sel)
    ov = ov & (np.arange(n_chunk)[:, None] < n_chunk - 1)
    return jnp.asarray(ov, BF16)


def kernel(x, mem, positions, norm_mix_g, w_in, cmp_pe, cmp_w1, cmp_w2, conv_w, conv_b, conv_ln_g, conv_ln_b, conv_pw_w, conv_pw_b, pool_w, pool_scale, w_out, norm_xa_g, norm_mem_g, xa_wq, xa_wkv, xa_wo, norm_ffn_g, peer_wq, peer_subkeys, peer_u, peer_v, final_g):
    B, S, D = x.shape
    T = B * S
    depth = w_in.shape[0]
    M = mem.shape[1]
    dc = conv_w.shape[-1]
    n_chunk = S // CMP_STRIDE
    n_sel = S // SEL_BLOCK
    assert S % max(TM_PROJ, TS_MIX, KC_SEL, TM_PEER) == 0 and SEL_TOPN <= n_sel <= LANES and D % LANES == 0
    assert S >= WINDOW + TQ_NSA and peer_u.shape[1] % sum(EXPERT_SUBS) == 0

    cos_t, sa_t, sb_t = _rope_tables(positions)
    overlap = _overlap_matrix(n_chunk, n_sel)
    row = lambda v: v.reshape(1, -1)
    x2 = x.reshape(T, D)
    mem2 = mem.reshape(B * M, D)
    n_gate = 3 * NSA_HEADS

    w_pad = jnp.concatenate([w_in[:, :, :D_QKV + n_gate], jnp.zeros((depth, D, LANES - n_gate), F32),
                             w_in[:, :, D_QKV + n_gate:]], axis=2).astype(BF16)
    pg = dc // len(POOL_WINDOWS)
    plw = jnp.zeros((depth, dc, dc), F32)
    for gi in range(len(POOL_WINDOWS)):
        plw = plw.at[:, gi * pg:(gi + 1) * pg, gi * pg:(gi + 1) * pg].set(pool_w[:, gi])
    plw, conv_pw, wo = plw.astype(BF16), conv_pw_w.astype(BF16), w_out.astype(BF16)
    wkv, wq, wxo = xa_wkv.astype(BF16), xa_wq.astype(BF16), xa_wo.astype(BF16)
    pwq_t = jnp.swapaxes(peer_wq, 1, 2).astype(BF16)
    psk, pu = peer_subkeys.astype(BF16), peer_u.astype(BF16)
    pv_t = jnp.swapaxes(peer_v, 1, 2).astype(BF16)

    for l in range(depth):
        q, kvc, kv, misc = _inproj(x2, row(norm_mix_g[l]), w_pad[l], cos_t, sa_t, sb_t, tm=TM_PROJ)
        xc = kvc.reshape(B, n_chunk, CMP_STRIDE * 2 * LANES)
        cmp = _compress(xc, *_compress_weights(cmp_pe[l], cmp_w1[l], cmp_w2[l]))
        o_nsa = _nsa(q, cmp[0], cmp[1], kv.reshape(B, S, -1), misc, overlap, B, S, tq=TQ_NSA)
        x2 = _mixout(misc, o_nsa, x2, conv_w[l], row(conv_b[l]), row(conv_ln_g[l]), row(conv_ln_b[l]),
                     conv_pw[l], row(conv_pw_b[l]), plw[l], row(pool_scale[l]), wo[l], B, S, ts=TS_MIX)
        memkv = _norm_matmul(mem2, row(norm_mem_g[l]), wkv[l], tm=TM_MEM, out_dtype=BF16)
        x2 = _xattn(x2, row(norm_xa_g[l]), wq[l], memkv.reshape(B, M, 2 * D), wxo[l], B, S, tm=TM_PROJ)
        hn, c1, g1, r2, g2 = _peer_route(x2, row(norm_ffn_g[l]), pwq_t[l], psk[l], tm=TM_PEER)
        x2 = _peer_expert(hn, c1, g1, r2, g2, pu, pv_t, l, x2,
                          row(final_g) if l == depth - 1 else None, tm=TM_PEER, subs=EXPERT_SUBS)
    return x2.reshape(B, S, D)
```

```python
import functools
import math

import jax
import jax.numpy as jnp
import numpy as np
from jax import lax
from jax.experimental import pallas as pl
from jax.experimental.pallas import tpu as pltpu

F32 = jnp.float32
BF16 = jnp.bfloat16

NSA_HEADS = 8
NSA_KV_HEADS = 2
NSA_GROUP = NSA_HEADS // NSA_KV_HEADS
HEAD_DIM = 64
D_NSA = NSA_HEADS * HEAD_DIM
D_KV = NSA_KV_HEADS * HEAD_DIM
ROT_DIM = HEAD_DIM // 4
ROT_HALF = ROT_DIM // 2
ROPE_THETA = 500000.0
CMP_BLOCK = 32
CMP_STRIDE = 16
SEL_BLOCK = 64
SEL_TOPN = 16
WINDOW = 512
CONV_WIDTH = 31
POOL_WINDOWS = (2, 4, 8, 16)
XA_HEADS = 4
PEER_HEADS = 8
PEER_KEYS = 128
PEER_TOPK = 16
EPS = 1e-6
NEG = -1e30
FORCE = 1e4
LOWEST = -3.0e38

LANES = 128
SUBLANES = 8
VMEM_LIMIT = 56 * 1024 * 1024

NT_DIMS = (((1,), (1,)), ((), ()))

TM_PROJ = 512
TM_MEM = 256
TQ_NSA = 256
KC_SEL = 512
TS_MIX = 512
TM_PEER = 512
EXPERT_SUBS = (256, 768, 768, 256)
EXPERT_LOOKAHEAD = 3


def _params(*sem):
    return pltpu.CompilerParams(dimension_semantics=sem, vmem_limit_bytes=VMEM_LIMIT)


def _full(shape):
    nd = len(shape)
    return pl.BlockSpec(shape, lambda *_: (0,) * nd)


def _rms(x, g):
    return x * lax.rsqrt(jnp.mean(x * x, axis=-1, keepdims=True) + EPS) * g


def _gelu(x):
    c = math.sqrt(2.0 / math.pi)
    return 0.5 * x * (1.0 + jnp.tanh(c * (x + 0.044715 * (x * x * x))))


def _dot(a, b):
    return jnp.dot(a, b, preferred_element_type=F32)


def _dot_nt(a, b):
    return lax.dot_general(a, b, NT_DIMS, preferred_element_type=F32)


N_ROPE_Q = D_NSA // LANES
N_KV_CHUNKS = 6
D_QKV = D_NSA + N_KV_CHUNKS * D_KV
N_KV_OUT = 2 + 2 * NSA_KV_HEADS


def _inproj_kernel(x_ref, g_ref, w_ref, cos_ref, sa_ref, sb_ref, q_ref, kvc_ref, kv_ref, misc_ref):
    y = _rms(x_ref[...], g_ref[...])
    z = _dot(y.astype(BF16), w_ref[...])
    cos, sa, sb = cos_ref[...], sa_ref[...], sb_ref[...]

    def rope(c):
        return c * cos + pltpu.roll(c, ROT_HALF, 1) * sa + pltpu.roll(c, LANES - ROT_HALF, 1) * sb

    scale = HEAD_DIM ** -0.5
    for j in range(N_ROPE_Q):
        q_ref[:, j * LANES:(j + 1) * LANES] = (rope(z[:, j * LANES:(j + 1) * LANES]) * scale).astype(BF16)
    lane = lax.broadcasted_iota(jnp.int32, (z.shape[0], LANES), 1)
    out = 0
    for j in range(N_KV_CHUNKS):
        c = z[:, D_NSA + j * LANES:D_NSA + (j + 1) * LANES]
        if j % 2 == 0:
            c = rope(c)
        if j < 2:
            kvc_ref[:, j * LANES:(j + 1) * LANES] = c.astype(BF16)
        elif j % 2 == 0:
            kv_ref[:, out * LANES:(out + 1) * LANES] = c.astype(BF16)
            out += 1
    for j in (3, 5):
        c = z[:, D_NSA + j * LANES:D_NSA + (j + 1) * LANES]
        for k in range(NSA_KV_HEADS):
            own = (lane >= HEAD_DIM) if k == 1 else (lane < HEAD_DIM)
            kv_ref[:, out * LANES:(out + 1) * LANES] = jnp.where(own, c, 1.0).astype(BF16)
            out += 1
    misc_ref[...] = z[:, D_QKV:]


def _inproj(x2, g, w_pad, cos_t, sa_t, sb_t, tm):
    T, D = x2.shape
    n = w_pad.shape[1]
    n_misc = n - D_QKV
    row = lambda w: pl.BlockSpec((tm, w), lambda i: (i, 0))
    return pl.pallas_call(
        _inproj_kernel,
        grid=(T // tm,),
        in_specs=[row(D), _full((1, D)), _full((D, n)), row(LANES), row(LANES), row(LANES)],
        out_specs=[row(D_NSA), row(2 * LANES), row(N_KV_OUT * LANES), row(n_misc)],
        out_shape=[jax.ShapeDtypeStruct((T, D_NSA), BF16),
                   jax.ShapeDtypeStruct((T, 2 * LANES), BF16),
                   jax.ShapeDtypeStruct((T, N_KV_OUT * LANES), BF16),
                   jax.ShapeDtypeStruct((T, n_misc), F32)],
        compiler_params=_params("parallel"),
    )(x2, g, w_pad, cos_t, sa_t, sb_t)


def _compress_kernel(x_ref, pe_ref, w1_ref, w2_ref, o_ref):
    x = x_ref[0].astype(F32)
    n = x.shape[0]
    out = jnp.zeros((n, LANES), F32)
    for hd in range(NSA_KV_HEADS):
        a = _dot((x + pe_ref[0, hd, 0:1]).astype(BF16), w1_ref[0, hd, 0])
        b = _dot((x + pe_ref[0, hd, 1:2]).astype(BF16), w1_ref[0, hd, 1])
        hid = _gelu(a + pltpu.roll(b, n - 1, 0))
        out = out + _dot(hid.astype(BF16), w2_ref[0, hd])
    o_ref[0, 0] = out.astype(BF16)


def _compress(xc, pe, w1, w2):
    nb, nch, width = xc.shape
    hid = w1.shape[-1]
    return pl.pallas_call(
        _compress_kernel,
        grid=(2, nb),
        in_specs=[pl.BlockSpec((1, nch, width), lambda w, i: (i, 0, 0)),
                  pl.BlockSpec((1, NSA_KV_HEADS, 2, width), lambda w, i: (w, 0, 0, 0)),
                  pl.BlockSpec((1, NSA_KV_HEADS, 2, width, hid), lambda w, i: (w, 0, 0, 0, 0)),
                  pl.BlockSpec((1, NSA_KV_HEADS, hid, LANES), lambda w, i: (w, 0, 0, 0))],
        out_specs=pl.BlockSpec((1, 1, nch, LANES), lambda w, i: (w, i, 0, 0)),
        out_shape=jax.ShapeDtypeStruct((2, nb, nch, LANES), BF16),
        compiler_params=_params("parallel", "parallel"),
    )(xc, pe, w1, w2)


def _compress_weights(pe, w1, w2):
    hid = w1.shape[-1]
    eye_t = jnp.eye(2, dtype=F32)[:, None, None, None, :, None, None]
    eye_h = jnp.eye(NSA_KV_HEADS, dtype=F32)[None, :, None, None, None, :, None]
    per = (2, 1, 2, CMP_STRIDE, 1, 1, HEAD_DIM)
    width = CMP_STRIDE * 2 * NSA_KV_HEADS * HEAD_DIM
    pe_big = (pe.reshape(per) * eye_t * eye_h).reshape(2, NSA_KV_HEADS, 2, width)
    w1_big = (w1.reshape(per + (hid,)) * eye_t[..., None] * eye_h[..., None]).reshape(2, NSA_KV_HEADS, 2, width, hid)
    eye_o = jnp.eye(NSA_KV_HEADS, dtype=F32)[None, :, None, :, None]
    w2_big = (w2[:, None, :, None, :] * eye_o).reshape(2, NSA_KV_HEADS, hid, NSA_KV_HEADS * HEAD_DIM)
    return pe_big, w1_big.astype(BF16), w2_big.astype(BF16)


def _nsa_kernel(q_ref, kcmp_ref, vcmp_ref, ks_ref, kw_ref, vs0_ref, vs1_ref, vw0_ref, vw1_ref, gate_ref, ov_ref,
                o_ref, s_sc, mrun_sc, acc_sc, *, tq, kc_sel, kc_win):
    t0 = pl.program_id(1) * tq
    rows = NSA_GROUP * tq
    lane = lax.broadcasted_iota(jnp.int32, (tq, LANES), 1)
    lane_r = lax.broadcasted_iota(jnp.int32, (rows, LANES), 1)
    tpos = t0 + lax.broadcasted_iota(jnp.int32, (tq, 1), 0)
    gates = jax.nn.sigmoid(gate_ref[...])
    n_cmp = kcmp_ref.shape[1]
    n_sel = ks_ref.shape[1] // SEL_BLOCK
    vs_refs, vw_refs = (vs0_ref, vs1_ref), (vw0_ref, vw1_ref)

    def normalise(acc, own):
        return acc / jnp.where(own, pltpu.roll(acc, HEAD_DIM, 1), 1.0)

    def own_lanes(lanes, k):
        return (lanes >= HEAD_DIM) if k == 1 else (lanes < HEAD_DIM)

    def queries(k):
        parts = []
        for g in range(NSA_GROUP):
            hh = k * NSA_GROUP + g
            c = q_ref[:, (hh // 2) * LANES:(hh // 2 + 1) * LANES].astype(F32)
            if hh % 2 != k:
                c = pltpu.roll(c, HEAD_DIM, 1)
            parts.append(jnp.where(own_lanes(lane, k), c, 0.0))
        return jnp.concatenate(parts, axis=0).astype(BF16)

    def compressed(qs):
        s = _dot_nt(qs, kcmp_ref[0]).reshape(NSA_GROUP, tq, n_cmp)
        cmp_end = lax.broadcasted_iota(jnp.int32, (tq, n_cmp), 1) * CMP_STRIDE + (CMP_BLOCK - 1)
        valid = cmp_end <= tpos
        s = s + jnp.where(valid, 0.0, NEG)[None]
        e = jnp.exp(s - jnp.max(s, axis=-1, keepdims=True)) * valid.astype(F32)[None]
        l = jnp.sum(e, axis=-1, keepdims=True)
        p3 = e * (1.0 / jnp.where(l > 0.0, l, 1.0))
        o_cmp = _dot(p3.reshape(rows, n_cmp).astype(BF16), vcmp_ref[0])

        psum = p3[0] + p3[1] + p3[2] + p3[3]
        ov = ov_ref[...]
        p_hi = psum.astype(BF16)
        r1 = psum - p_hi.astype(F32)
        p_mid = r1.astype(BF16)
        p_lo = (r1 - p_mid.astype(F32)).astype(BF16)
        imp = _dot(p_hi, ov) + _dot(p_mid, ov) + _dot(p_lo, ov)
        imp = imp.T[:n_sel]
        blk = lax.broadcasted_iota(jnp.int32, (n_sel, tq), 0)
        tpos_t = t0 + lax.broadcasted_iota(jnp.int32, (1, tq), 1)
        cur = tpos_t // SEL_BLOCK
        forced = (blk == 0) | (blk == cur) | (blk == cur - 1)
        imp = jnp.where(blk * SEL_BLOCK <= tpos_t, jnp.where(forced, FORCE, imp), NEG)
        blk_f = blk.astype(F32)
        for _ in range(SEL_TOPN):
            m = jnp.max(imp, axis=0, keepdims=True)
            first = jnp.min(jnp.where(imp == m, blk_f, float(n_sel)), axis=0, keepdims=True)
            imp = jnp.where(blk_f == first, LOWEST, imp)
        sel_t = jnp.concatenate([(imp == LOWEST).astype(F32), jnp.zeros((LANES - n_sel, tq), F32)], axis=0)
        return o_cmp, sel_t.T.astype(BF16)

    def window(k, qs):
        w0 = pl.multiple_of(jnp.maximum(t0 + tq - kc_win, 0), tq)
        sw = _dot_nt(qs, kw_ref[0, pl.ds(w0, kc_win), :]).reshape(NSA_GROUP, tq, kc_win)
        wpos = w0 + lax.broadcasted_iota(jnp.int32, (tq, kc_win), 1)
        in_win = (wpos <= tpos) & (wpos > tpos - WINDOW)
        sw = (sw + jnp.where(in_win, 0.0, NEG)[None]).reshape(rows, kc_win)
        pw = jnp.exp(sw - jnp.max(sw, axis=1, keepdims=True))
        return normalise(_dot(pw.astype(BF16), vw_refs[k][0, pl.ds(w0, kc_win), :]), own_lanes(lane_r, k))

    heads = range(NSA_KV_HEADS)
    qs = [queries(k) for k in heads]
    o_cmp, sel = zip(*[compressed(qs[k]) for k in heads])
    o_win = [window(k, qs[k]) for k in heads]

    n_chunks = (t0 + tq + kc_sel - 1) // kc_sel
    mrun_sc[...] = jnp.full(mrun_sc.shape, NEG, F32)

    def scores(c, carry):
        k0 = pl.multiple_of(c * kc_sel, kc_sel)
        jrow = lax.broadcasted_iota(jnp.int32, (LANES, kc_sel), 0)
        kcol = lax.broadcasted_iota(jnp.int32, (LANES, kc_sel), 1)
        expand = (jrow == k0 // SEL_BLOCK + kcol // SEL_BLOCK).astype(BF16)
        causal = k0 + lax.broadcasted_iota(jnp.int32, (tq, kc_sel), 1) <= tpos
        kb = ks_ref[0, pl.ds(k0, kc_sel), :]
        for k in heads:
            seen = (_dot(sel[k], expand) > 0.5) & causal
            s = _dot_nt(qs[k], kb).reshape(NSA_GROUP, tq, kc_sel)
            s = (s + jnp.where(seen, 0.0, NEG)[None]).reshape(rows, kc_sel)
            s_sc[k, c] = s
            m = mrun_sc[k]
            for j in range(kc_sel // LANES):
                m = jnp.maximum(m, s[:, j * LANES:(j + 1) * LANES])
            mrun_sc[k] = m
        return carry

    lax.fori_loop(0, n_chunks, scores, 0)
    m_sel = [jnp.broadcast_to(jnp.max(mrun_sc[k], axis=1, keepdims=True), (rows, LANES)) for k in heads]
    acc_sc[...] = jnp.zeros(acc_sc.shape, F32)

    def weigh(c, carry):
        k0 = pl.multiple_of(c * kc_sel, kc_sel)
        for k in heads:
            s = s_sc[k, c]
            p = jnp.concatenate([jnp.exp(s[:, j * LANES:(j + 1) * LANES] - m_sel[k])
                                 for j in range(kc_sel // LANES)], axis=1)
            acc_sc[k] += _dot(p.astype(BF16), vs_refs[k][0, pl.ds(k0, kc_sel), :])
        return carry

    lax.fori_loop(0, n_chunks, weigh, 0)

    for k in heads:
        o_sel = normalise(acc_sc[k], own_lanes(lane_r, k))
        outs = []
        for g in range(NSA_GROUP):
            r = slice(g * tq, (g + 1) * tq)
            gi = (k * NSA_GROUP + g) * 3
            og = (gates[:, gi:gi + 1] * o_cmp[k][r] + gates[:, gi + 1:gi + 2] * o_sel[r]
                  + gates[:, gi + 2:gi + 3] * o_win[k][r])
            if g % 2 != k:
                og = pltpu.roll(og, HEAD_DIM, 1)
            outs.append(og)
        for j in range(NSA_GROUP // 2):
            chunk = jnp.where(lane < HEAD_DIM, outs[2 * j], outs[2 * j + 1])
            cj = k * (NSA_GROUP // 2) + j
            o_ref[:, cj * LANES:(cj + 1) * LANES] = chunk.astype(BF16)


def _nsa(q, kcmp, vcmp, kv, gates_misc, overlap, B, S, tq):
    kc_sel, kc_win = KC_SEL, WINDOW + tq
    nq = S // tq
    n_cmp = kcmp.shape[1]
    rows = NSA_GROUP * tq
    kvspec = lambda j: pl.BlockSpec((1, S, LANES), lambda b, i: (b, 0, j), pipeline_mode=pl.Buffered(1))
    return pl.pallas_call(
        functools.partial(_nsa_kernel, tq=tq, kc_sel=kc_sel, kc_win=kc_win),
        grid=(B, nq),
        in_specs=[pl.BlockSpec((tq, D_NSA), lambda b, i: (b * nq + i, 0)),
                  pl.BlockSpec((1, n_cmp, LANES), lambda b, i: (b, 0, 0)),
                  pl.BlockSpec((1, n_cmp, LANES), lambda b, i: (b, 0, 0)),
                  kvspec(0), kvspec(1), kvspec(2), kvspec(3), kvspec(4), kvspec(5),
                  pl.BlockSpec((tq, LANES), lambda b, i: (b * nq + i, 0)),
                  _full(overlap.shape)],
        out_specs=pl.BlockSpec((tq, D_NSA), lambda b, i: (b * nq + i, 0)),
        out_shape=jax.ShapeDtypeStruct((B * S, D_NSA), BF16),
        scratch_shapes=[pltpu.VMEM((NSA_KV_HEADS, S // kc_sel, rows, kc_sel), F32),
                        pltpu.VMEM((NSA_KV_HEADS, rows, LANES), F32), pltpu.VMEM((NSA_KV_HEADS, rows, LANES), F32)],
        compiler_params=_params("parallel", "parallel"),
    )(q, kcmp, vcmp, kv, kv, kv, kv, kv, kv, gates_misc, overlap)


HALO = 32


def _mixout_kernel(mc_ref, mp_ref, on_ref, x_ref, cw_ref, cb_ref, lg_ref, lb_ref, pw_ref, pb_ref, plw_ref,
                   pls_ref, wo_ref, o_ref, hbuf, pbuf, hsh, *, ts, dc):
    i = pl.program_id(1)
    first = i == 0
    off_a, off_b, off_p = LANES, LANES + dc, LANES + 2 * dc

    def glu(ref, r):
        return ref[r, off_a:off_a + dc] * jax.nn.sigmoid(ref[r, off_b:off_b + dc])

    tail = slice(ts - HALO, ts)
    hbuf[0:HALO, :] = jnp.where(first, 0.0, glu(mp_ref, tail))
    hbuf[HALO:, :] = glu(mc_ref, slice(None))
    pbuf[0:HALO, :] = jnp.where(first, 0.0, mp_ref[tail, off_p:off_p + dc])
    pcur = mc_ref[:, off_p:off_p + dc]
    pbuf[HALO:, :] = pcur

    shifted = ts + HALO - SUBLANES
    for r in range(1, SUBLANES):
        hsh[r, 0:shifted, :] = hbuf[pl.ds(r, shifted), :]
    acc = jnp.zeros((ts, dc), F32) + cb_ref[...]
    for w in range(CONV_WIDTH):
        q, r = divmod(HALO - (CONV_WIDTH - 1) + w, SUBLANES)
        tap = hbuf[pl.ds(q * SUBLANES, ts), :] if r == 0 else hsh[r, pl.ds(q * SUBLANES, ts), :]
        acc = acc + tap * cw_ref[w:w + 1, :]
    mu = jnp.mean(acc, axis=-1, keepdims=True)
    xc = acc - mu
    y = xc * lax.rsqrt(jnp.mean(xc * xc, axis=-1, keepdims=True) + EPS) * lg_ref[...] + lb_ref[...]
    y = y * jax.nn.sigmoid(y)
    o_conv = _dot(y.astype(BF16), pw_ref[...]) + pb_ref[...]

    tglob = i * ts + lax.broadcasted_iota(jnp.int32, (ts, dc), 0)
    lane = lax.broadcasted_iota(jnp.int32, (ts, dc), 1)
    pg = dc // len(POOL_WINDOWS)
    run = pcur
    d = 1
    mean = jnp.zeros((ts, dc), F32)
    for gi, w in enumerate(POOL_WINDOWS):
        while d < w:
            run = run + pbuf[pl.ds(HALO - d, ts), :]
            d += 1
        cnt = jnp.minimum(tglob + 1, w).astype(F32)
        mean = jnp.where(lane // pg == gi, run / cnt, mean)
    o_pool = _dot((mean - pcur).astype(BF16), plw_ref[...]) * pls_ref[...]

    dn = on_ref.shape[1]
    o_ref[...] = (x_ref[...] + _dot(on_ref[...], wo_ref[0:dn, :])
                  + _dot(o_conv.astype(BF16), wo_ref[dn:dn + dc, :])
                  + _dot(o_pool.astype(BF16), wo_ref[dn + dc:, :]))


def _mixout(misc, o_nsa, x2, cw, cb, lg, lb, pw, pb, plw, pls, wo, B, S, ts):
    T, D = x2.shape
    dc = cw.shape[1]
    ns = S // ts
    nm = misc.shape[1]
    cur = lambda w: pl.BlockSpec((ts, w), lambda b, i: (b * ns + i, 0))
    prev = pl.BlockSpec((ts, nm), lambda b, i: (b * ns + jnp.maximum(i - 1, 0), 0))
    return pl.pallas_call(
        functools.partial(_mixout_kernel, ts=ts, dc=dc),
        grid=(B, ns),
        in_specs=[cur(nm), prev, cur(o_nsa.shape[1]), cur(D), _full(cw.shape), _full(cb.shape), _full(lg.shape),
                  _full(lb.shape), _full(pw.shape), _full(pb.shape), _full(plw.shape), _full(pls.shape),
                  _full(wo.shape)],
        out_specs=cur(D),
        out_shape=jax.ShapeDtypeStruct((T, D), F32),
        scratch_shapes=[pltpu.VMEM((ts + HALO, dc), F32), pltpu.VMEM((ts + HALO, dc), F32),
                        pltpu.VMEM((SUBLANES, ts + HALO, dc), F32)],
        compiler_params=_params("parallel", "parallel"),
    )(misc, misc, o_nsa, x2, cw, cb, lg, lb, pw, pb, plw, pls, wo)


def _norm_matmul_kernel(x_ref, g_ref, w_ref, o_ref):
    o_ref[...] = _dot(_rms(x_ref[...], g_ref[...]).astype(BF16), w_ref[...]).astype(o_ref.dtype)


def _norm_matmul(x2, g, w, tm, out_dtype):
    T, D = x2.shape
    n = w.shape[1]
    return pl.pallas_call(
        _norm_matmul_kernel,
        grid=(T // tm,),
        in_specs=[pl.BlockSpec((tm, D), lambda i: (i, 0)), _full((1, D)), _full((D, n))],
        out_specs=pl.BlockSpec((tm, n), lambda i: (i, 0)),
        out_shape=jax.ShapeDtypeStruct((T, n), out_dtype),
        compiler_params=_params("parallel"),
    )(x2, g, w)


def _xattn_kernel(x_ref, g_ref, wq_ref, k_ref, v_ref, wo_ref, o_ref):
    x = x_ref[...]
    D = x.shape[1]
    dh = D // XA_HEADS
    q = _dot(_rms(x, g_ref[...]).astype(BF16), wq_ref[...]) * (dh ** -0.5)
    outs = []
    for h in range(XA_HEADS):
        c = slice(h * dh, (h + 1) * dh)
        s = _dot_nt(q[:, c].astype(BF16), k_ref[0, :, c])
        e = jnp.exp(s - jnp.max(s, axis=-1, keepdims=True))
        p = e * (1.0 / jnp.sum(e, axis=-1, keepdims=True))
        outs.append(_dot(p.astype(BF16), v_ref[0, :, c]))
    o = jnp.concatenate(outs, axis=1)
    o_ref[...] = x + _dot(o.astype(BF16), wo_ref[...])


def _xattn(x2, g, wq, memkv, wo, B, S, tm):
    T, D = x2.shape
    ns = S // tm
    M = memkv.shape[1]
    return pl.pallas_call(
        _xattn_kernel,
        grid=(B, ns),
        in_specs=[pl.BlockSpec((tm, D), lambda b, i: (b * ns + i, 0)), _full((1, D)), _full((D, D)),
                  pl.BlockSpec((1, M, D), lambda b, i: (b, 0, 0)),
                  pl.BlockSpec((1, M, D), lambda b, i: (b, 0, 1)),
                  _full((D, D))],
        out_specs=pl.BlockSpec((tm, D), lambda b, i: (b * ns + i, 0)),
        out_shape=jax.ShapeDtypeStruct((T, D), F32),
        compiler_params=_params("parallel", "parallel"),
    )(x2, g, wq, memkv, memkv, wo)


PAIR_LIST = tuple((a, b) for a in range(PEER_TOPK) for b in range(PEER_TOPK) if (a + 1) * (b + 1) <= PEER_TOPK)
N_PAIR_ROWS = -(-len(PAIR_LIST) // 8) * 8
PAIR_COUNT = tuple(PEER_TOPK // (a + 1) for a in range(PEER_TOPK))
PAIR_START = tuple(sum(PAIR_COUNT[:a]) for a in range(PEER_TOPK))


def _top_rows_exact(v, n):
    R = v.shape[0]
    ridx = lax.broadcasted_iota(jnp.int32, v.shape, 0).astype(F32)
    rank = jnp.full(v.shape, float(n), F32)
    vals = []
    for r in range(n):
        m = jnp.max(v, axis=0, keepdims=True)
        first = jnp.min(jnp.where(v == m, ridx, float(R)), axis=0, keepdims=True)
        taken = ridx == first
        v = jnp.where(taken, LOWEST, v)
        rank = jnp.where(taken, float(r), rank)
        vals.append(m)
    return vals, rank


def _top_rows_distinct(v, n):
    rank = jnp.full(v.shape, float(n), F32)
    vals = []
    for r in range(n):
        m = jnp.max(v, axis=0, keepdims=True)
        taken = v == m
        v = jnp.where(taken, LOWEST, v)
        rank = jnp.where(taken, float(r), rank)
        vals.append(m)
    return vals, rank


def _peer_route_kernel(x_ref, g_ref, wqt_ref, sk_ref, hn_ref, c1_ref, g1_ref, r2_ref, g2_ref, qt_sc, s_sc, top_sc,
                       rank_sc, cand_sc, cw_sc):
    hn = _rms(x_ref[...], g_ref[...]).astype(BF16)
    hn_ref[...] = hn
    qt_sc[...] = _dot_nt(wqt_ref[...], hn).astype(BF16)
    nk, n = PEER_KEYS, PEER_TOPK
    tm = x_ref.shape[0]

    def put(li, vals, rank):
        for r in range(n):
            top_sc[pl.ds(li * n + r, 1), :] = vals[r]
        rank_sc[pl.ds(pl.multiple_of(li * nk, nk), nk), :] = rank

    def one_head(h, carry):
        ranked = jnp.zeros((1, tm), F32)
        for half in range(2):
            li = 2 * h + half
            r0 = pl.multiple_of(li * nk, nk)
            s = _dot(sk_ref[half], qt_sc[pl.ds(r0, nk), :])
            s_sc[pl.ds(r0, nk), :] = s
            vals, rank = _top_rows_distinct(s, n)
            put(li, vals, rank)
            ranked = jnp.maximum(ranked, jnp.sum((rank < float(n)).astype(F32), axis=0, keepdims=True))

        @pl.when(jnp.max(ranked) > float(n))
        def _():
            for half in range(2):
                li = 2 * h + half
                put(li, *_top_rows_exact(s_sc[pl.ds(pl.multiple_of(li * nk, nk), nk), :], n))

        return carry

    lax.fori_loop(0, PEER_HEADS, one_head, 0)

    cand_sc[...] = jnp.full(cand_sc.shape, LOWEST, F32)
    cw_sc[...] = jnp.zeros(cw_sc.shape, F32)
    for h in range(PEER_HEADS):
        l1, l2 = 2 * h, 2 * h + 1
        top1 = [top_sc[l1 * n + a:l1 * n + a + 1, :] for a in range(n)]
        top2 = [top_sc[l2 * n + a:l2 * n + a + 1, :] for a in range(n)]
        e1 = [jnp.exp(t - top1[0]) for t in top1]
        e2 = [jnp.exp(t - top2[0]) for t in top2]
        for r, (a, b) in enumerate(PAIR_LIST):
            cand_sc[h, r:r + 1, :] = top1[a] + top2[b]
            cw_sc[h, r:r + 1, :] = e1[a] * e2[b]
        _, crank = _top_rows_exact(cand_sc[h], n)
        chosen = (crank < float(n)).astype(F32)
        z = jnp.sum(chosen * cw_sc[h], axis=0, keepdims=True)
        rank1 = rank_sc[l1 * nk:(l1 + 1) * nk, :].astype(BF16)
        count1 = jnp.zeros(rank1.shape, BF16)
        for a in range(n):
            n_a = jnp.sum(chosen[PAIR_START[a]:PAIR_START[a] + PAIR_COUNT[a]], axis=0, keepdims=True)
            count1 = jnp.where(rank1 == float(a), n_a.astype(BF16), count1)
        rows = slice(h * nk, (h + 1) * nk)
        c1_ref[rows, :] = count1.astype(F32)
        g1_ref[rows, :] = jnp.exp(s_sc[l1 * nk:(l1 + 1) * nk, :] - top1[0]) * (1.0 / z)
        r2_ref[rows, :] = rank_sc[l2 * nk:(l2 + 1) * nk, :].astype(BF16)
        g2_ref[rows, :] = jnp.exp(s_sc[l2 * nk:(l2 + 1) * nk, :] - top2[0]).astype(BF16)


def _peer_route(x2, g, wqt, sk, tm):
    T, D = x2.shape
    nr = PEER_HEADS * PEER_KEYS
    col = pl.BlockSpec((nr, tm), lambda i: (0, i))
    return pl.pallas_call(
        _peer_route_kernel,
        grid=(T // tm,),
        in_specs=[pl.BlockSpec((tm, D), lambda i: (i, 0)), _full((1, D)), _full(wqt.shape), _full(sk.shape)],
        out_specs=[pl.BlockSpec((tm, D), lambda i: (i, 0)), col, col, col, col],
        out_shape=[jax.ShapeDtypeStruct((T, D), BF16), jax.ShapeDtypeStruct((nr, T), F32),
                   jax.ShapeDtypeStruct((nr, T), F32), jax.ShapeDtypeStruct((nr, T), BF16),
                   jax.ShapeDtypeStruct((nr, T), BF16)],
        scratch_shapes=[pltpu.VMEM((2 * nr, tm), BF16), pltpu.VMEM((2 * nr, tm), F32),
                        pltpu.VMEM((2 * PEER_HEADS * PEER_TOPK, tm), F32),
                        pltpu.VMEM((2 * nr, tm), F32), pltpu.VMEM((PEER_HEADS, N_PAIR_ROWS, tm), F32),
                        pltpu.VMEM((PEER_HEADS, N_PAIR_ROWS, tm), F32)],
        compiler_params=_params("parallel"),
    )(x2, g, wqt, sk)


def _peer_expert_kernel(*refs, subs, final_norm):
    hn_ref, c1_ref, g1_ref, r2_ref, g2_ref, u_ref, vt_ref, x_ref = refs[:8]
    fg_ref = refs[8] if final_norm else None
    o_ref, acc_sc, wa_sc, act_sc = refs[-4:]
    c = pl.program_id(1)
    nk = PEER_KEYS
    te = sum(subs)
    offs = [sum(subs[:j]) for j in range(len(subs))]
    tm = hn_ref.shape[0]
    slots = act_sc.shape[0]

    @pl.when(c == 0)
    def _():
        acc_sc[...] = jnp.zeros(acc_sc.shape, F32)

    def pre_act(j):
        o, rows = offs[j], subs[j]
        act_sc[j % slots, 0:rows, :] = _dot_nt(u_ref[o:o + rows, :], hn_ref[...]).astype(BF16)

    for j in range(min(EXPERT_LOOKAHEAD, len(subs))):
        pre_act(j)
    for j, (o, rows) in enumerate(zip(offs, subs)):
        if j + EXPERT_LOOKAHEAD < len(subs):
            pre_act(j + EXPERT_LOOKAHEAD)
        for il in range(rows // nk):
            i = c * (te // nk) + o // nk + il
            w = jnp.zeros((nk, tm), BF16)
            for h in range(PEER_HEADS):
                count = c1_ref[pl.ds(h * nk + i, 1), :].astype(BF16)
                g1row = g1_ref[pl.ds(h * nk + i, 1), :].astype(BF16)
                hr = slice(h * nk, (h + 1) * nk)
                w = w + jnp.where(r2_ref[hr, :] < count, g2_ref[hr, :], 0.0) * g1row
            er = slice(il * nk, (il + 1) * nk)
            wa_sc[o + il * nk:o + (il + 1) * nk, :] = w * _gelu(act_sc[j % slots, er, :])
        acc_sc[...] += _dot(vt_ref[:, o:o + rows], wa_sc[o:o + rows, :])

    @pl.when(c == pl.num_programs(1) - 1)
    def _():
        out = x_ref[...] + acc_sc[...].T
        o_ref[...] = _rms(out, fg_ref[...]) if final_norm else out


def _peer_expert(hn, c1, g1, r2, g2, u, vt, layer, x2, final_g, tm, subs):
    T, D = x2.shape
    ne = u.shape[1]
    te = sum(subs)
    tok = pl.BlockSpec((c1.shape[0], tm), lambda i, c: (0, i))
    row = pl.BlockSpec((tm, D), lambda i, c: (i, 0))
    return pl.pallas_call(
        functools.partial(_peer_expert_kernel, subs=subs, final_norm=final_g is not None),
        grid=(T // tm, ne // te),
        in_specs=[row, tok, tok, tok, tok,
                  pl.BlockSpec((None, te, D), lambda i, c: (layer, c, 0)),
                  pl.BlockSpec((None, D, te), lambda i, c: (layer, 0, c)),
                  row] + ([] if final_g is None else [_full((1, D))]),
        out_specs=row,
        out_shape=jax.ShapeDtypeStruct((T, D), F32),
        scratch_shapes=[pltpu.VMEM((D, tm), F32), pltpu.VMEM((te, tm), BF16),
                        pltpu.VMEM((EXPERT_LOOKAHEAD + 1, max(subs), tm), BF16)],
        compiler_params=_params("parallel", "arbitrary"),
    )(hn, c1, g1, r2, g2, u, vt, x2, *([] if final_g is None else [final_g]))


def _rope_tables(positions):
    B, S = positions.shape
    freqs = ROPE_THETA ** (-jnp.arange(ROT_HALF, dtype=F32) * 2.0 / ROT_DIM)
    ang = positions.astype(F32)[:, :, None] * freqs
    cos, sin = jnp.cos(ang), jnp.sin(ang)
    ones = jnp.ones((B, S, HEAD_DIM - ROT_DIM), F32)
    zeros8 = jnp.zeros((B, S, ROT_HALF), F32)
    zeros = jnp.zeros((B, S, HEAD_DIM - ROT_DIM), F32)
    cos_h = jnp.concatenate([cos, cos, ones], axis=-1)
    sa_h = jnp.concatenate([zeros8, sin, zeros], axis=-1)
    sb_h = jnp.concatenate([-sin, zeros8, zeros], axis=-1)
    rep = LANES // HEAD_DIM
    tile = lambda t: jnp.tile(t, (1, 1, rep)).reshape(B * S, LANES)
    return tile(cos_h), tile(sa_h), tile(sb_h)


def _overlap_matrix(n_chunk, n_sel):
    ci = np.arange(n_chunk)[:, None] * CMP_STRIDE
    sj = np.arange(LANES)[None, :] * SEL_BLOCK
    ov = (ci < sj + SEL_BLOCK) & (ci + CMP_BLOCK > sj) & (np.arange(LANES)[None, :] < n_sel)
    ov = ov & (np.arange(n_chunk)[:, None] < n_chunk - 1)
    return jnp.asarray(ov, BF16)


def kernel(x, mem, positions, norm_mix_g, w_in, cmp_pe, cmp_w1, cmp_w2, conv_w, conv_b, conv_ln_g, conv_ln_b, conv_pw_w, conv_pw_b, pool_w, pool_scale, w_out, norm_xa_g, norm_mem_g, xa_wq, xa_wkv, xa_wo, norm_ffn_g, peer_wq, peer_subkeys, peer_u, peer_v, final_g):
    B, S, D = x.shape
    T = B * S
    depth = w_in.shape[0]
    M = mem.shape[1]
    dc = conv_w.shape[-1]
    n_chunk = S // CMP_STRIDE
    n_sel = S // SEL_BLOCK
    assert S % max(TM_PROJ, TS_MIX, KC_SEL, TM_PEER) == 0 and SEL_TOPN <= n_sel <= LANES and D % LANES == 0
    assert S >= WINDOW + TQ_NSA and peer_u.shape[1] % sum(EXPERT_SUBS) == 0

    cos_t, sa_t, sb_t = _rope_tables(positions)
    overlap = _overlap_matrix(n_chunk, n_sel)
    row = lambda v: v.reshape(1, -1)
    x2 = x.reshape(T, D)
    mem2 = mem.reshape(B * M, D)
    n_gate = 3 * NSA_HEADS

    w_pad = jnp.concatenate([w_in[:, :, :D_QKV + n_gate], jnp.zeros((depth, D, LANES - n_gate), F32),
                             w_in[:, :, D_QKV + n_gate:]], axis=2).astype(BF16)
    pg = dc // len(POOL_WINDOWS)
    plw = jnp.zeros((depth, dc, dc), F32)
    for gi in range(len(POOL_WINDOWS)):
        plw = plw.at[:, gi * pg:(gi + 1) * pg, gi * pg:(gi + 1) * pg].set(pool_w[:, gi])
    plw, conv_pw, wo = plw.astype(BF16), conv_pw_w.astype(BF16), w_out.astype(BF16)
    wkv, wq, wxo = xa_wkv.astype(BF16), xa_wq.astype(BF16), xa_wo.astype(BF16)
    pwq_t = jnp.swapaxes(peer_wq, 1, 2).astype(BF16)
    psk, pu = peer_subkeys.astype(BF16), peer_u.astype(BF16)
    pv_t = jnp.swapaxes(peer_v, 1, 2).astype(BF16)

    for l in range(depth):
        q, kvc, kv, misc = _inproj(x2, row(norm_mix_g[l]), w_pad[l], cos_t, sa_t, sb_t, tm=TM_PROJ)
        xc = kvc.reshape(B, n_chunk, CMP_STRIDE * 2 * LANES)
        cmp = _compress(xc, *_compress_weights(cmp_pe[l], cmp_w1[l], cmp_w2[l]))
        o_nsa = _nsa(q, cmp[0], cmp[1], kv.reshape(B, S, -1), misc, overlap, B, S, tq=TQ_NSA)
        x2 = _mixout(misc, o_nsa, x2, conv_w[l], row(conv_b[l]), row(conv_ln_g[l]), row(conv_ln_b[l]),
                     conv_pw[l], row(conv_pw_b[l]), plw[l], row(pool_scale[l]), wo[l], B, S, ts=TS_MIX)
        memkv = _norm_matmul(mem2, row(norm_mem_g[l]), wkv[l], tm=TM_MEM, out_dtype=BF16)
        x2 = _xattn(x2, row(norm_xa_g[l]), wq[l], memkv.reshape(B, M, 2 * D), wxo[l], B, S, tm=TM_PROJ)
        hn, c1, g1, r2, g2 = _peer_route(x2, row(norm_ffn_g[l]), pwq_t[l], psk[l], tm=TM_PEER)
        x2 = _peer_expert(hn, c1, g1, r2, g2, pu, pv_t, l, x2,
                          row(final_g) if l == depth - 1 else None, tm=TM_PEER, subs=EXPERT_SUBS)
    return x2.reshape(B, S, D)
```

```python
import functools
import math

import jax
import jax.numpy as jnp
import numpy as np
from jax import lax
from jax.experimental import pallas as pl
from jax.experimental.pallas import tpu as pltpu

F32 = jnp.float32
BF16 = jnp.bfloat16

NSA_HEADS = 8
NSA_KV_HEADS = 2
NSA_GROUP = NSA_HEADS // NSA_KV_HEADS
HEAD_DIM = 64
D_NSA = NSA_HEADS * HEAD_DIM
D_KV = NSA_KV_HEADS * HEAD_DIM
ROT_DIM = HEAD_DIM // 4
ROT_HALF = ROT_DIM // 2
ROPE_THETA = 500000.0
CMP_BLOCK = 32
CMP_STRIDE = 16
SEL_BLOCK = 64
SEL_TOPN = 16
WINDOW = 512
CONV_WIDTH = 31
POOL_WINDOWS = (2, 4, 8, 16)
XA_HEADS = 4
PEER_HEADS = 8
PEER_KEYS = 128
PEER_TOPK = 16
EPS = 1e-6
NEG = -1e30
FORCE = 1e4
LOWEST = -3.0e38

LANES = 128
SUBLANES = 8
VMEM_LIMIT = 56 * 1024 * 1024

NT_DIMS = (((1,), (1,)), ((), ()))

TM_PROJ = 512
TM_MEM = 256
TQ_NSA = 256
KC_SEL = 512
TS_MIX = 512
TM_PEER = 512
EXPERT_SUBS = (256, 768, 768, 256)
EXPERT_LOOKAHEAD = 3


def _params(*sem):
    return pltpu.CompilerParams(dimension_semantics=sem, vmem_limit_bytes=VMEM_LIMIT)


def _full(shape):
    nd = len(shape)
    return pl.BlockSpec(shape, lambda *_: (0,) * nd)


def _rms(x, g):
    return x * lax.rsqrt(jnp.mean(x * x, axis=-1, keepdims=True) + EPS) * g


def _gelu(x):
    c = math.sqrt(2.0 / math.pi)
    return 0.5 * x * (1.0 + jnp.tanh(c * (x + 0.044715 * (x * x * x))))


def _dot(a, b):
    return jnp.dot(a, b, preferred_element_type=F32)


def _dot_nt(a, b):
    return lax.dot_general(a, b, NT_DIMS, preferred_element_type=F32)


N_ROPE_Q = D_NSA // LANES
N_KV_CHUNKS = 6
D_QKV = D_NSA + N_KV_CHUNKS * D_KV
N_KV_OUT = 2 + 2 * NSA_KV_HEADS


def _inproj_kernel(x_ref, g_ref, w_ref, cos_ref, sa_ref, sb_ref, q_ref, kvc_ref, kv_ref, misc_ref):
    y = _rms(x_ref[...], g_ref[...])
    z = _dot(y.astype(BF16), w_ref[...])
    cos, sa, sb = cos_ref[...], sa_ref[...], sb_ref[...]

    def rope(c):
        return c * cos + pltpu.roll(c, ROT_HALF, 1) * sa + pltpu.roll(c, LANES - ROT_HALF, 1) * sb

    scale = HEAD_DIM ** -0.5
    for j in range(N_ROPE_Q):
        q_ref[:, j * LANES:(j + 1) * LANES] = (rope(z[:, j * LANES:(j + 1) * LANES]) * scale).astype(BF16)
    lane = lax.broadcasted_iota(jnp.int32, (z.shape[0], LANES), 1)
    out = 0
    for j in range(N_KV_CHUNKS):
        c = z[:, D_NSA + j * LANES:D_NSA + (j + 1) * LANES]
        if j % 2 == 0:
            c = rope(c)
        if j < 2:
            kvc_ref[:, j * LANES:(j + 1) * LANES] = c.astype(BF16)
        elif j % 2 == 0:
            kv_ref[:, out * LANES:(out + 1) * LANES] = c.astype(BF16)
            out += 1
    for j in (3, 5):
        c = z[:, D_NSA + j * LANES:D_NSA + (j + 1) * LANES]
        for k in range(NSA_KV_HEADS):
            own = (lane >= HEAD_DIM) if k == 1 else (lane < HEAD_DIM)
            kv_ref[:, out * LANES:(out + 1) * LANES] = jnp.where(own, c, 1.0).astype(BF16)
            out += 1
    misc_ref[...] = z[:, D_QKV:]


def _inproj(x2, g, w_pad, cos_t, sa_t, sb_t, tm):
    T, D = x2.shape
    n = w_pad.shape[1]
    n_misc = n - D_QKV
    row = lambda w: pl.BlockSpec((tm, w), lambda i: (i, 0))
    return pl.pallas_call(
        _inproj_kernel,
        grid=(T // tm,),
        in_specs=[row(D), _full((1, D)), _full((D, n)), row(LANES), row(LANES), row(LANES)],
        out_specs=[row(D_NSA), row(2 * LANES), row(N_KV_OUT * LANES), row(n_misc)],
        out_shape=[jax.ShapeDtypeStruct((T, D_NSA), BF16),
                   jax.ShapeDtypeStruct((T, 2 * LANES), BF16),
                   jax.ShapeDtypeStruct((T, N_KV_OUT * LANES), BF16),
                   jax.ShapeDtypeStruct((T, n_misc), F32)],
        compiler_params=_params("parallel"),
    )(x2, g, w_pad, cos_t, sa_t, sb_t)


def _compress_kernel(x_ref, pe_ref, w1_ref, w2_ref, o_ref):
    x = x_ref[0].astype(F32)
    n = x.shape[0]
    out = jnp.zeros((n, LANES), F32)
    for hd in range(NSA_KV_HEADS):
        a = _dot((x + pe_ref[0, hd, 0:1]).astype(BF16), w1_ref[0, hd, 0])
        b = _dot((x + pe_ref[0, hd, 1:2]).astype(BF16), w1_ref[0, hd, 1])
        hid = _gelu(a + pltpu.roll(b, n - 1, 0))
        out = out + _dot(hid.astype(BF16), w2_ref[0, hd])
    o_ref[0, 0] = out.astype(BF16)


def _compress(xc, pe, w1, w2):
    nb, nch, width = xc.shape
    hid = w1.shape[-1]
    return pl.pallas_call(
        _compress_kernel,
        grid=(2, nb),
        in_specs=[pl.BlockSpec((1, nch, width), lambda w, i: (i, 0, 0)),
                  pl.BlockSpec((1, NSA_KV_HEADS, 2, width), lambda w, i: (w, 0, 0, 0)),
                  pl.BlockSpec((1, NSA_KV_HEADS, 2, width, hid), lambda w, i: (w, 0, 0, 0, 0)),
                  pl.BlockSpec((1, NSA_KV_HEADS, hid, LANES), lambda w, i: (w, 0, 0, 0))],
        out_specs=pl.BlockSpec((1, 1, nch, LANES), lambda w, i: (w, i, 0, 0)),
        out_shape=jax.ShapeDtypeStruct((2, nb, nch, LANES), BF16),
        compiler_params=_params("parallel", "parallel"),
    )(xc, pe, w1, w2)


def _compress_weights(pe, w1, w2):
    hid = w1.shape[-1]
    eye_t = jnp.eye(2, dtype=F32)[:, None, None, None, :, None, None]
    eye_h = jnp.eye(NSA_KV_HEADS, dtype=F32)[None, :, None, None, None, :, None]
    per = (2, 1, 2, CMP_STRIDE, 1, 1, HEAD_DIM)
    width = CMP_STRIDE * 2 * NSA_KV_HEADS * HEAD_DIM
    pe_big = (pe.reshape(per) * eye_t * eye_h).reshape(2, NSA_KV_HEADS, 2, width)
    w1_big = (w1.reshape(per + (hid,)) * eye_t[..., None] * eye_h[..., None]).reshape(2, NSA_KV_HEADS, 2, width, hid)
    eye_o = jnp.eye(NSA_KV_HEADS, dtype=F32)[None, :, None, :, None]
    w2_big = (w2[:, None, :, None, :] * eye_o).reshape(2, NSA_KV_HEADS, hid, NSA_KV_HEADS * HEAD_DIM)
    return pe_big, w1_big.astype(BF16), w2_big.astype(BF16)


def _nsa_kernel(q_ref, kcmp_ref, vcmp_ref, ks_ref, kw_ref, vs0_ref, vs1_ref, vw0_ref, vw1_ref, gate_ref, ov_ref,
                o_ref, s_sc, mrun_sc, acc_sc, *, tq, kc_sel, kc_win):
    t0 = pl.program_id(1) * tq
    rows = NSA_GROUP * tq
    lane = lax.broadcasted_iota(jnp.int32, (tq, LANES), 1)
    lane_r = lax.broadcasted_iota(jnp.int32, (rows, LANES), 1)
    tpos = t0 + lax.broadcasted_iota(jnp.int32, (tq, 1), 0)
    gates = jax.nn.sigmoid(gate_ref[...])
    n_cmp = kcmp_ref.shape[1]
    n_sel = ks_ref.shape[1] // SEL_BLOCK
    vs_refs, vw_refs = (vs0_ref, vs1_ref), (vw0_ref, vw1_ref)

    def normalise(acc, own):
        return acc / jnp.where(own, pltpu.roll(acc, HEAD_DIM, 1), 1.0)

    def own_lanes(lanes, k):
        return (lanes >= HEAD_DIM) if k == 1 else (lanes < HEAD_DIM)

    def queries(k):
        parts = []
        for g in range(NSA_GROUP):
            hh = k * NSA_GROUP + g
            c = q_ref[:, (hh // 2) * LANES:(hh // 2 + 1) * LANES].astype(F32)
            if hh % 2 != k:
                c = pltpu.roll(c, HEAD_DIM, 1)
            parts.append(jnp.where(own_lanes(lane, k), c, 0.0))
        return jnp.concatenate(parts, axis=0).astype(BF16)

    def compressed(qs):
        s = _dot_nt(qs, kcmp_ref[0]).reshape(NSA_GROUP, tq, n_cmp)
        cmp_end = lax.broadcasted_iota(jnp.int32, (tq, n_cmp), 1) * CMP_STRIDE + (CMP_BLOCK - 1)
        valid = cmp_end <= tpos
        s = s + jnp.where(valid, 0.0, NEG)[None]
        e = jnp.exp(s - jnp.max(s, axis=-1, keepdims=True)) * valid.astype(F32)[None]
        l = jnp.sum(e, axis=-1, keepdims=True)
        p3 = e * (1.0 / jnp.where(l > 0.0, l, 1.0))
        o_cmp = _dot(p3.reshape(rows, n_cmp).astype(BF16), vcmp_ref[0])

        psum = p3[0] + p3[1] + p3[2] + p3[3]
        ov = ov_ref[...]
        p_hi = psum.astype(BF16)
        r1 = psum - p_hi.astype(F32)
        p_mid = r1.astype(BF16)
        p_lo = (r1 - p_mid.astype(F32)).astype(BF16)
        imp = _dot(p_hi, ov) + _dot(p_mid, ov) + _dot(p_lo, ov)
        imp = imp.T[:n_sel]
        blk = lax.broadcasted_iota(jnp.int32, (n_sel, tq), 0)
        tpos_t = t0 + lax.broadcasted_iota(jnp.int32, (1, tq), 1)
        cur = tpos_t // SEL_BLOCK
        forced = (blk == 0) | (blk == cur) | (blk == cur - 1)
        imp = jnp.where(blk * SEL_BLOCK <= tpos_t, jnp.where(forced, FORCE, imp), NEG)
        blk_f = blk.astype(F32)
        for _ in range(SEL_TOPN):
            m = jnp.max(imp, axis=0, keepdims=True)
            first = jnp.min(jnp.where(imp == m, blk_f, float(n_sel)), axis=0, keepdims=True)
            imp = jnp.where(blk_f == first, LOWEST, imp)
        sel_t = jnp.concatenate([(imp == LOWEST).astype(F32), jnp.zeros((LANES - n_sel, tq), F32)], axis=0)
        return o_cmp, sel_t.T.astype(BF16)

    def window(k, qs):
        w0 = pl.multiple_of(jnp.maximum(t0 + tq - kc_win, 0), tq)
        sw = _dot_nt(qs, kw_ref[0, pl.ds(w0, kc_win), :]).reshape(NSA_GROUP, tq, kc_win)
        wpos = w0 + lax.broadcasted_iota(jnp.int32, (tq, kc_win), 1)
        in_win = (wpos <= tpos) & (wpos > tpos - WINDOW)
        sw = (sw + jnp.where(in_win, 0.0, NEG)[None]).reshape(rows, kc_win)
        pw = jnp.exp(sw - jnp.max(sw, axis=1, keepdims=True))
        return normalise(_dot(pw.astype(BF16), vw_refs[k][0, pl.ds(w0, kc_win), :]), own_lanes(lane_r, k))

    heads = range(NSA_KV_HEADS)
    qs = [queries(k) for k in heads]
    o_cmp, sel = zip(*[compressed(qs[k]) for k in heads])
    o_win = [window(k, qs[k]) for k in heads]

    n_chunks = (t0 + tq + kc_sel - 1) // kc_sel
    mrun_sc[...] = jnp.full(mrun_sc.shape, NEG, F32)

    def scores(c, carry):
        k0 = pl.multiple_of(c * kc_sel, kc_sel)
        jrow = lax.broadcasted_iota(jnp.int32, (LANES, kc_sel), 0)
        kcol = lax.broadcasted_iota(jnp.int32, (LANES, kc_sel), 1)
        expand = (jrow == k0 // SEL_BLOCK + kcol // SEL_BLOCK).astype(BF16)
        causal = k0 + lax.broadcasted_iota(jnp.int32, (tq, kc_sel), 1) <= tpos
        kb = ks_ref[0, pl.ds(k0, kc_sel), :]
        for k in heads:
            seen = (_dot(sel[k], expand) > 0.5) & causal
            s = _dot_nt(qs[k], kb).reshape(NSA_GROUP, tq, kc_sel)
            s = (s + jnp.where(seen, 0.0, NEG)[None]).reshape(rows, kc_sel)
            s_sc[k, c] = s
            m = mrun_sc[k]
            for j in range(kc_sel // LANES):
                m = jnp.maximum(m, s[:, j * LANES:(j + 1) * LANES])
            mrun_sc[k] = m
        return carry

    lax.fori_loop(0, n_chunks, scores, 0)
    m_sel = [jnp.broadcast_to(jnp.max(mrun_sc[k], axis=1, keepdims=True), (rows, LANES)) for k in heads]
    acc_sc[...] = jnp.zeros(acc_sc.shape, F32)

    def weigh(c, carry):
        k0 = pl.multiple_of(c * kc_sel, kc_sel)
        for k in heads:
            s = s_sc[k, c]
            p = jnp.concatenate([jnp.exp(s[:, j * LANES:(j + 1) * LANES] - m_sel[k])
                                 for j in range(kc_sel // LANES)], axis=1)
            acc_sc[k] += _dot(p.astype(BF16), vs_refs[k][0, pl.ds(k0, kc_sel), :])
        return carry

    lax.fori_loop(0, n_chunks, weigh, 0)

    for k in heads:
        o_sel = normalise(acc_sc[k], own_lanes(lane_r, k))
        outs = []
        for g in range(NSA_GROUP):
            r = slice(g * tq, (g + 1) * tq)
            gi = (k * NSA_GROUP + g) * 3
            og = (gates[:, gi:gi + 1] * o_cmp[k][r] + gates[:, gi + 1:gi + 2] * o_sel[r]
                  + gates[:, gi + 2:gi + 3] * o_win[k][r])
            if g % 2 != k:
                og = pltpu.roll(og, HEAD_DIM, 1)
            outs.append(og)
        for j in range(NSA_GROUP // 2):
            chunk = jnp.where(lane < HEAD_DIM, outs[2 * j], outs[2 * j + 1])
            cj = k * (NSA_GROUP // 2) + j
            o_ref[:, cj * LANES:(cj + 1) * LANES] = chunk.astype(BF16)


def _nsa(q, kcmp, vcmp, kv, gates_misc, overlap, B, S, tq):
    kc_sel, kc_win = KC_SEL, WINDOW + tq
    nq = S // tq
    n_cmp = kcmp.shape[1]
    rows = NSA_GROUP * tq
    kvspec = lambda j: pl.BlockSpec((1, S, LANES), lambda b, i: (b, 0, j), pipeline_mode=pl.Buffered(1))
    return pl.pallas_call(
        functools.partial(_nsa_kernel, tq=tq, kc_sel=kc_sel, kc_win=kc_win),
        grid=(B, nq),
        in_specs=[pl.BlockSpec((tq, D_NSA), lambda b, i: (b * nq + i, 0)),
                  pl.BlockSpec((1, n_cmp, LANES), lambda b, i: (b, 0, 0)),
                  pl.BlockSpec((1, n_cmp, LANES), lambda b, i: (b, 0, 0)),
                  kvspec(0), kvspec(1), kvspec(2), kvspec(3), kvspec(4), kvspec(5),
                  pl.BlockSpec((tq, LANES), lambda b, i: (b * nq + i, 0)),
                  _full(overlap.shape)],
        out_specs=pl.BlockSpec((tq, D_NSA), lambda b, i: (b * nq + i, 0)),
        out_shape=jax.ShapeDtypeStruct((B * S, D_NSA), BF16),
        scratch_shapes=[pltpu.VMEM((NSA_KV_HEADS, S // kc_sel, rows, kc_sel), F32),
                        pltpu.VMEM((NSA_KV_HEADS, rows, LANES), F32), pltpu.VMEM((NSA_KV_HEADS, rows, LANES), F32)],
        compiler_params=_params("parallel", "parallel"),
    )(q, kcmp, vcmp, kv, kv, kv, kv, kv, kv, gates_misc, overlap)


HALO = 32


def _mixout_kernel(mc_ref, mp_ref, on_ref, x_ref, cw_ref, cb_ref, lg_ref, lb_ref, pw_ref, pb_ref, plw_ref,
                   pls_ref, wo_ref, o_ref, hbuf, pbuf, hsh, *, ts, dc):
    i = pl.program_id(1)
    first = i == 0
    off_a, off_b, off_p = LANES, LANES + dc, LANES + 2 * dc

    def glu(ref, r):
        return ref[r, off_a:off_a + dc] * jax.nn.sigmoid(ref[r, off_b:off_b + dc])

    tail = slice(ts - HALO, ts)
    hbuf[0:HALO, :] = jnp.where(first, 0.0, glu(mp_ref, tail))
    hbuf[HALO:, :] = glu(mc_ref, slice(None))
    pbuf[0:HALO, :] = jnp.where(first, 0.0, mp_ref[tail, off_p:off_p + dc])
    pcur = mc_ref[:, off_p:off_p + dc]
    pbuf[HALO:, :] = pcur

    shifted = ts + HALO - SUBLANES
    for r in range(1, SUBLANES):
        hsh[r, 0:shifted, :] = hbuf[pl.ds(r, shifted), :]
    acc = jnp.zeros((ts, dc), F32) + cb_ref[...]
    for w in range(CONV_WIDTH):
        q, r = divmod(HALO - (CONV_WIDTH - 1) + w, SUBLANES)
        tap = hbuf[pl.ds(q * SUBLANES, ts), :] if r == 0 else hsh[r, pl.ds(q * SUBLANES, ts), :]
        acc = acc + tap * cw_ref[w:w + 1, :]
    mu = jnp.mean(acc, axis=-1, keepdims=True)
    xc = acc - mu
    y = xc * lax.rsqrt(jnp.mean(xc * xc, axis=-1, keepdims=True) + EPS) * lg_ref[...] + lb_ref[...]
    y = y * jax.nn.sigmoid(y)
    o_conv = _dot(y.astype(BF16), pw_ref[...]) + pb_ref[...]

    tglob = i * ts + lax.broadcasted_iota(jnp.int32, (ts, dc), 0)
    lane = lax.broadcasted_iota(jnp.int32, (ts, dc), 1)
    pg = dc // len(POOL_WINDOWS)
    run = pcur
    d = 1
    mean = jnp.zeros((ts, dc), F32)
    for gi, w in enumerate(POOL_WINDOWS):
        while d < w:
            run = run + pbuf[pl.ds(HALO - d, ts), :]
            d += 1
        cnt = jnp.minimum(tglob + 1, w).astype(F32)
        mean = jnp.where(lane // pg == gi, run / cnt, mean)
    o_pool = _dot((mean - pcur).astype(BF16), plw_ref[...]) * pls_ref[...]

    dn = on_ref.shape[1]
    o_ref[...] = (x_ref[...] + _dot(on_ref[...], wo_ref[0:dn, :])
                  + _dot(o_conv.astype(BF16), wo_ref[dn:dn + dc, :])
                  + _dot(o_pool.astype(BF16), wo_ref[dn + dc:, :]))


def _mixout(misc, o_nsa, x2, cw, cb, lg, lb, pw, pb, plw, pls, wo, B, S, ts):
    T, D = x2.shape
    dc = cw.shape[1]
    ns = S // ts
    nm = misc.shape[1]
    cur = lambda w: pl.BlockSpec((ts, w), lambda b, i: (b * ns + i, 0))
    prev = pl.BlockSpec((ts, nm), lambda b, i: (b * ns + jnp.maximum(i - 1, 0), 0))
    return pl.pallas_call(
        functools.partial(_mixout_kernel, ts=ts, dc=dc),
        grid=(B, ns),
        in_specs=[cur(nm), prev, cur(o_nsa.shape[1]), cur(D), _full(cw.shape), _full(cb.shape), _full(lg.shape),
                  _full(lb.shape), _full(pw.shape), _full(pb.shape), _full(plw.shape), _full(pls.shape),
                  _full(wo.shape)],
        out_specs=cur(D),
        out_shape=jax.ShapeDtypeStruct((T, D), F32),
        scratch_shapes=[pltpu.VMEM((ts + HALO, dc), F32), pltpu.VMEM((ts + HALO, dc), F32),
                        pltpu.VMEM((SUBLANES, ts + HALO, dc), F32)],
        compiler_params=_params("parallel", "parallel"),
    )(misc, misc, o_nsa, x2, cw, cb, lg, lb, pw, pb, plw, pls, wo)


def _norm_matmul_kernel(x_ref, g_ref, w_ref, o_ref):
    o_ref[...] = _dot(_rms(x_ref[...], g_ref[...]).astype(BF16), w_ref[...]).astype(o_ref.dtype)


def _norm_matmul(x2, g, w, tm, out_dtype):
    T, D = x2.shape
    n = w.shape[1]
    return pl.pallas_call(
        _norm_matmul_kernel,
        grid=(T // tm,),
        in_specs=[pl.BlockSpec((tm, D), lambda i: (i, 0)), _full((1, D)), _full((D, n))],
        out_specs=pl.BlockSpec((tm, n), lambda i: (i, 0)),
        out_shape=jax.ShapeDtypeStruct((T, n), out_dtype),
        compiler_params=_params("parallel"),
    )(x2, g, w)


def _xattn_kernel(x_ref, g_ref, wq_ref, k_ref, v_ref, wo_ref, o_ref):
    x = x_ref[...]
    D = x.shape[1]
    dh = D // XA_HEADS
    q = _dot(_rms(x, g_ref[...]).astype(BF16), wq_ref[...]) * (dh ** -0.5)
    outs = []
    for h in range(XA_HEADS):
        c = slice(h * dh, (h + 1) * dh)
        s = _dot_nt(q[:, c].astype(BF16), k_ref[0, :, c])
        e = jnp.exp(s - jnp.max(s, axis=-1, keepdims=True))
        p = e * (1.0 / jnp.sum(e, axis=-1, keepdims=True))
        outs.append(_dot(p.astype(BF16), v_ref[0, :, c]))
    o = jnp.concatenate(outs, axis=1)
    o_ref[...] = x + _dot(o.astype(BF16), wo_ref[...])


def _xattn(x2, g, wq, memkv, wo, B, S, tm):
    T, D = x2.shape
    ns = S // tm
    M = memkv.shape[1]
    return pl.pallas_call(
        _xattn_kernel,
        grid=(B, ns),
        in_specs=[pl.BlockSpec((tm, D), lambda b, i: (b * ns + i, 0)), _full((1, D)), _full((D, D)),
                  pl.BlockSpec((1, M, D), lambda b, i: (b, 0, 0)),
                  pl.BlockSpec((1, M, D), lambda b, i: (b, 0, 1)),
                  _full((D, D))],
        out_specs=pl.BlockSpec((tm, D), lambda b, i: (b * ns + i, 0)),
        out_shape=jax.ShapeDtypeStruct((T, D), F32),
        compiler_params=_params("parallel", "parallel"),
    )(x2, g, wq, memkv, memkv, wo)


PAIR_LIST = tuple((a, b) for a in range(PEER_TOPK) for b in range(PEER_TOPK) if (a + 1) * (b + 1) <= PEER_TOPK)
N_PAIR_ROWS = -(-len(PAIR_LIST) // 8) * 8
PAIR_COUNT = tuple(PEER_TOPK // (a + 1) for a in range(PEER_TOPK))
PAIR_START = tuple(sum(PAIR_COUNT[:a]) for a in range(PEER_TOPK))


def _top_rows_exact(v, n):
    R = v.shape[0]
    ridx = lax.broadcasted_iota(jnp.int32, v.shape, 0).astype(F32)
    rank = jnp.full(v.shape, float(n), F32)
    vals = []
    for r in range(n):
        m = jnp.max(v, axis=0, keepdims=True)
        first = jnp.min(jnp.where(v == m, ridx, float(R)), axis=0, keepdims=True)
        taken = ridx == first
        v = jnp.where(taken, LOWEST, v)
        rank = jnp.where(taken, float(r), rank)
        vals.append(m)
    return vals, rank


def _top_rows_distinct(v, n):
    rank = jnp.full(v.shape, float(n), F32)
    vals = []
    for r in range(n):
        m = jnp.max(v, axis=0, keepdims=True)
        taken = v == m
        v = jnp.where(taken, LOWEST, v)
        rank = jnp.where(taken, float(r), rank)
        vals.append(m)
    return vals, rank


def _peer_route_kernel(x_ref, g_ref, wqt_ref, sk_ref, hn_ref, c1_ref, g1_ref, r2_ref, g2_ref, s_sc, top_sc,
                       rank_sc, cand_sc, cw_sc):
    hn_ref[...] = _rms(x_ref[...], g_ref[...]).astype(BF16)
    nk, n = PEER_KEYS, PEER_TOPK
    tm = x_ref.shape[0]

    def scores_into(h):
        r0 = pl.multiple_of(h * 2 * nk, 2 * nk)
        qt = _dot_nt(wqt_ref[pl.ds(r0, 2 * nk), :], hn_ref[...]).astype(BF16)
        for half in range(2):
            s_sc[pl.ds(r0 + half * nk, nk), :] = _dot(sk_ref[half], qt[half * nk:(half + 1) * nk])

    def put(li, vals, rank):
        for r in range(n):
            top_sc[pl.ds(li * n + r, 1), :] = vals[r]
        rank_sc[pl.ds(pl.multiple_of(li * nk, nk), nk), :] = rank

    def one_head(h, carry):
        scores = [s_sc[pl.ds(pl.multiple_of((2 * h + half) * nk, nk), nk), :] for half in range(2)]
        scores_into(jnp.minimum(h + 1, PEER_HEADS - 1))
        ranked = jnp.zeros((1, tm), F32)
        for half in range(2):
            li = 2 * h + half
            vals, rank = _top_rows_distinct(scores[half], n)
            put(li, vals, rank)
            ranked = jnp.maximum(ranked, jnp.sum((rank < float(n)).astype(F32), axis=0, keepdims=True))

        @pl.when(jnp.max(ranked) > float(n))
        def _():
            for half in range(2):
                li = 2 * h + half
                put(li, *_top_rows_exact(s_sc[pl.ds(pl.multiple_of(li * nk, nk), nk), :], n))

        return carry

    scores_into(0)
    lax.fori_loop(0, PEER_HEADS, one_head, 0)

    cand_sc[...] = jnp.full(cand_sc.shape, LOWEST, F32)
    cw_sc[...] = jnp.zeros(cw_sc.shape, F32)
    for h in range(PEER_HEADS):
        l1, l2 = 2 * h, 2 * h + 1
        top1 = [top_sc[l1 * n + a:l1 * n + a + 1, :] for a in range(n)]
        top2 = [top_sc[l2 * n + a:l2 * n + a + 1, :] for a in range(n)]
        e1 = [jnp.exp(t - top1[0]) for t in top1]
        e2 = [jnp.exp(t - top2[0]) for t in top2]
        for r, (a, b) in enumerate(PAIR_LIST):
            cand_sc[h, r:r + 1, :] = top1[a] + top2[b]
            cw_sc[h, r:r + 1, :] = e1[a] * e2[b]
        _, crank = _top_rows_exact(cand_sc[h], n)
        chosen = (crank < float(n)).astype(F32)
        z = jnp.sum(chosen * cw_sc[h], axis=0, keepdims=True)
        rank1 = rank_sc[l1 * nk:(l1 + 1) * nk, :].astype(BF16)
        count1 = jnp.zeros(rank1.shape, BF16)
        for a in range(n):
            n_a = jnp.sum(chosen[PAIR_START[a]:PAIR_START[a] + PAIR_COUNT[a]], axis=0, keepdims=True)
            count1 = jnp.where(rank1 == float(a), n_a.astype(BF16), count1)
        rows = slice(h * nk, (h + 1) * nk)
        c1_ref[rows, :] = count1.astype(F32)
        g1_ref[rows, :] = jnp.exp(s_sc[l1 * nk:(l1 + 1) * nk, :] - top1[0]) * (1.0 / z)
        r2_ref[rows, :] = rank_sc[l2 * nk:(l2 + 1) * nk, :].astype(BF16)
        g2_ref[rows, :] = jnp.exp(s_sc[l2 * nk:(l2 + 1) * nk, :] - top2[0]).astype(BF16)


def _peer_route(x2, g, wqt, sk, tm):
    T, D = x2.shape
    nr = PEER_HEADS * PEER_KEYS
    col = pl.BlockSpec((nr, tm), lambda i: (0, i))
    return pl.pallas_call(
        _peer_route_kernel,
        grid=(T // tm,),
        in_specs=[pl.BlockSpec((tm, D), lambda i: (i, 0)), _full((1, D)), _full(wqt.shape), _full(sk.shape)],
        out_specs=[pl.BlockSpec((tm, D), lambda i: (i, 0)), col, col, col, col],
        out_shape=[jax.ShapeDtypeStruct((T, D), BF16), jax.ShapeDtypeStruct((nr, T), F32),
                   jax.ShapeDtypeStruct((nr, T), F32), jax.ShapeDtypeStruct((nr, T), BF16),
                   jax.ShapeDtypeStruct((nr, T), BF16)],
        scratch_shapes=[pltpu.VMEM((2 * nr, tm), F32),
                        pltpu.VMEM((2 * PEER_HEADS * PEER_TOPK, tm), F32),
                        pltpu.VMEM((2 * nr, tm), F32), pltpu.VMEM((PEER_HEADS, N_PAIR_ROWS, tm), F32),
                        pltpu.VMEM((PEER_HEADS, N_PAIR_ROWS, tm), F32)],
        compiler_params=_params("parallel"),
    )(x2, g, wqt, sk)


def _peer_expert_kernel(*refs, subs, final_norm):
    hn_ref, c1_ref, g1_ref, r2_ref, g2_ref, u_ref, vt_ref, x_ref = refs[:8]
    fg_ref = refs[8] if final_norm else None
    o_ref, acc_sc, wa_sc, act_sc = refs[-4:]
    c = pl.program_id(1)
    nk = PEER_KEYS
    te = sum(subs)
    offs = [sum(subs[:j]) for j in range(len(subs))]
    tm = hn_ref.shape[0]
    slots = act_sc.shape[0]

    @pl.when(c == 0)
    def _():
        acc_sc[...] = jnp.zeros(acc_sc.shape, F32)

    def pre_act(j):
        o, rows = offs[j], subs[j]
        act_sc[j % slots, 0:rows, :] = _dot_nt(u_ref[o:o + rows, :], hn_ref[...]).astype(BF16)

    for j in range(min(EXPERT_LOOKAHEAD, len(subs))):
        pre_act(j)
    for j, (o, rows) in enumerate(zip(offs, subs)):
        if j + EXPERT_LOOKAHEAD < len(subs):
            pre_act(j + EXPERT_LOOKAHEAD)
        for il in range(rows // nk):
            i = c * (te // nk) + o // nk + il
            w = jnp.zeros((nk, tm), BF16)
            for h in range(PEER_HEADS):
                count = c1_ref[pl.ds(h * nk + i, 1), :].astype(BF16)
                g1row = g1_ref[pl.ds(h * nk + i, 1), :].astype(BF16)
                hr = slice(h * nk, (h + 1) * nk)
                w = w + jnp.where(r2_ref[hr, :] < count, g2_ref[hr, :], 0.0) * g1row
            er = slice(il * nk, (il + 1) * nk)
            wa_sc[o + il * nk:o + (il + 1) * nk, :] = w * _gelu(act_sc[j % slots, er, :])
        acc_sc[...] += _dot(vt_ref[:, o:o + rows], wa_sc[o:o + rows, :])

    @pl.when(c == pl.num_programs(1) - 1)
    def _():
        out = x_ref[...] + acc_sc[...].T
        o_ref[...] = _rms(out, fg_ref[...]) if final_norm else out


def _peer_expert(hn, c1, g1, r2, g2, u, vt, layer, x2, final_g, tm, subs):
    T, D = x2.shape
    ne = u.shape[1]
    te = sum(subs)
    tok = pl.BlockSpec((c1.shape[0], tm), lambda i, c: (0, i))
    row = pl.BlockSpec((tm, D), lambda i, c: (i, 0))
    return pl.pallas_call(
        functools.partial(_peer_expert_kernel, subs=subs, final_norm=final_g is not None),
        grid=(T // tm, ne // te),
        in_specs=[row, tok, tok, tok, tok,
                  pl.BlockSpec((None, te, D), lambda i, c: (layer, c, 0)),
                  pl.BlockSpec((None, D, te), lambda i, c: (layer, 0, c)),
                  row] + ([] if final_g is None else [_full((1, D))]),
        out_specs=row,
        out_shape=jax.ShapeDtypeStruct((T, D), F32),
        scratch_shapes=[pltpu.VMEM((D, tm), F32), pltpu.VMEM((te, tm), BF16),
                        pltpu.VMEM((EXPERT_LOOKAHEAD + 1, max(subs), tm), BF16)],
        compiler_params=_params("parallel", "arbitrary"),
    )(hn, c1, g1, r2, g2, u, vt, x2, *([] if final_g is None else [final_g]))


def _rope_tables(positions):
    B, S = positions.shape
    freqs = ROPE_THETA ** (-jnp.arange(ROT_HALF, dtype=F32) * 2.0 / ROT_DIM)
    ang = positions.astype(F32)[:, :, None] * freqs
    cos, sin = jnp.cos(ang), jnp.sin(ang)
    ones = jnp.ones((B, S, HEAD_DIM - ROT_DIM), F32)
    zeros8 = jnp.zeros((B, S, ROT_HALF), F32)
    zeros = jnp.zeros((B, S, HEAD_DIM - ROT_DIM), F32)
    cos_h = jnp.concatenate([cos, cos, ones], axis=-1)
    sa_h = jnp.concatenate([zeros8, sin, zeros], axis=-1)
    sb_h = jnp.concatenate([-sin, zeros8, zeros], axis=-1)
    rep = LANES // HEAD_DIM
    tile = lambda t: jnp.tile(t, (1, 1, rep)).reshape(B * S, LANES)
    return tile(cos_h), tile(sa_h), tile(sb_h)


def _overlap_matrix(n_chunk, n_sel):
    ci = np.arange(n_chunk)[:, None] * CMP_STRIDE
    sj = np.arange(LANES)[None, :] * SEL_BLOCK
    ov = (ci < sj + SEL_BLOCK) & (ci + CMP_BLOCK > sj) & (np.arange(LANES)[None, :] < n_sel)
    ov = ov & (np.arange(n_chunk)[:, None] < n_chunk - 1)
    return jnp.asarray(ov, BF16)


def kernel(x, mem, positions, norm_mix_g, w_in, cmp_pe, cmp_w1, cmp_w2, conv_w, conv_b, conv_ln_g, conv_ln_b, conv_pw_w, conv_pw_b, pool_w, pool_scale, w_out, norm_xa_g, norm_mem_g, xa_wq, xa_wkv, xa_wo, norm_ffn_g, peer_wq, peer_subkeys, peer_u, peer_v, final_g):
    B, S, D = x.shape
    T = B * S
    depth = w_in.shape[0]
    M = mem.shape[1]
    dc = conv_w.shape[-1]
    n_chunk = S // CMP_STRIDE
    n_sel = S // SEL_BLOCK
    assert S % max(TM_PROJ, TS_MIX, KC_SEL, TM_PEER) == 0 and SEL_TOPN <= n_sel <= LANES and D % LANES == 0
    assert S >= WINDOW + TQ_NSA and peer_u.shape[1] % sum(EXPERT_SUBS) == 0

    cos_t, sa_t, sb_t = _rope_tables(positions)
    overlap = _overlap_matrix(n_chunk, n_sel)
    row = lambda v: v.reshape(1, -1)
    x2 = x.reshape(T, D)
    mem2 = mem.reshape(B * M, D)
    n_gate = 3 * NSA_HEADS

    w_pad = jnp.concatenate([w_in[:, :, :D_QKV + n_gate], jnp.zeros((depth, D, LANES - n_gate), F32),
                             w_in[:, :, D_QKV + n_gate:]], axis=2).astype(BF16)
    pg = dc // len(POOL_WINDOWS)
    plw = jnp.zeros((depth, dc, dc), F32)
    for gi in range(len(POOL_WINDOWS)):
        plw = plw.at[:, gi * pg:(gi + 1) * pg, gi * pg:(gi + 1) * pg].set(pool_w[:, gi])
    plw, conv_pw, wo = plw.astype(BF16), conv_pw_w.astype(BF16), w_out.astype(BF16)
    wkv, wq, wxo = xa_wkv.astype(BF16), xa_wq.astype(BF16), xa_wo.astype(BF16)
    pwq_t = jnp.swapaxes(peer_wq, 1, 2).astype(BF16)
    psk, pu = peer_subkeys.astype(BF16), peer_u.astype(BF16)
    pv_t = jnp.swapaxes(peer_v, 1, 2).astype(BF16)

    for l in range(depth):
        q, kvc, kv, misc = _inproj(x2, row(norm_mix_g[l]), w_pad[l], cos_t, sa_t, sb_t, tm=TM_PROJ)
        xc = kvc.reshape(B, n_chunk, CMP_STRIDE * 2 * LANES)
        cmp = _compress(xc, *_compress_weights(cmp_pe[l], cmp_w1[l], cmp_w2[l]))
        o_nsa = _nsa(q, cmp[0], cmp[1], kv.reshape(B, S, -1), misc, overlap, B, S, tq=TQ_NSA)
        x2 = _mixout(misc, o_nsa, x2, conv_w[l], row(conv_b[l]), row(conv_ln_g[l]), row(conv_ln_b[l]),
                     conv_pw[l], row(conv_pw_b[l]), plw[l], row(pool_scale[l]), wo[l], B, S, ts=TS_MIX)
        memkv = _norm_matmul(mem2, row(norm_mem_g[l]), wkv[l], tm=TM_MEM, out_dtype=BF16)
        x2 = _xattn(x2, row(norm_xa_g[l]), wq[l], memkv.reshape(B, M, 2 * D), wxo[l], B, S, tm=TM_PROJ)
        hn, c1, g1, r2, g2 = _peer_route(x2, row(norm_ffn_g[l]), pwq_t[l], psk[l], tm=TM_PEER)
        x2 = _peer_expert(hn, c1, g1, r2, g2, pu, pv_t, l, x2,
                          row(final_g) if l == depth - 1 else None, tm=TM_PEER, subs=EXPERT_SUBS)
    return x2.reshape(B, S, D)
```

```python
import functools
import math

import jax
import jax.numpy as jnp
import numpy as np
from jax import lax
from jax.experimental import pallas as pl
from jax.experimental.pallas import tpu as pltpu

F32 = jnp.float32
BF16 = jnp.bfloat16

NSA_HEADS = 8
NSA_KV_HEADS = 2
NSA_GROUP = NSA_HEADS // NSA_KV_HEADS
HEAD_DIM = 64
D_NSA = NSA_HEADS * HEAD_DIM
D_KV = NSA_KV_HEADS * HEAD_DIM
ROT_DIM = HEAD_DIM // 4
ROT_HALF = ROT_DIM // 2
ROPE_THETA = 500000.0
CMP_BLOCK = 32
CMP_STRIDE = 16
SEL_BLOCK = 64
SEL_TOPN = 16
WINDOW = 512
CONV_WIDTH = 31
POOL_WINDOWS = (2, 4, 8, 16)
XA_HEADS = 4
PEER_HEADS = 8
PEER_KEYS = 128
PEER_TOPK = 16
EPS = 1e-6
NEG = -1e30
FORCE = 1e4
LOWEST = -3.0e38

LANES = 128
SUBLANES = 8
VMEM_LIMIT = 56 * 1024 * 1024

NT_DIMS = (((1,), (1,)), ((), ()))

TM_PROJ = 512
TM_MEM = 256
TQ_NSA = 256
KC_SEL = 512
TS_MIX = 512
TM_PEER = 512
EXPERT_SUBS = (256, 768, 768, 256)
EXPERT_LOOKAHEAD = 3


def _params(*sem):
    return pltpu.CompilerParams(dimension_semantics=sem, vmem_limit_bytes=VMEM_LIMIT)


def _full(shape):
    nd = len(shape)
    return pl.BlockSpec(shape, lambda *_: (0,) * nd)


def _rms(x, g):
    return x * lax.rsqrt(jnp.mean(x * x, axis=-1, keepdims=True) + EPS) * g


def _gelu(x):
    c = math.sqrt(2.0 / math.pi)
    return 0.5 * x * (1.0 + jnp.tanh(c * (x + 0.044715 * (x * x * x))))


def _dot(a, b):
    return jnp.dot(a, b, preferred_element_type=F32)


def _dot_nt(a, b):
    return lax.dot_general(a, b, NT_DIMS, preferred_element_type=F32)


N_ROPE_Q = D_NSA // LANES
N_KV_CHUNKS = 6
D_QKV = D_NSA + N_KV_CHUNKS * D_KV
N_KV_OUT = 2 + 2 * NSA_KV_HEADS


def _inproj_kernel(x_ref, g_ref, w_ref, cos_ref, sa_ref, sb_ref, q_ref, kvc_ref, kv_ref, misc_ref):
    y = _rms(x_ref[...], g_ref[...])
    z = _dot(y.astype(BF16), w_ref[...])
    cos, sa, sb = cos_ref[...], sa_ref[...], sb_ref[...]

    def rope(c):
        return c * cos + pltpu.roll(c, ROT_HALF, 1) * sa + pltpu.roll(c, LANES - ROT_HALF, 1) * sb

    scale = HEAD_DIM ** -0.5
    for j in range(N_ROPE_Q):
        q_ref[:, j * LANES:(j + 1) * LANES] = (rope(z[:, j * LANES:(j + 1) * LANES]) * scale).astype(BF16)
    lane = lax.broadcasted_iota(jnp.int32, (z.shape[0], LANES), 1)
    out = 0
    for j in range(N_KV_CHUNKS):
        c = z[:, D_NSA + j * LANES:D_NSA + (j + 1) * LANES]
        if j % 2 == 0:
            c = rope(c)
        if j < 2:
            kvc_ref[:, j * LANES:(j + 1) * LANES] = c.astype(BF16)
        elif j % 2 == 0:
            kv_ref[:, out * LANES:(out + 1) * LANES] = c.astype(BF16)
            out += 1
    for j in (3, 5):
        c = z[:, D_NSA + j * LANES:D_NSA + (j + 1) * LANES]
        for k in range(NSA_KV_HEADS):
            own = (lane >= HEAD_DIM) if k == 1 else (lane < HEAD_DIM)
            kv_ref[:, out * LANES:(out + 1) * LANES] = jnp.where(own, c, 1.0).astype(BF16)
            out += 1
    misc_ref[...] = z[:, D_QKV:]


def _inproj(x2, g, w_pad, cos_t, sa_t, sb_t, tm):
    T, D = x2.shape
    n = w_pad.shape[1]
    n_misc = n - D_QKV
    row = lambda w: pl.BlockSpec((tm, w), lambda i: (i, 0))
    return pl.pallas_call(
        _inproj_kernel,
        grid=(T // tm,),
        in_specs=[row(D), _full((1, D)), _full((D, n)), row(LANES), row(LANES), row(LANES)],
        out_specs=[row(D_NSA), row(2 * LANES), row(N_KV_OUT * LANES), row(n_misc)],
        out_shape=[jax.ShapeDtypeStruct((T, D_NSA), BF16),
                   jax.ShapeDtypeStruct((T, 2 * LANES), BF16),
                   jax.ShapeDtypeStruct((T, N_KV_OUT * LANES), BF16),
                   jax.ShapeDtypeStruct((T, n_misc), F32)],
        compiler_params=_params("parallel"),
    )(x2, g, w_pad, cos_t, sa_t, sb_t)


def _compress_kernel(x_ref, pe_ref, w1_ref, w2_ref, o_ref):
    x = x_ref[0].astype(F32)
    n = x.shape[0]
    out = jnp.zeros((n, LANES), F32)
    for hd in range(NSA_KV_HEADS):
        a = _dot((x + pe_ref[0, hd, 0:1]).astype(BF16), w1_ref[0, hd, 0])
        b = _dot((x + pe_ref[0, hd, 1:2]).astype(BF16), w1_ref[0, hd, 1])
        hid = _gelu(a + pltpu.roll(b, n - 1, 0))
        out = out + _dot(hid.astype(BF16), w2_ref[0, hd])
    o_ref[0, 0] = out.astype(BF16)


def _compress(xc, pe, w1, w2):
    nb, nch, width = xc.shape
    hid = w1.shape[-1]
    return pl.pallas_call(
        _compress_kernel,
        grid=(2, nb),
        in_specs=[pl.BlockSpec((1, nch, width), lambda w, i: (i, 0, 0)),
                  pl.BlockSpec((1, NSA_KV_HEADS, 2, width), lambda w, i: (w, 0, 0, 0)),
                  pl.BlockSpec((1, NSA_KV_HEADS, 2, width, hid), lambda w, i: (w, 0, 0, 0, 0)),
                  pl.BlockSpec((1, NSA_KV_HEADS, hid, LANES), lambda w, i: (w, 0, 0, 0))],
        out_specs=pl.BlockSpec((1, 1, nch, LANES), lambda w, i: (w, i, 0, 0)),
        out_shape=jax.ShapeDtypeStruct((2, nb, nch, LANES), BF16),
        compiler_params=_params("parallel", "parallel"),
    )(xc, pe, w1, w2)


def _compress_weights(pe, w1, w2):
    hid = w1.shape[-1]
    eye_t = jnp.eye(2, dtype=F32)[:, None, None, None, :, None, None]
    eye_h = jnp.eye(NSA_KV_HEADS, dtype=F32)[None, :, None, None, None, :, None]
    per = (2, 1, 2, CMP_STRIDE, 1, 1, HEAD_DIM)
    width = CMP_STRIDE * 2 * NSA_KV_HEADS * HEAD_DIM
    pe_big = (pe.reshape(per) * eye_t * eye_h).reshape(2, NSA_KV_HEADS, 2, width)
    w1_big = (w1.reshape(per + (hid,)) * eye_t[..., None] * eye_h[..., None]).reshape(2, NSA_KV_HEADS, 2, width, hid)
    eye_o = jnp.eye(NSA_KV_HEADS, dtype=F32)[None, :, None, :, None]
    w2_big = (w2[:, None, :, None, :] * eye_o).reshape(2, NSA_KV_HEADS, hid, NSA_KV_HEADS * HEAD_DIM)
    return pe_big, w1_big.astype(BF16), w2_big.astype(BF16)


def _nsa_kernel(q_ref, kcmp_ref, vcmp_ref, ks_ref, kw_ref, vs0_ref, vs1_ref, vw0_ref, vw1_ref, gate_ref, ov_ref,
                o_ref, s_sc, mrun_sc, acc_sc, *, tq, kc_sel, kc_win):
    t0 = pl.program_id(1) * tq
    rows = NSA_GROUP * tq
    lane = lax.broadcasted_iota(jnp.int32, (tq, LANES), 1)
    lane_r = lax.broadcasted_iota(jnp.int32, (rows, LANES), 1)
    tpos = t0 + lax.broadcasted_iota(jnp.int32, (tq, 1), 0)
    gates = jax.nn.sigmoid(gate_ref[...])
    n_cmp = kcmp_ref.shape[1]
    n_sel = ks_ref.shape[1] // SEL_BLOCK
    vs_refs, vw_refs = (vs0_ref, vs1_ref), (vw0_ref, vw1_ref)

    def normalise(acc, own):
        return acc / jnp.where(own, pltpu.roll(acc, HEAD_DIM, 1), 1.0)

    def own_lanes(lanes, k):
        return (lanes >= HEAD_DIM) if k == 1 else (lanes < HEAD_DIM)

    def queries(k):
        parts = []
        for g in range(NSA_GROUP):
            hh = k * NSA_GROUP + g
            c = q_ref[:, (hh // 2) * LANES:(hh // 2 + 1) * LANES].astype(F32)
            if hh % 2 != k:
                c = pltpu.roll(c, HEAD_DIM, 1)
            parts.append(jnp.where(own_lanes(lane, k), c, 0.0))
        return jnp.concatenate(parts, axis=0).astype(BF16)

    def compressed(qs):
        s = _dot_nt(qs, kcmp_ref[0]).reshape(NSA_GROUP, tq, n_cmp)
        cmp_end = lax.broadcasted_iota(jnp.int32, (tq, n_cmp), 1) * CMP_STRIDE + (CMP_BLOCK - 1)
        valid = cmp_end <= tpos
        s = s + jnp.where(valid, 0.0, NEG)[None]
        e = jnp.exp(s - jnp.max(s, axis=-1, keepdims=True)) * valid.astype(F32)[None]
        l = jnp.sum(e, axis=-1, keepdims=True)
        p3 = e * (1.0 / jnp.where(l > 0.0, l, 1.0))
        o_cmp = _dot(p3.reshape(rows, n_cmp).astype(BF16), vcmp_ref[0])

        psum = p3[0] + p3[1] + p3[2] + p3[3]
        ov = ov_ref[...]
        p_hi = psum.astype(BF16)
        r1 = psum - p_hi.astype(F32)
        p_mid = r1.astype(BF16)
        p_lo = (r1 - p_mid.astype(F32)).astype(BF16)
        imp = _dot(p_hi, ov) + _dot(p_mid, ov) + _dot(p_lo, ov)
        imp = imp.T[:n_sel]
        blk = lax.broadcasted_iota(jnp.int32, (n_sel, tq), 0)
        tpos_t = t0 + lax.broadcasted_iota(jnp.int32, (1, tq), 1)
        cur = tpos_t // SEL_BLOCK
        forced = (blk == 0) | (blk == cur) | (blk == cur - 1)
        imp = jnp.where(blk * SEL_BLOCK <= tpos_t, jnp.where(forced, FORCE, imp), NEG)
        blk_f = blk.astype(F32)
        for _ in range(SEL_TOPN):
            m = jnp.max(imp, axis=0, keepdims=True)
            first = jnp.min(jnp.where(imp == m, blk_f, float(n_sel)), axis=0, keepdims=True)
            imp = jnp.where(blk_f == first, LOWEST, imp)
        sel_t = jnp.concatenate([(imp == LOWEST).astype(F32), jnp.zeros((LANES - n_sel, tq), F32)], axis=0)
        return o_cmp, sel_t.T.astype(BF16)

    def window(k, qs):
        w0 = pl.multiple_of(jnp.maximum(t0 + tq - kc_win, 0), tq)
        sw = _dot_nt(qs, kw_ref[0, pl.ds(w0, kc_win), :]).reshape(NSA_GROUP, tq, kc_win)
        wpos = w0 + lax.broadcasted_iota(jnp.int32, (tq, kc_win), 1)
        in_win = (wpos <= tpos) & (wpos > tpos - WINDOW)
        sw = (sw + jnp.where(in_win, 0.0, NEG)[None]).reshape(rows, kc_win)
        pw = jnp.exp(sw - jnp.max(sw, axis=1, keepdims=True))
        return normalise(_dot(pw.astype(BF16), vw_refs[k][0, pl.ds(w0, kc_win), :]), own_lanes(lane_r, k))

    heads = range(NSA_KV_HEADS)
    qs = [queries(k) for k in heads]
    o_cmp, sel = zip(*[compressed(qs[k]) for k in heads])
    o_win = [window(k, qs[k]) for k in heads]

    n_chunks = (t0 + tq + kc_sel - 1) // kc_sel
    mrun_sc[...] = jnp.full(mrun_sc.shape, NEG, F32)

    def scores(c, carry):
        k0 = pl.multiple_of(c * kc_sel, kc_sel)
        jrow = lax.broadcasted_iota(jnp.int32, (LANES, kc_sel), 0)
        kcol = lax.broadcasted_iota(jnp.int32, (LANES, kc_sel), 1)
        expand = (jrow == k0 // SEL_BLOCK + kcol // SEL_BLOCK).astype(BF16)
        causal = k0 + lax.broadcasted_iota(jnp.int32, (tq, kc_sel), 1) <= tpos
        kb = ks_ref[0, pl.ds(k0, kc_sel), :]
        for k in heads:
            seen = (_dot(sel[k], expand) > 0.5) & causal
            s = _dot_nt(qs[k], kb).reshape(NSA_GROUP, tq, kc_sel)
            s = (s + jnp.where(seen, 0.0, NEG)[None]).reshape(rows, kc_sel)
            s_sc[k, c] = s
            m = mrun_sc[k]
            for j in range(kc_sel // LANES):
                m = jnp.maximum(m, s[:, j * LANES:(j + 1) * LANES])
            mrun_sc[k] = m
        return carry

    lax.fori_loop(0, n_chunks, scores, 0)
    m_sel = [jnp.broadcast_to(jnp.max(mrun_sc[k], axis=1, keepdims=True), (rows, LANES)) for k in heads]
    acc_sc[...] = jnp.zeros(acc_sc.shape, F32)

    def weigh(c, carry):
        k0 = pl.multiple_of(c * kc_sel, kc_sel)
        for k in heads:
            s = s_sc[k, c]
            p = jnp.concatenate([jnp.exp(s[:, j * LANES:(j + 1) * LANES] - m_sel[k])
                                 for j in range(kc_sel // LANES)], axis=1)
            acc_sc[k] += _dot(p.astype(BF16), vs_refs[k][0, pl.ds(k0, kc_sel), :])
        return carry

    lax.fori_loop(0, n_chunks, weigh, 0)

    for k in heads:
        o_sel = normalise(acc_sc[k], own_lanes(lane_r, k))
        outs = []
        for g in range(NSA_GROUP):
            r = slice(g * tq, (g + 1) * tq)
            gi = (k * NSA_GROUP + g) * 3
            og = (gates[:, gi:gi + 1] * o_cmp[k][r] + gates[:, gi + 1:gi + 2] * o_sel[r]
                  + gates[:, gi + 2:gi + 3] * o_win[k][r])
            if g % 2 != k:
                og = pltpu.roll(og, HEAD_DIM, 1)
            outs.append(og)
        for j in range(NSA_GROUP // 2):
            chunk = jnp.where(lane < HEAD_DIM, outs[2 * j], outs[2 * j + 1])
            cj = k * (NSA_GROUP // 2) + j
            o_ref[:, cj * LANES:(cj + 1) * LANES] = chunk.astype(BF16)


def _nsa(q, kcmp, vcmp, kv, gates_misc, overlap, B, S, tq):
    kc_sel, kc_win = KC_SEL, WINDOW + tq
    nq = S // tq
    n_cmp = kcmp.shape[1]
    rows = NSA_GROUP * tq
    kvspec = lambda j: pl.BlockSpec((1, S, LANES), lambda b, i: (b, 0, j), pipeline_mode=pl.Buffered(1))
    return pl.pallas_call(
        functools.partial(_nsa_kernel, tq=tq, kc_sel=kc_sel, kc_win=kc_win),
        grid=(B, nq),
        in_specs=[pl.BlockSpec((tq, D_NSA), lambda b, i: (b * nq + i, 0)),
                  pl.BlockSpec((1, n_cmp, LANES), lambda b, i: (b, 0, 0)),
                  pl.BlockSpec((1, n_cmp, LANES), lambda b, i: (b, 0, 0)),
                  kvspec(0), kvspec(1), kvspec(2), kvspec(3), kvspec(4), kvspec(5),
                  pl.BlockSpec((tq, LANES), lambda b, i: (b * nq + i, 0)),
                  _full(overlap.shape)],
        out_specs=pl.BlockSpec((tq, D_NSA), lambda b, i: (b * nq + i, 0)),
        out_shape=jax.ShapeDtypeStruct((B * S, D_NSA), BF16),
        scratch_shapes=[pltpu.VMEM((NSA_KV_HEADS, S // kc_sel, rows, kc_sel), F32),
                        pltpu.VMEM((NSA_KV_HEADS, rows, LANES), F32), pltpu.VMEM((NSA_KV_HEADS, rows, LANES), F32)],
        compiler_params=_params("parallel", "parallel"),
    )(q, kcmp, vcmp, kv, kv, kv, kv, kv, kv, gates_misc, overlap)


HALO = 32


def _mixout_kernel(mc_ref, mp_ref, on_ref, x_ref, cw_ref, cb_ref, lg_ref, lb_ref, pw_ref, pb_ref, plw_ref,
                   pls_ref, wo_ref, o_ref, hbuf, pbuf, hsh, *, ts, dc):
    i = pl.program_id(1)
    first = i == 0
    off_a, off_b, off_p = LANES, LANES + dc, LANES + 2 * dc

    def glu(ref, r):
        return ref[r, off_a:off_a + dc] * jax.nn.sigmoid(ref[r, off_b:off_b + dc])

    tail = slice(ts - HALO, ts)
    hbuf[0:HALO, :] = jnp.where(first, 0.0, glu(mp_ref, tail))
    hbuf[HALO:, :] = glu(mc_ref, slice(None))
    pbuf[0:HALO, :] = jnp.where(first, 0.0, mp_ref[tail, off_p:off_p + dc])
    pcur = mc_ref[:, off_p:off_p + dc]
    pbuf[HALO:, :] = pcur

    shifted = ts + HALO - SUBLANES
    for r in range(1, SUBLANES):
        hsh[r, 0:shifted, :] = hbuf[pl.ds(r, shifted), :]
    acc = jnp.zeros((ts, dc), F32) + cb_ref[...]
    for w in range(CONV_WIDTH):
        q, r = divmod(HALO - (CONV_WIDTH - 1) + w, SUBLANES)
        tap = hbuf[pl.ds(q * SUBLANES, ts), :] if r == 0 else hsh[r, pl.ds(q * SUBLANES, ts), :]
        acc = acc + tap * cw_ref[w:w + 1, :]
    mu = jnp.mean(acc, axis=-1, keepdims=True)
    xc = acc - mu
    y = xc * lax.rsqrt(jnp.mean(xc * xc, axis=-1, keepdims=True) + EPS) * lg_ref[...] + lb_ref[...]
    y = y * jax.nn.sigmoid(y)
    o_conv = _dot(y.astype(BF16), pw_ref[...]) + pb_ref[...]

    tglob = i * ts + lax.broadcasted_iota(jnp.int32, (ts, dc), 0)
    lane = lax.broadcasted_iota(jnp.int32, (ts, dc), 1)
    pg = dc // len(POOL_WINDOWS)
    run = pcur
    d = 1
    mean = jnp.zeros((ts, dc), F32)
    for gi, w in enumerate(POOL_WINDOWS):
        while d < w:
            run = run + pbuf[pl.ds(HALO - d, ts), :]
            d += 1
        cnt = jnp.minimum(tglob + 1, w).astype(F32)
        mean = jnp.where(lane // pg == gi, run / cnt, mean)
    o_pool = _dot((mean - pcur).astype(BF16), plw_ref[...]) * pls_ref[...]

    dn = on_ref.shape[1]
    o_ref[...] = (x_ref[...] + _dot(on_ref[...], wo_ref[0:dn, :])
                  + _dot(o_conv.astype(BF16), wo_ref[dn:dn + dc, :])
                  + _dot(o_pool.astype(BF16), wo_ref[dn + dc:, :]))


def _mixout(misc, o_nsa, x2, cw, cb, lg, lb, pw, pb, plw, pls, wo, B, S, ts):
    T, D = x2.shape
    dc = cw.shape[1]
    ns = S // ts
    nm = misc.shape[1]
    cur = lambda w: pl.BlockSpec((ts, w), lambda b, i: (b * ns + i, 0))
    prev = pl.BlockSpec((ts, nm), lambda b, i: (b * ns + jnp.maximum(i - 1, 0), 0))
    return pl.pallas_call(
        functools.partial(_mixout_kernel, ts=ts, dc=dc),
        grid=(B, ns),
        in_specs=[cur(nm), prev, cur(o_nsa.shape[1]), cur(D), _full(cw.shape), _full(cb.shape), _full(lg.shape),
                  _full(lb.shape), _full(pw.shape), _full(pb.shape), _full(plw.shape), _full(pls.shape),
                  _full(wo.shape)],
        out_specs=cur(D),
        out_shape=jax.ShapeDtypeStruct((T, D), F32),
        scratch_shapes=[pltpu.VMEM((ts + HALO, dc), F32), pltpu.VMEM((ts + HALO, dc), F32),
                        pltpu.VMEM((SUBLANES, ts + HALO, dc), F32)],
        compiler_params=_params("parallel", "parallel"),
    )(misc, misc, o_nsa, x2, cw, cb, lg, lb, pw, pb, plw, pls, wo)


def _norm_matmul_kernel(x_ref, g_ref, w_ref, o_ref):
    o_ref[...] = _dot(_rms(x_ref[...], g_ref[...]).astype(BF16), w_ref[...]).astype(o_ref.dtype)


def _norm_matmul(x2, g, w, tm, out_dtype):
    T, D = x2.shape
    n = w.shape[1]
    return pl.pallas_call(
        _norm_matmul_kernel,
        grid=(T // tm,),
        in_specs=[pl.BlockSpec((tm, D), lambda i: (i, 0)), _full((1, D)), _full((D, n))],
        out_specs=pl.BlockSpec((tm, n), lambda i: (i, 0)),
        out_shape=jax.ShapeDtypeStruct((T, n), out_dtype),
        compiler_params=_params("parallel"),
    )(x2, g, w)


def _xattn_kernel(x_ref, g_ref, wq_ref, k_ref, v_ref, wo_ref, o_ref):
    x = x_ref[...]
    D = x.shape[1]
    dh = D // XA_HEADS
    q = _dot(_rms(x, g_ref[...]).astype(BF16), wq_ref[...]) * (dh ** -0.5)
    outs = []
    for h in range(XA_HEADS):
        c = slice(h * dh, (h + 1) * dh)
        s = _dot_nt(q[:, c].astype(BF16), k_ref[0, :, c])
        e = jnp.exp(s - jnp.max(s, axis=-1, keepdims=True))
        p = e * (1.0 / jnp.sum(e, axis=-1, keepdims=True))
        outs.append(_dot(p.astype(BF16), v_ref[0, :, c]))
    o = jnp.concatenate(outs, axis=1)
    o_ref[...] = x + _dot(o.astype(BF16), wo_ref[...])


def _xattn(x2, g, wq, memkv, wo, B, S, tm):
    T, D = x2.shape
    ns = S // tm
    M = memkv.shape[1]
    return pl.pallas_call(
        _xattn_kernel,
        grid=(B, ns),
        in_specs=[pl.BlockSpec((tm, D), lambda b, i: (b * ns + i, 0)), _full((1, D)), _full((D, D)),
                  pl.BlockSpec((1, M, D), lambda b, i: (b, 0, 0)),
                  pl.BlockSpec((1, M, D), lambda b, i: (b, 0, 1)),
                  _full((D, D))],
        out_specs=pl.BlockSpec((tm, D), lambda b, i: (b * ns + i, 0)),
        out_shape=jax.ShapeDtypeStruct((T, D), F32),
        compiler_params=_params("parallel", "parallel"),
    )(x2, g, wq, memkv, memkv, wo)


PAIR_LIST = tuple((a, b) for a in range(PEER_TOPK) for b in range(PEER_TOPK) if (a + 1) * (b + 1) <= PEER_TOPK)
N_PAIR_ROWS = -(-len(PAIR_LIST) // SUBLANES) * SUBLANES
PAIR_COUNT = tuple(PEER_TOPK // (a + 1) for a in range(PEER_TOPK))
PAIR_START = tuple(sum(PAIR_COUNT[:a]) for a in range(PEER_TOPK))


def _top_rows_exact(v, n):
    R = v.shape[0]
    ridx = lax.broadcasted_iota(jnp.int32, v.shape, 0).astype(F32)
    rank = jnp.full(v.shape, float(n), F32)
    vals = []
    for r in range(n):
        m = jnp.max(v, axis=0, keepdims=True)
        first = jnp.min(jnp.where(v == m, ridx, float(R)), axis=0, keepdims=True)
        taken = ridx == first
        v = jnp.where(taken, LOWEST, v)
        rank = jnp.where(taken, float(r), rank)
        vals.append(m)
    return vals, rank


def _top_rows_distinct(v, n):
    rank = jnp.full(v.shape, float(n), F32)
    vals = []
    for r in range(n):
        m = jnp.max(v, axis=0, keepdims=True)
        taken = v == m
        v = jnp.where(taken, LOWEST, v)
        rank = jnp.where(taken, float(r), rank)
        vals.append(m)
    return vals, rank


def _peer_route_kernel(x_ref, g_ref, wqt_ref, sk_ref, hn_ref, c1_ref, g1_ref, r2_ref, g2_ref, s_sc, top_sc,
                       rank_sc, cand_sc, cw_sc):
    hn_ref[...] = _rms(x_ref[...], g_ref[...]).astype(BF16)
    nk, n = PEER_KEYS, PEER_TOPK
    tm = x_ref.shape[0]

    def scores_into(h):
        r0 = pl.multiple_of(h * 2 * nk, 2 * nk)
        qt = _dot_nt(wqt_ref[pl.ds(r0, 2 * nk), :], hn_ref[...]).astype(BF16)
        for half in range(2):
            s_sc[pl.ds(r0 + half * nk, nk), :] = _dot(sk_ref[half], qt[half * nk:(half + 1) * nk])

    def put(li, vals, rank):
        for r in range(n):
            top_sc[pl.ds(li * n + r, 1), :] = vals[r]
        rank_sc[pl.ds(pl.multiple_of(li * nk, nk), nk), :] = rank

    def one_head(h, carry):
        scores = [s_sc[pl.ds(pl.multiple_of((2 * h + half) * nk, nk), nk), :] for half in range(2)]
        scores_into(jnp.minimum(h + 1, PEER_HEADS - 1))
        ranked = jnp.zeros((1, tm), F32)
        for half in range(2):
            li = 2 * h + half
            vals, rank = _top_rows_distinct(scores[half], n)
            put(li, vals, rank)
            ranked = jnp.maximum(ranked, jnp.sum((rank < float(n)).astype(F32), axis=0, keepdims=True))

        @pl.when(jnp.max(ranked) > float(n))
        def _():
            for half in range(2):
                li = 2 * h + half
                put(li, *_top_rows_exact(s_sc[pl.ds(pl.multiple_of(li * nk, nk), nk), :], n))

        return carry

    scores_into(0)
    lax.fori_loop(0, PEER_HEADS, one_head, 0)

    cand_sc[...] = jnp.full(cand_sc.shape, LOWEST, F32)
    cw_sc[...] = jnp.zeros(cw_sc.shape, F32)
    for h in range(PEER_HEADS):
        l1, l2 = 2 * h, 2 * h + 1
        top1 = [top_sc[l1 * n + a:l1 * n + a + 1, :] for a in range(n)]
        top2 = [top_sc[l2 * n + a:l2 * n + a + 1, :] for a in range(n)]
        e1 = [jnp.exp(t - top1[0]) for t in top1]
        e2 = [jnp.exp(t - top2[0]) for t in top2]
        for r, (a, b) in enumerate(PAIR_LIST):
            cand_sc[h, r:r + 1, :] = top1[a] + top2[b]
            cw_sc[h, r:r + 1, :] = e1[a] * e2[b]
        _, crank = _top_rows_exact(cand_sc[h], n)
        chosen = (crank < float(n)).astype(F32)
        z = jnp.sum(chosen * cw_sc[h], axis=0, keepdims=True)
        rank1 = rank_sc[l1 * nk:(l1 + 1) * nk, :].astype(BF16)
        count1 = jnp.zeros(rank1.shape, BF16)
        for a in range(n):
            n_a = jnp.sum(chosen[PAIR_START[a]:PAIR_START[a] + PAIR_COUNT[a]], axis=0, keepdims=True)
            count1 = jnp.where(rank1 == float(a), n_a.astype(BF16), count1)
        rows = slice(h * nk, (h + 1) * nk)
        c1_ref[rows, :] = count1.astype(F32)
        g1_ref[rows, :] = jnp.exp(s_sc[l1 * nk:(l1 + 1) * nk, :] - top1[0]) * (1.0 / z)
        r2_ref[rows, :] = rank_sc[l2 * nk:(l2 + 1) * nk, :].astype(BF16)
        g2_ref[rows, :] = jnp.exp(s_sc[l2 * nk:(l2 + 1) * nk, :] - top2[0]).astype(BF16)


def _peer_route(x2, g, wqt, sk, tm):
    T, D = x2.shape
    nr = PEER_HEADS * PEER_KEYS
    col = pl.BlockSpec((nr, tm), lambda i: (0, i))
    return pl.pallas_call(
        _peer_route_kernel,
        grid=(T // tm,),
        in_specs=[pl.BlockSpec((tm, D), lambda i: (i, 0)), _full((1, D)), _full(wqt.shape), _full(sk.shape)],
        out_specs=[pl.BlockSpec((tm, D), lambda i: (i, 0)), col, col, col, col],
        out_shape=[jax.ShapeDtypeStruct((T, D), BF16), jax.ShapeDtypeStruct((nr, T), F32),
                   jax.ShapeDtypeStruct((nr, T), F32), jax.ShapeDtypeStruct((nr, T), BF16),
                   jax.ShapeDtypeStruct((nr, T), BF16)],
        scratch_shapes=[pltpu.VMEM((2 * nr, tm), F32),
                        pltpu.VMEM((2 * PEER_HEADS * PEER_TOPK, tm), F32),
                        pltpu.VMEM((2 * nr, tm), F32), pltpu.VMEM((PEER_HEADS, N_PAIR_ROWS, tm), F32),
                        pltpu.VMEM((PEER_HEADS, N_PAIR_ROWS, tm), F32)],
        compiler_params=_params("parallel"),
    )(x2, g, wqt, sk)


def _peer_expert_kernel(*refs, subs, final_norm):
    hn_ref, c1_ref, g1_ref, r2_ref, g2_ref, u_ref, vt_ref, x_ref = refs[:8]
    fg_ref = refs[8] if final_norm else None
    o_ref, acc_sc, wa_sc, act_sc = refs[-4:]
    c = pl.program_id(1)
    nk = PEER_KEYS
    te = sum(subs)
    offs = [sum(subs[:j]) for j in range(len(subs))]
    tm = hn_ref.shape[0]
    slots = act_sc.shape[0]

    @pl.when(c == 0)
    def _():
        acc_sc[...] = jnp.zeros(acc_sc.shape, F32)

    def pre_act(j):
        o, rows = offs[j], subs[j]
        act_sc[j % slots, 0:rows, :] = _dot_nt(u_ref[o:o + rows, :], hn_ref[...]).astype(BF16)

    for j in range(min(EXPERT_LOOKAHEAD, len(subs))):
        pre_act(j)
    for j, (o, rows) in enumerate(zip(offs, subs)):
        if j + EXPERT_LOOKAHEAD < len(subs):
            pre_act(j + EXPERT_LOOKAHEAD)
        for il in range(rows // nk):
            i = c * (te // nk) + o // nk + il
            w = jnp.zeros((nk, tm), BF16)
            for h in range(PEER_HEADS):
                count = c1_ref[pl.ds(h * nk + i, 1), :].astype(BF16)
                g1row = g1_ref[pl.ds(h * nk + i, 1), :].astype(BF16)
                hr = slice(h * nk, (h + 1) * nk)
                w = w + jnp.where(r2_ref[hr, :] < count, g2_ref[hr, :], 0.0) * g1row
            er = slice(il * nk, (il + 1) * nk)
            wa_sc[o + il * nk:o + (il + 1) * nk, :] = w * _gelu(act_sc[j % slots, er, :])
        acc_sc[...] += _dot(vt_ref[:, o:o + rows], wa_sc[o:o + rows, :])

    @pl.when(c == pl.num_programs(1) - 1)
    def _():
        out = x_ref[...] + acc_sc[...].T
        o_ref[...] = _rms(out, fg_ref[...]) if final_norm else out


def _peer_expert(hn, c1, g1, r2, g2, u, vt, layer, x2, final_g, tm, subs):
    T, D = x2.shape
    ne = u.shape[1]
    te = sum(subs)
    tok = pl.BlockSpec((c1.shape[0], tm), lambda i, c: (0, i))
    row = pl.BlockSpec((tm, D), lambda i, c: (i, 0))
    return pl.pallas_call(
        functools.partial(_peer_expert_kernel, subs=subs, final_norm=final_g is not None),
        grid=(T // tm, ne // te),
        in_specs=[row, tok, tok, tok, tok,
                  pl.BlockSpec((None, te, D), lambda i, c: (layer, c, 0)),
                  pl.BlockSpec((None, D, te), lambda i, c: (layer, 0, c)),
                  row] + ([] if final_g is None else [_full((1, D))]),
        out_specs=row,
        out_shape=jax.ShapeDtypeStruct((T, D), F32),
        scratch_shapes=[pltpu.VMEM((D, tm), F32), pltpu.VMEM((te, tm), BF16),
                        pltpu.VMEM((EXPERT_LOOKAHEAD + 1, max(subs), tm), BF16)],
        compiler_params=_params("parallel", "arbitrary"),
    )(hn, c1, g1, r2, g2, u, vt, x2, *([] if final_g is None else [final_g]))


def _rope_tables(positions):
    B, S = positions.shape
    freqs = ROPE_THETA ** (-jnp.arange(ROT_HALF, dtype=F32) * 2.0 / ROT_DIM)
    ang = positions.astype(F32)[:, :, None] * freqs
    cos, sin = jnp.cos(ang), jnp.sin(ang)
    ones = jnp.ones((B, S, HEAD_DIM - ROT_DIM), F32)
    zeros8 = jnp.zeros((B, S, ROT_HALF), F32)
    zeros = jnp.zeros((B, S, HEAD_DIM - ROT_DIM), F32)
    cos_h = jnp.concatenate([cos, cos, ones], axis=-1)
    sa_h = jnp.concatenate([zeros8, sin, zeros], axis=-1)
    sb_h = jnp.concatenate([-sin, zeros8, zeros], axis=-1)
    rep = LANES // HEAD_DIM
    tile = lambda t: jnp.tile(t, (1, 1, rep)).reshape(B * S, LANES)
    return tile(cos_h), tile(sa_h), tile(sb_h)


def _overlap_matrix(n_chunk, n_sel):
    ci = np.arange(n_chunk)[:, None] * CMP_STRIDE
    sj = np.arange(LANES)[None, :] * SEL_BLOCK
    ov = (ci < sj + SEL_BLOCK) & (ci + CMP_BLOCK > sj) & (np.arange(LANES)[None, :] < n_sel)
    ov = ov & (np.arange(n_chunk)[:, None] < n_chunk - 1)
    return jnp.asarray(ov, BF16)


def kernel(x, mem, positions, norm_mix_g, w_in, cmp_pe, cmp_w1, cmp_w2, conv_w, conv_b, conv_ln_g, conv_ln_b, conv_pw_w, conv_pw_b, pool_w, pool_scale, w_out, norm_xa_g, norm_mem_g, xa_wq, xa_wkv, xa_wo, norm_ffn_g, peer_wq, peer_subkeys, peer_u, peer_v, final_g):
    B, S, D = x.shape
    T = B * S
    depth = w_in.shape[0]
    M = mem.shape[1]
    dc = conv_w.shape[-1]
    n_chunk = S // CMP_STRIDE
    n_sel = S // SEL_BLOCK
    assert S % max(TM_PROJ, TS_MIX, KC_SEL, TM_PEER) == 0 and SEL_TOPN <= n_sel <= LANES and D % LANES == 0
    assert S >= WINDOW + TQ_NSA and peer_u.shape[1] % sum(EXPERT_SUBS) == 0

    cos_t, sa_t, sb_t = _rope_tables(positions)
    overlap = _overlap_matrix(n_chunk, n_sel)
    row = lambda v: v.reshape(1, -1)
    x2 = x.reshape(T, D)
    mem2 = mem.reshape(B * M, D)
    n_gate = 3 * NSA_HEADS

    w_pad = jnp.concatenate([w_in[:, :, :D_QKV + n_gate], jnp.zeros((depth, D, LANES - n_gate), F32),
                             w_in[:, :, D_QKV + n_gate:]], axis=2).astype(BF16)
    pg = dc // len(POOL_WINDOWS)
    plw = jnp.zeros((depth, dc, dc), F32)
    for gi in range(len(POOL_WINDOWS)):
        plw = plw.at[:, gi * pg:(gi + 1) * pg, gi * pg:(gi + 1) * pg].set(pool_w[:, gi])
    plw, conv_pw, wo = plw.astype(BF16), conv_pw_w.astype(BF16), w_out.astype(BF16)
    wkv, wq, wxo = xa_wkv.astype(BF16), xa_wq.astype(BF16), xa_wo.astype(BF16)
    pwq_t = jnp.swapaxes(peer_wq, 1, 2).astype(BF16)
    psk, pu = peer_subkeys.astype(BF16), peer_u.astype(BF16)
    pv_t = jnp.swapaxes(peer_v, 1, 2).astype(BF16)

    for l in range(depth):
        q, kvc, kv, misc = _inproj(x2, row(norm_mix_g[l]), w_pad[l], cos_t, sa_t, sb_t, tm=TM_PROJ)
        xc = kvc.reshape(B, n_chunk, CMP_STRIDE * 2 * LANES)
        cmp = _compress(xc, *_compress_weights(cmp_pe[l], cmp_w1[l], cmp_w2[l]))
        o_nsa = _nsa(q, cmp[0], cmp[1], kv.reshape(B, S, -1), misc, overlap, B, S, tq=TQ_NSA)
        x2 = _mixout(misc, o_nsa, x2, conv_w[l], row(conv_b[l]), row(conv_ln_g[l]), row(conv_ln_b[l]),
                     conv_pw[l], row(conv_pw_b[l]), plw[l], row(pool_scale[l]), wo[l], B, S, ts=TS_MIX)
        memkv = _norm_matmul(mem2, row(norm_mem_g[l]), wkv[l], tm=TM_MEM, out_dtype=BF16)
        x2 = _xattn(x2, row(norm_xa_g[l]), wq[l], memkv.reshape(B, M, 2 * D), wxo[l], B, S, tm=TM_PROJ)
        hn, c1, g1, r2, g2 = _peer_route(x2, row(norm_ffn_g[l]), pwq_t[l], psk[l], tm=TM_PEER)
        x2 = _peer_expert(hn, c1, g1, r2, g2, pu, pv_t, l, x2,
                          row(final_g) if l == depth - 1 else None, tm=TM_PEER, subs=EXPERT_SUBS)
    return x2.reshape(B, S, D)
```

```python
import functools
import math

import jax
import jax.numpy as jnp
import numpy as np
from jax import lax
from jax.experimental import pallas as pl
from jax.experimental.pallas import tpu as pltpu

F32 = jnp.float32
BF16 = jnp.bfloat16

NSA_HEADS = 8
NSA_KV_HEADS = 2
NSA_GROUP = NSA_HEADS // NSA_KV_HEADS
HEAD_DIM = 64
D_NSA = NSA_HEADS * HEAD_DIM
D_KV = NSA_KV_HEADS * HEAD_DIM
ROT_DIM = HEAD_DIM // 4
ROT_HALF = ROT_DIM // 2
ROPE_THETA = 500000.0
CMP_BLOCK = 32
CMP_STRIDE = 16
SEL_BLOCK = 64
SEL_TOPN = 16
WINDOW = 512
CONV_WIDTH = 31
POOL_WINDOWS = (2, 4, 8, 16)
XA_HEADS = 4
PEER_HEADS = 8
PEER_KEYS = 128
PEER_TOPK = 16
EPS = 1e-6
NEG = -1e30
FORCE = 1e4
LOWEST = -3.0e38

LANES = 128
SUBLANES = 8
VMEM_LIMIT = 56 * 1024 * 1024

NT_DIMS = (((1,), (1,)), ((), ()))

TM_PROJ = 512
TM_MEM = 256
TQ_NSA = 256
TQ_WIN = 128
KC_SEL = 512
TS_MIX = 512
TM_PEER = 512
EXPERT_SUBS = (256, 768, 768, 256)
EXPERT_LOOKAHEAD = 3


def _params(*sem):
    return pltpu.CompilerParams(dimension_semantics=sem, vmem_limit_bytes=VMEM_LIMIT)


def _full(shape):
    nd = len(shape)
    return pl.BlockSpec(shape, lambda *_: (0,) * nd)


def _rms(x, g):
    return x * lax.rsqrt(jnp.mean(x * x, axis=-1, keepdims=True) + EPS) * g


def _gelu(x):
    c = math.sqrt(2.0 / math.pi)
    return 0.5 * x * (1.0 + jnp.tanh(c * (x + 0.044715 * (x * x * x))))


def _dot(a, b):
    return jnp.dot(a, b, preferred_element_type=F32)


def _dot_nt(a, b):
    return lax.dot_general(a, b, NT_DIMS, preferred_element_type=F32)


N_ROPE_Q = D_NSA // LANES
N_KV_CHUNKS = 6
D_QKV = D_NSA + N_KV_CHUNKS * D_KV
N_KV_OUT = 2 + 2 * NSA_KV_HEADS


def _inproj_kernel(x_ref, g_ref, w_ref, cos_ref, sa_ref, sb_ref, q_ref, kvc_ref, kv_ref, misc_ref):
    y = _rms(x_ref[...], g_ref[...])
    z = _dot(y.astype(BF16), w_ref[...])
    cos, sa, sb = cos_ref[...], sa_ref[...], sb_ref[...]

    def rope(c):
        return c * cos + pltpu.roll(c, ROT_HALF, 1) * sa + pltpu.roll(c, LANES - ROT_HALF, 1) * sb

    scale = HEAD_DIM ** -0.5
    for j in range(N_ROPE_Q):
        q_ref[:, j * LANES:(j + 1) * LANES] = (rope(z[:, j * LANES:(j + 1) * LANES]) * scale).astype(BF16)
    lane = lax.broadcasted_iota(jnp.int32, (z.shape[0], LANES), 1)
    out = 0
    for j in range(N_KV_CHUNKS):
        c = z[:, D_NSA + j * LANES:D_NSA + (j + 1) * LANES]
        if j % 2 == 0:
            c = rope(c)
        if j < 2:
            kvc_ref[:, j * LANES:(j + 1) * LANES] = c.astype(BF16)
        elif j % 2 == 0:
            kv_ref[:, out * LANES:(out + 1) * LANES] = c.astype(BF16)
            out += 1
    for j in (3, 5):
        c = z[:, D_NSA + j * LANES:D_NSA + (j + 1) * LANES]
        for k in range(NSA_KV_HEADS):
            own = (lane >= HEAD_DIM) if k == 1 else (lane < HEAD_DIM)
            kv_ref[:, out * LANES:(out + 1) * LANES] = jnp.where(own, c, 1.0).astype(BF16)
            out += 1
    misc_ref[...] = z[:, D_QKV:]


def _inproj(x2, g, w_pad, cos_t, sa_t, sb_t, tm):
    T, D = x2.shape
    n = w_pad.shape[1]
    n_misc = n - D_QKV
    row = lambda w: pl.BlockSpec((tm, w), lambda i: (i, 0))
    return pl.pallas_call(
        _inproj_kernel,
        grid=(T // tm,),
        in_specs=[row(D), _full((1, D)), _full((D, n)), row(LANES), row(LANES), row(LANES)],
        out_specs=[row(D_NSA), row(2 * LANES), row(N_KV_OUT * LANES), row(n_misc)],
        out_shape=[jax.ShapeDtypeStruct((T, D_NSA), BF16),
                   jax.ShapeDtypeStruct((T, 2 * LANES), BF16),
                   jax.ShapeDtypeStruct((T, N_KV_OUT * LANES), BF16),
                   jax.ShapeDtypeStruct((T, n_misc), F32)],
        compiler_params=_params("parallel"),
    )(x2, g, w_pad, cos_t, sa_t, sb_t)


def _compress_kernel(x_ref, pe_ref, w1_ref, w2_ref, o_ref):
    x = x_ref[0].astype(F32)
    n = x.shape[0]
    out = jnp.zeros((n, LANES), F32)
    for hd in range(NSA_KV_HEADS):
        a = _dot((x + pe_ref[0, hd, 0:1]).astype(BF16), w1_ref[0, hd, 0])
        b = _dot((x + pe_ref[0, hd, 1:2]).astype(BF16), w1_ref[0, hd, 1])
        hid = _gelu(a + pltpu.roll(b, n - 1, 0))
        out = out + _dot(hid.astype(BF16), w2_ref[0, hd])
    o_ref[0, 0] = out.astype(BF16)


def _compress(xc, pe, w1, w2):
    nb, nch, width = xc.shape
    hid = w1.shape[-1]
    return pl.pallas_call(
        _compress_kernel,
        grid=(2, nb),
        in_specs=[pl.BlockSpec((1, nch, width), lambda w, i: (i, 0, 0)),
                  pl.BlockSpec((1, NSA_KV_HEADS, 2, width), lambda w, i: (w, 0, 0, 0)),
                  pl.BlockSpec((1, NSA_KV_HEADS, 2, width, hid), lambda w, i: (w, 0, 0, 0, 0)),
                  pl.BlockSpec((1, NSA_KV_HEADS, hid, LANES), lambda w, i: (w, 0, 0, 0))],
        out_specs=pl.BlockSpec((1, 1, nch, LANES), lambda w, i: (w, i, 0, 0)),
        out_shape=jax.ShapeDtypeStruct((2, nb, nch, LANES), BF16),
        compiler_params=_params("parallel", "parallel"),
    )(xc, pe, w1, w2)


def _compress_weights(pe, w1, w2):
    hid = w1.shape[-1]
    eye_t = jnp.eye(2, dtype=F32)[:, None, None, None, :, None, None]
    eye_h = jnp.eye(NSA_KV_HEADS, dtype=F32)[None, :, None, None, None, :, None]
    per = (2, 1, 2, CMP_STRIDE, 1, 1, HEAD_DIM)
    width = CMP_STRIDE * 2 * NSA_KV_HEADS * HEAD_DIM
    pe_big = (pe.reshape(per) * eye_t * eye_h).reshape(2, NSA_KV_HEADS, 2, width)
    w1_big = (w1.reshape(per + (hid,)) * eye_t[..., None] * eye_h[..., None]).reshape(2, NSA_KV_HEADS, 2, width, hid)
    eye_o = jnp.eye(NSA_KV_HEADS, dtype=F32)[None, :, None, :, None]
    w2_big = (w2[:, None, :, None, :] * eye_o).reshape(2, NSA_KV_HEADS, hid, NSA_KV_HEADS * HEAD_DIM)
    return pe_big, w1_big.astype(BF16), w2_big.astype(BF16)


def _nsa_kernel(q_ref, kcmp_ref, vcmp_ref, ks_ref, kw_ref, vs0_ref, vs1_ref, vw0_ref, vw1_ref, gate_ref, ov_ref,
                o_ref, s_sc, mrun_sc, acc_sc, *, tq, kc_sel, kc_win):
    t0 = pl.program_id(1) * tq
    rows = NSA_GROUP * tq
    lane = lax.broadcasted_iota(jnp.int32, (tq, LANES), 1)
    lane_r = lax.broadcasted_iota(jnp.int32, (rows, LANES), 1)
    tpos = t0 + lax.broadcasted_iota(jnp.int32, (tq, 1), 0)
    gates = jax.nn.sigmoid(gate_ref[...])
    n_cmp = kcmp_ref.shape[1]
    n_sel = ks_ref.shape[1] // SEL_BLOCK
    vs_refs, vw_refs = (vs0_ref, vs1_ref), (vw0_ref, vw1_ref)

    def normalise(acc, own):
        return acc / jnp.where(own, pltpu.roll(acc, HEAD_DIM, 1), 1.0)

    def own_lanes(lanes, k):
        return (lanes >= HEAD_DIM) if k == 1 else (lanes < HEAD_DIM)

    def queries(k):
        parts = []
        for g in range(NSA_GROUP):
            hh = k * NSA_GROUP + g
            c = q_ref[:, (hh // 2) * LANES:(hh // 2 + 1) * LANES].astype(F32)
            if hh % 2 != k:
                c = pltpu.roll(c, HEAD_DIM, 1)
            parts.append(jnp.where(own_lanes(lane, k), c, 0.0))
        return jnp.concatenate(parts, axis=0).astype(BF16)

    def compressed(qs):
        s = _dot_nt(qs, kcmp_ref[0]).reshape(NSA_GROUP, tq, n_cmp)
        cmp_end = lax.broadcasted_iota(jnp.int32, (tq, n_cmp), 1) * CMP_STRIDE + (CMP_BLOCK - 1)
        valid = cmp_end <= tpos
        s = s + jnp.where(valid, 0.0, NEG)[None]
        e = jnp.exp(s - jnp.max(s, axis=-1, keepdims=True)) * valid.astype(F32)[None]
        l = jnp.sum(e, axis=-1, keepdims=True)
        p3 = e * (1.0 / jnp.where(l > 0.0, l, 1.0))
        o_cmp = _dot(p3.reshape(rows, n_cmp).astype(BF16), vcmp_ref[0])

        psum = p3[0] + p3[1] + p3[2] + p3[3]
        ov = ov_ref[...]
        p_hi = psum.astype(BF16)
        r1 = psum - p_hi.astype(F32)
        p_mid = r1.astype(BF16)
        p_lo = (r1 - p_mid.astype(F32)).astype(BF16)
        imp = _dot(p_hi, ov) + _dot(p_mid, ov) + _dot(p_lo, ov)
        imp = imp.T[:n_sel]
        blk = lax.broadcasted_iota(jnp.int32, (n_sel, tq), 0)
        tpos_t = t0 + lax.broadcasted_iota(jnp.int32, (1, tq), 1)
        cur = tpos_t // SEL_BLOCK
        forced = (blk == 0) | (blk == cur) | (blk == cur - 1)
        imp = jnp.where(blk * SEL_BLOCK <= tpos_t, jnp.where(forced, FORCE, imp), NEG)
        blk_f = blk.astype(F32)
        for _ in range(SEL_TOPN):
            m = jnp.max(imp, axis=0, keepdims=True)
            first = jnp.min(jnp.where(imp == m, blk_f, float(n_sel)), axis=0, keepdims=True)
            imp = jnp.where(blk_f == first, LOWEST, imp)
        sel_t = jnp.concatenate([(imp == LOWEST).astype(F32), jnp.zeros((LANES - n_sel, tq), F32)], axis=0)
        return o_cmp, sel_t.T.astype(BF16)

    def window(k, qs):
        wq, kcw = TQ_WIN, WINDOW + TQ_WIN
        outs = []
        for piece in range(tq // wq):
            qh = jnp.concatenate([qs[g * tq + piece * wq:g * tq + (piece + 1) * wq] for g in range(NSA_GROUP)],
                                 axis=0)
            w0 = pl.multiple_of(jnp.maximum(t0 + (piece + 1) * wq - kcw, 0), wq)
            sw = _dot_nt(qh, kw_ref[0, pl.ds(w0, kcw), :]).reshape(NSA_GROUP, wq, kcw)
            wpos = w0 + lax.broadcasted_iota(jnp.int32, (wq, kcw), 1)
            tp = t0 + piece * wq + lax.broadcasted_iota(jnp.int32, (wq, 1), 0)
            in_win = (wpos <= tp) & (wpos > tp - WINDOW)
            sw = (sw + jnp.where(in_win, 0.0, NEG)[None]).reshape(NSA_GROUP * wq, kcw)
            pw = jnp.exp(sw - jnp.max(sw, axis=1, keepdims=True))
            acc = _dot(pw.astype(BF16), vw_refs[k][0, pl.ds(w0, kcw), :])
            lane_p = lax.broadcasted_iota(jnp.int32, (NSA_GROUP * wq, LANES), 1)
            outs.append(normalise(acc, own_lanes(lane_p, k)))
        return jnp.concatenate([o[g * wq:(g + 1) * wq] for g in range(NSA_GROUP) for o in outs], axis=0)

    heads = range(NSA_KV_HEADS)
    qs = [queries(k) for k in heads]
    o_cmp, sel = zip(*[compressed(qs[k]) for k in heads])
    o_win = [window(k, qs[k]) for k in heads]

    n_chunks = (t0 + tq + kc_sel - 1) // kc_sel
    mrun_sc[...] = jnp.full(mrun_sc.shape, NEG, F32)

    def scores(c, carry):
        k0 = pl.multiple_of(c * kc_sel, kc_sel)
        jrow = lax.broadcasted_iota(jnp.int32, (LANES, kc_sel), 0)
        kcol = lax.broadcasted_iota(jnp.int32, (LANES, kc_sel), 1)
        expand = (jrow == k0 // SEL_BLOCK + kcol // SEL_BLOCK).astype(BF16)
        causal = k0 + lax.broadcasted_iota(jnp.int32, (tq, kc_sel), 1) <= tpos
        kb = ks_ref[0, pl.ds(k0, kc_sel), :]
        for k in heads:
            seen = (_dot(sel[k], expand) > 0.5) & causal
            s = _dot_nt(qs[k], kb).reshape(NSA_GROUP, tq, kc_sel)
            s = (s + jnp.where(seen, 0.0, NEG)[None]).reshape(rows, kc_sel)
            s_sc[k, c] = s
            m = mrun_sc[k]
            for j in range(kc_sel // LANES):
                m = jnp.maximum(m, s[:, j * LANES:(j + 1) * LANES])
            mrun_sc[k] = m
        return carry

    lax.fori_loop(0, n_chunks, scores, 0)
    m_sel = [jnp.broadcast_to(jnp.max(mrun_sc[k], axis=1, keepdims=True), (rows, LANES)) for k in heads]
    acc_sc[...] = jnp.zeros(acc_sc.shape, F32)

    def weigh(c, carry):
        k0 = pl.multiple_of(c * kc_sel, kc_sel)
        for k in heads:
            s = s_sc[k, c]
            p = jnp.concatenate([jnp.exp(s[:, j * LANES:(j + 1) * LANES] - m_sel[k])
                                 for j in range(kc_sel // LANES)], axis=1)
            acc_sc[k] += _dot(p.astype(BF16), vs_refs[k][0, pl.ds(k0, kc_sel), :])
        return carry

    lax.fori_loop(0, n_chunks, weigh, 0)

    for k in heads:
        o_sel = normalise(acc_sc[k], own_lanes(lane_r, k))
        outs = []
        for g in range(NSA_GROUP):
            r = slice(g * tq, (g + 1) * tq)
            gi = (k * NSA_GROUP + g) * 3
            og = (gates[:, gi:gi + 1] * o_cmp[k][r] + gates[:, gi + 1:gi + 2] * o_sel[r]
                  + gates[:, gi + 2:gi + 3] * o_win[k][r])
            if g % 2 != k:
                og = pltpu.roll(og, HEAD_DIM, 1)
            outs.append(og)
        for j in range(NSA_GROUP // 2):
            chunk = jnp.where(lane < HEAD_DIM, outs[2 * j], outs[2 * j + 1])
            cj = k * (NSA_GROUP // 2) + j
            o_ref[:, cj * LANES:(cj + 1) * LANES] = chunk.astype(BF16)


def _nsa(q, kcmp, vcmp, kv, gates_misc, overlap, B, S, tq):
    kc_sel, kc_win = KC_SEL, WINDOW + tq
    nq = S // tq
    n_cmp = kcmp.shape[1]
    rows = NSA_GROUP * tq
    kvspec = lambda j: pl.BlockSpec((1, S, LANES), lambda b, i: (b, 0, j), pipeline_mode=pl.Buffered(1))
    return pl.pallas_call(
        functools.partial(_nsa_kernel, tq=tq, kc_sel=kc_sel, kc_win=kc_win),
        grid=(B, nq),
        in_specs=[pl.BlockSpec((tq, D_NSA), lambda b, i: (b * nq + i, 0)),
                  pl.BlockSpec((1, n_cmp, LANES), lambda b, i: (b, 0, 0)),
                  pl.BlockSpec((1, n_cmp, LANES), lambda b, i: (b, 0, 0)),
                  kvspec(0), kvspec(1), kvspec(2), kvspec(3), kvspec(4), kvspec(5),
                  pl.BlockSpec((tq, LANES), lambda b, i: (b * nq + i, 0)),
                  _full(overlap.shape)],
        out_specs=pl.BlockSpec((tq, D_NSA), lambda b, i: (b * nq + i, 0)),
        out_shape=jax.ShapeDtypeStruct((B * S, D_NSA), BF16),
        scratch_shapes=[pltpu.VMEM((NSA_KV_HEADS, S // kc_sel, rows, kc_sel), F32),
                        pltpu.VMEM((NSA_KV_HEADS, rows, LANES), F32), pltpu.VMEM((NSA_KV_HEADS, rows, LANES), F32)],
        compiler_params=_params("parallel", "parallel"),
    )(q, kcmp, vcmp, kv, kv, kv, kv, kv, kv, gates_misc, overlap)


HALO = 32


def _mixout_kernel(mc_ref, mp_ref, on_ref, x_ref, cw_ref, cb_ref, lg_ref, lb_ref, pw_ref, pb_ref, plw_ref,
                   pls_ref, wo_ref, o_ref, hbuf, pbuf, hsh, *, ts, dc):
    i = pl.program_id(1)
    first = i == 0
    off_a, off_b, off_p = LANES, LANES + dc, LANES + 2 * dc

    def glu(ref, r):
        return ref[r, off_a:off_a + dc] * jax.nn.sigmoid(ref[r, off_b:off_b + dc])

    tail = slice(ts - HALO, ts)
    hbuf[0:HALO, :] = jnp.where(first, 0.0, glu(mp_ref, tail))
    hbuf[HALO:, :] = glu(mc_ref, slice(None))
    pbuf[0:HALO, :] = jnp.where(first, 0.0, mp_ref[tail, off_p:off_p + dc])
    pcur = mc_ref[:, off_p:off_p + dc]
    pbuf[HALO:, :] = pcur

    shifted = ts + HALO - SUBLANES
    for r in range(1, SUBLANES):
        hsh[r, 0:shifted, :] = hbuf[pl.ds(r, shifted), :]
    acc = jnp.zeros((ts, dc), F32) + cb_ref[...]
    for w in range(CONV_WIDTH):
        q, r = divmod(HALO - (CONV_WIDTH - 1) + w, SUBLANES)
        tap = hbuf[pl.ds(q * SUBLANES, ts), :] if r == 0 else hsh[r, pl.ds(q * SUBLANES, ts), :]
        acc = acc + tap * cw_ref[w:w + 1, :]
    mu = jnp.mean(acc, axis=-1, keepdims=True)
    xc = acc - mu
    y = xc * lax.rsqrt(jnp.mean(xc * xc, axis=-1, keepdims=True) + EPS) * lg_ref[...] + lb_ref[...]
    y = y * jax.nn.sigmoid(y)
    o_conv = _dot(y.astype(BF16), pw_ref[...]) + pb_ref[...]

    tglob = i * ts + lax.broadcasted_iota(jnp.int32, (ts, dc), 0)
    lane = lax.broadcasted_iota(jnp.int32, (ts, dc), 1)
    pg = dc // len(POOL_WINDOWS)
    run = pcur
    d = 1
    mean = jnp.zeros((ts, dc), F32)
    for gi, w in enumerate(POOL_WINDOWS):
        while d < w:
            run = run + pbuf[pl.ds(HALO - d, ts), :]
            d += 1
        cnt = jnp.minimum(tglob + 1, w).astype(F32)
        mean = jnp.where(lane // pg == gi, run / cnt, mean)
    o_pool = _dot((mean - pcur).astype(BF16), plw_ref[...]) * pls_ref[...]

    dn = on_ref.shape[1]
    o_ref[...] = (x_ref[...] + _dot(on_ref[...], wo_ref[0:dn, :])
                  + _dot(o_conv.astype(BF16), wo_ref[dn:dn + dc, :])
                  + _dot(o_pool.astype(BF16), wo_ref[dn + dc:, :]))


def _mixout(misc, o_nsa, x2, cw, cb, lg, lb, pw, pb, plw, pls, wo, B, S, ts):
    T, D = x2.shape
    dc = cw.shape[1]
    ns = S // ts
    nm = misc.shape[1]
    cur = lambda w: pl.BlockSpec((ts, w), lambda b, i: (b * ns + i, 0))
    prev = pl.BlockSpec((ts, nm), lambda b, i: (b * ns + jnp.maximum(i - 1, 0), 0))
    return pl.pallas_call(
        functools.partial(_mixout_kernel, ts=ts, dc=dc),
        grid=(B, ns),
        in_specs=[cur(nm), prev, cur(o_nsa.shape[1]), cur(D), _full(cw.shape), _full(cb.shape), _full(lg.shape),
                  _full(lb.shape), _full(pw.shape), _full(pb.shape), _full(plw.shape), _full(pls.shape),
                  _full(wo.shape)],
        out_specs=cur(D),
        out_shape=jax.ShapeDtypeStruct((T, D), F32),
        scratch_shapes=[pltpu.VMEM((ts + HALO, dc), F32), pltpu.VMEM((ts + HALO, dc), F32),
                        pltpu.VMEM((SUBLANES, ts + HALO, dc), F32)],
        compiler_params=_params("parallel", "parallel"),
    )(misc, misc, o_nsa, x2, cw, cb, lg, lb, pw, pb, plw, pls, wo)


def _norm_matmul_kernel(x_ref, g_ref, w_ref, o_ref):
    o_ref[...] = _dot(_rms(x_ref[...], g_ref[...]).astype(BF16), w_ref[...]).astype(o_ref.dtype)


def _norm_matmul(x2, g, w, tm, out_dtype):
    T, D = x2.shape
    n = w.shape[1]
    return pl.pallas_call(
        _norm_matmul_kernel,
        grid=(T // tm,),
        in_specs=[pl.BlockSpec((tm, D), lambda i: (i, 0)), _full((1, D)), _full((D, n))],
        out_specs=pl.BlockSpec((tm, n), lambda i: (i, 0)),
        out_shape=jax.ShapeDtypeStruct((T, n), out_dtype),
        compiler_params=_params("parallel"),
    )(x2, g, w)


def _xattn_kernel(x_ref, g_ref, wq_ref, k_ref, v_ref, wo_ref, o_ref):
    x = x_ref[...]
    D = x.shape[1]
    dh = D // XA_HEADS
    q = _dot(_rms(x, g_ref[...]).astype(BF16), wq_ref[...]) * (dh ** -0.5)
    outs = []
    for h in range(XA_HEADS):
        c = slice(h * dh, (h + 1) * dh)
        s = _dot_nt(q[:, c].astype(BF16), k_ref[0, :, c])
        e = jnp.exp(s - jnp.max(s, axis=-1, keepdims=True))
        p = e * (1.0 / jnp.sum(e, axis=-1, keepdims=True))
        outs.append(_dot(p.astype(BF16), v_ref[0, :, c]))
    o = jnp.concatenate(outs, axis=1)
    o_ref[...] = x + _dot(o.astype(BF16), wo_ref[...])


def _xattn(x2, g, wq, memkv, wo, B, S, tm):
    T, D = x2.shape
    ns = S // tm
    M = memkv.shape[1]
    return pl.pallas_call(
        _xattn_kernel,
        grid=(B, ns),
        in_specs=[pl.BlockSpec((tm, D), lambda b, i: (b * ns + i, 0)), _full((1, D)), _full((D, D)),
                  pl.BlockSpec((1, M, D), lambda b, i: (b, 0, 0)),
                  pl.BlockSpec((1, M, D), lambda b, i: (b, 0, 1)),
                  _full((D, D))],
        out_specs=pl.BlockSpec((tm, D), lambda b, i: (b * ns + i, 0)),
        out_shape=jax.ShapeDtypeStruct((T, D), F32),
        compiler_params=_params("parallel", "parallel"),
    )(x2, g, wq, memkv, memkv, wo)


PAIR_LIST = tuple((a, b) for a in range(PEER_TOPK) for b in range(PEER_TOPK) if (a + 1) * (b + 1) <= PEER_TOPK)
N_PAIR_ROWS = -(-len(PAIR_LIST) // SUBLANES) * SUBLANES
PAIR_COUNT = tuple(PEER_TOPK // (a + 1) for a in range(PEER_TOPK))
PAIR_START = tuple(sum(PAIR_COUNT[:a]) for a in range(PEER_TOPK))


def _top_rows_exact(v, n):
    R = v.shape[0]
    ridx = lax.broadcasted_iota(jnp.int32, v.shape, 0).astype(F32)
    rank = jnp.full(v.shape, float(n), F32)
    vals = []
    for r in range(n):
        m = jnp.max(v, axis=0, keepdims=True)
        first = jnp.min(jnp.where(v == m, ridx, float(R)), axis=0, keepdims=True)
        taken = ridx == first
        v = jnp.where(taken, LOWEST, v)
        rank = jnp.where(taken, float(r), rank)
        vals.append(m)
    return vals, rank


def _top_rows_distinct(v, n):
    rank = jnp.full(v.shape, float(n), F32)
    vals = []
    for r in range(n):
        m = jnp.max(v, axis=0, keepdims=True)
        taken = v == m
        v = jnp.where(taken, LOWEST, v)
        rank = jnp.where(taken, float(r), rank)
        vals.append(m)
    return vals, rank


def _peer_route_kernel(x_ref, g_ref, wqt_ref, sk_ref, hn_ref, c1_ref, g1_ref, r2_ref, g2_ref, s_sc, top_sc,
                       rank_sc, cand_sc, cw_sc):
    hn_ref[...] = _rms(x_ref[...], g_ref[...]).astype(BF16)
    nk, n = PEER_KEYS, PEER_TOPK
    tm = x_ref.shape[0]

    def scores_into(h):
        r0 = pl.multiple_of(h * 2 * nk, 2 * nk)
        qt = _dot_nt(wqt_ref[pl.ds(r0, 2 * nk), :], hn_ref[...]).astype(BF16)
        for half in range(2):
            s_sc[pl.ds(r0 + half * nk, nk), :] = _dot(sk_ref[half], qt[half * nk:(half + 1) * nk])

    def put(li, vals, rank):
        for r in range(n):
            top_sc[pl.ds(li * n + r, 1), :] = vals[r]
        rank_sc[pl.ds(pl.multiple_of(li * nk, nk), nk), :] = rank

    def one_head(h, carry):
        scores = [s_sc[pl.ds(pl.multiple_of((2 * h + half) * nk, nk), nk), :] for half in range(2)]
        scores_into(jnp.minimum(h + 1, PEER_HEADS - 1))
        ranked = jnp.zeros((1, tm), F32)
        for half in range(2):
            li = 2 * h + half
            vals, rank = _top_rows_distinct(scores[half], n)
            put(li, vals, rank)
            ranked = jnp.maximum(ranked, jnp.sum((rank < float(n)).astype(F32), axis=0, keepdims=True))

        @pl.when(jnp.max(ranked) > float(n))
        def _():
            for half in range(2):
                li = 2 * h + half
                put(li, *_top_rows_exact(s_sc[pl.ds(pl.multiple_of(li * nk, nk), nk), :], n))

        return carry

    scores_into(0)
    lax.fori_loop(0, PEER_HEADS, one_head, 0)

    cand_sc[...] = jnp.full(cand_sc.shape, LOWEST, F32)
    cw_sc[...] = jnp.zeros(cw_sc.shape, F32)
    for h in range(PEER_HEADS):
        l1, l2 = 2 * h, 2 * h + 1
        top1 = [top_sc[l1 * n + a:l1 * n + a + 1, :] for a in range(n)]
        top2 = [top_sc[l2 * n + a:l2 * n + a + 1, :] for a in range(n)]
        e1 = [jnp.exp(t - top1[0]) for t in top1]
        e2 = [jnp.exp(t - top2[0]) for t in top2]
        for r, (a, b) in enumerate(PAIR_LIST):
            cand_sc[h, r:r + 1, :] = top1[a] + top2[b]
            cw_sc[h, r:r + 1, :] = e1[a] * e2[b]
        _, crank = _top_rows_exact(cand_sc[h], n)
        chosen = (crank < float(n)).astype(F32)
        z = jnp.sum(chosen * cw_sc[h], axis=0, keepdims=True)
        rank1 = rank_sc[l1 * nk:(l1 + 1) * nk, :].astype(BF16)
        count1 = jnp.zeros(rank1.shape, BF16)
        for a in range(n):
            n_a = jnp.sum(chosen[PAIR_START[a]:PAIR_START[a] + PAIR_COUNT[a]], axis=0, keepdims=True)
            count1 = jnp.where(rank1 == float(a), n_a.astype(BF16), count1)
        rows = slice(h * nk, (h + 1) * nk)
        c1_ref[rows, :] = count1.astype(F32)
        g1_ref[rows, :] = jnp.exp(s_sc[l1 * nk:(l1 + 1) * nk, :] - top1[0]) * (1.0 / z)
        r2_ref[rows, :] = rank_sc[l2 * nk:(l2 + 1) * nk, :].astype(BF16)
        g2_ref[rows, :] = jnp.exp(s_sc[l2 * nk:(l2 + 1) * nk, :] - top2[0]).astype(BF16)


def _peer_route(x2, g, wqt, sk, tm):
    T, D = x2.shape
    nr = PEER_HEADS * PEER_KEYS
    col = pl.BlockSpec((nr, tm), lambda i: (0, i))
    return pl.pallas_call(
        _peer_route_kernel,
        grid=(T // tm,),
        in_specs=[pl.BlockSpec((tm, D), lambda i: (i, 0)), _full((1, D)), _full(wqt.shape), _full(sk.shape)],
        out_specs=[pl.BlockSpec((tm, D), lambda i: (i, 0)), col, col, col, col],
        out_shape=[jax.ShapeDtypeStruct((T, D), BF16), jax.ShapeDtypeStruct((nr, T), F32),
                   jax.ShapeDtypeStruct((nr, T), F32), jax.ShapeDtypeStruct((nr, T), BF16),
                   jax.ShapeDtypeStruct((nr, T), BF16)],
        scratch_shapes=[pltpu.VMEM((2 * nr, tm), F32),
                        pltpu.VMEM((2 * PEER_HEADS * PEER_TOPK, tm), F32),
                        pltpu.VMEM((2 * nr, tm), F32), pltpu.VMEM((PEER_HEADS, N_PAIR_ROWS, tm), F32),
                        pltpu.VMEM((PEER_HEADS, N_PAIR_ROWS, tm), F32)],
        compiler_params=_params("parallel"),
    )(x2, g, wqt, sk)


def _peer_expert_kernel(*refs, subs, final_norm):
    hn_ref, c1_ref, g1_ref, r2_ref, g2_ref, u_ref, vt_ref, x_ref = refs[:8]
    fg_ref = refs[8] if final_norm else None
    o_ref, acc_sc, wa_sc, act_sc = refs[-4:]
    c = pl.program_id(1)
    nk = PEER_KEYS
    te = sum(subs)
    offs = [sum(subs[:j]) for j in range(len(subs))]
    tm = hn_ref.shape[0]
    slots = act_sc.shape[0]

    @pl.when(c == 0)
    def _():
        acc_sc[...] = jnp.zeros(acc_sc.shape, F32)

    def pre_act(j):
        o, rows = offs[j], subs[j]
        act_sc[j % slots, 0:rows, :] = _dot_nt(u_ref[o:o + rows, :], hn_ref[...]).astype(BF16)

    for j in range(min(EXPERT_LOOKAHEAD, len(subs))):
        pre_act(j)
    for j, (o, rows) in enumerate(zip(offs, subs)):
        if j + EXPERT_LOOKAHEAD < len(subs):
            pre_act(j + EXPERT_LOOKAHEAD)
        for il in range(rows // nk):
            i = c * (te // nk) + o // nk + il
            w = jnp.zeros((nk, tm), BF16)
            for h in range(PEER_HEADS):
                count = c1_ref[pl.ds(h * nk + i, 1), :].astype(BF16)
                g1row = g1_ref[pl.ds(h * nk + i, 1), :].astype(BF16)
                hr = slice(h * nk, (h + 1) * nk)
                w = w + jnp.where(r2_ref[hr, :] < count, g2_ref[hr, :], 0.0) * g1row
            er = slice(il * nk, (il + 1) * nk)
            wa_sc[o + il * nk:o + (il + 1) * nk, :] = w * _gelu(act_sc[j % slots, er, :])
        acc_sc[...] += _dot(vt_ref[:, o:o + rows], wa_sc[o:o + rows, :])

    @pl.when(c == pl.num_programs(1) - 1)
    def _():
        out = x_ref[...] + acc_sc[...].T
        o_ref[...] = _rms(out, fg_ref[...]) if final_norm else out


def _peer_expert(hn, c1, g1, r2, g2, u, vt, layer, x2, final_g, tm, subs):
    T, D = x2.shape
    ne = u.shape[1]
    te = sum(subs)
    tok = pl.BlockSpec((c1.shape[0], tm), lambda i, c: (0, i))
    row = pl.BlockSpec((tm, D), lambda i, c: (i, 0))
    return pl.pallas_call(
        functools.partial(_peer_expert_kernel, subs=subs, final_norm=final_g is not None),
        grid=(T // tm, ne // te),
        in_specs=[row, tok, tok, tok, tok,
                  pl.BlockSpec((None, te, D), lambda i, c: (layer, c, 0)),
                  pl.BlockSpec((None, D, te), lambda i, c: (layer, 0, c)),
                  row] + ([] if final_g is None else [_full((1, D))]),
        out_specs=row,
        out_shape=jax.ShapeDtypeStruct((T, D), F32),
        scratch_shapes=[pltpu.VMEM((D, tm), F32), pltpu.VMEM((te, tm), BF16),
                        pltpu.VMEM((EXPERT_LOOKAHEAD + 1, max(subs), tm), BF16)],
        compiler_params=_params("parallel", "arbitrary"),
    )(hn, c1, g1, r2, g2, u, vt, x2, *([] if final_g is None else [final_g]))


def _rope_tables(positions):
    B, S = positions.shape
    freqs = ROPE_THETA ** (-jnp.arange(ROT_HALF, dtype=F32) * 2.0 / ROT_DIM)
    ang = positions.astype(F32)[:, :, None] * freqs
    cos, sin = jnp.cos(ang), jnp.sin(ang)
    ones = jnp.ones((B, S, HEAD_DIM - ROT_DIM), F32)
    zeros8 = jnp.zeros((B, S, ROT_HALF), F32)
    zeros = jnp.zeros((B, S, HEAD_DIM - ROT_DIM), F32)
    cos_h = jnp.concatenate([cos, cos, ones], axis=-1)
    sa_h = jnp.concatenate([zeros8, sin, zeros], axis=-1)
    sb_h = jnp.concatenate([-sin, zeros8, zeros], axis=-1)
    rep = LANES // HEAD_DIM
    tile = lambda t: jnp.tile(t, (1, 1, rep)).reshape(B * S, LANES)
    return tile(cos_h), tile(sa_h), tile(sb_h)


def _overlap_matrix(n_chunk, n_sel):
    ci = np.arange(n_chunk)[:, None] * CMP_STRIDE
    sj = np.arange(LANES)[None, :] * SEL_BLOCK
    ov = (ci < sj + SEL_BLOCK) & (ci + CMP_BLOCK > sj) & (np.arange(LANES)[None, :] < n_sel)
    ov = ov & (np.arange(n_chunk)[:, None] < n_chunk - 1)
    return jnp.asarray(ov, BF16)


def kernel(x, mem, positions, norm_mix_g, w_in, cmp_pe, cmp_w1, cmp_w2, conv_w, conv_b, conv_ln_g, conv_ln_b, conv_pw_w, conv_pw_b, pool_w, pool_scale, w_out, norm_xa_g, norm_mem_g, xa_wq, xa_wkv, xa_wo, norm_ffn_g, peer_wq, peer_subkeys, peer_u, peer_v, final_g):
    B, S, D = x.shape
    T = B * S
    depth = w_in.shape[0]
    M = mem.shape[1]
    dc = conv_w.shape[-1]
    n_chunk = S // CMP_STRIDE
    n_sel = S // SEL_BLOCK
    assert S % max(TM_PROJ, TS_MIX, KC_SEL, TM_PEER) == 0 and SEL_TOPN <= n_sel <= LANES and D % LANES == 0
    assert S >= WINDOW + TQ_NSA and peer_u.shape[1] % sum(EXPERT_SUBS) == 0

    cos_t, sa_t, sb_t = _rope_tables(positions)
    overlap = _overlap_matrix(n_chunk, n_sel)
    row = lambda v: v.reshape(1, -1)
    x2 = x.reshape(T, D)
    mem2 = mem.reshape(B * M, D)
    n_gate = 3 * NSA_HEADS

    w_pad = jnp.concatenate([w_in[:, :, :D_QKV + n_gate], jnp.zeros((depth, D, LANES - n_gate), F32),
                             w_in[:, :, D_QKV + n_gate:]], axis=2).astype(BF16)
    pg = dc // len(POOL_WINDOWS)
    plw = jnp.zeros((depth, dc, dc), F32)
    for gi in range(len(POOL_WINDOWS)):
        plw = plw.at[:, gi * pg:(gi + 1) * pg, gi * pg:(gi + 1) * pg].set(pool_w[:, gi])
    plw, conv_pw, wo = plw.astype(BF16), conv_pw_w.astype(BF16), w_out.astype(BF16)
    wkv, wq, wxo = xa_wkv.astype(BF16), xa_wq.astype(BF16), xa_wo.astype(BF16)
    pwq_t = jnp.swapaxes(peer_wq, 1, 2).astype(BF16)
    psk, pu = peer_subkeys.astype(BF16), peer_u.astype(BF16)
    pv_t = jnp.swapaxes(peer_v, 1, 2).astype(BF16)

    for l in range(depth):
        q, kvc, kv, misc = _inproj(x2, row(norm_mix_g[l]), w_pad[l], cos_t, sa_t, sb_t, tm=TM_PROJ)
        xc = kvc.reshape(B, n_chunk, CMP_STRIDE * 2 * LANES)
        cmp = _compress(xc, *_compress_weights(cmp_pe[l], cmp_w1[l], cmp_w2[l]))
        o_nsa = _nsa(q, cmp[0], cmp[1], kv.reshape(B, S, -1), misc, overlap, B, S, tq=TQ_NSA)
        x2 = _mixout(misc, o_nsa, x2, conv_w[l], row(conv_b[l]), row(conv_ln_g[l]), row(conv_ln_b[l]),
                     conv_pw[l], row(conv_pw_b[l]), plw[l], row(pool_scale[l]), wo[l], B, S, ts=TS_MIX)
        memkv = _norm_matmul(mem2, row(norm_mem_g[l]), wkv[l], tm=TM_MEM, out_dtype=BF16)
        x2 = _xattn(x2, row(norm_xa_g[l]), wq[l], memkv.reshape(B, M, 2 * D), wxo[l], B, S, tm=TM_PROJ)
        hn, c1, g1, r2, g2 = _peer_route(x2, row(norm_ffn_g[l]), pwq_t[l], psk[l], tm=TM_PEER)
        x2 = _peer_expert(hn, c1, g1, r2, g2, pu, pv_t, l, x2,
                          row(final_g) if l == depth - 1 else None, tm=TM_PEER, subs=EXPERT_SUBS)
    return x2.reshape(B, S, D)
```
